```python
import math
import jax
import jax.numpy as jnp
from jax import lax
import numpy as np

D_MODEL = 1024
BATCH = 8
SEQ = 8192
DEPTH = 1
DEC_BATCH = 128
DEC_SEQ = 8
PAST_LEN = 8192
PAGE_SIZE = 128

N_HEADS = 8
HEAD_DIM = 64
ATTN_W = N_HEADS * HEAD_DIM
IDX_HEADS = 4
IDX_DIM = 64
TOPK_MAX = 256
QBLK = 128
REL_BUCKETS = 32
REL_MAX_EXACT = 16
REL_MAX_DIST = 128
POOL_GROUPS = 4
POOL_GC = 128
POOL_W = POOL_GROUPS * POOL_GC
POOL_WINDOWS = (2, 4, 8, 16)
POOL_CTX = 15
IN_WIDTHS = (ATTN_W, ATTN_W, ATTN_W, IDX_HEADS * IDX_DIM, IDX_DIM, IDX_HEADS, POOL_W, D_MODEL, D_MODEL)
D_IN = 3 * ATTN_W + IDX_HEADS * IDX_DIM + IDX_DIM + IDX_HEADS + POOL_W + 2 * D_MODEL
N_GROUPS = 4
EXPERTS_PER_GROUP = 8
N_EXPERTS = N_GROUPS * EXPERTS_PER_GROUP
TOP_K_FINE = 2
D_EXPERT = 512
MOE_BLOCK = 128
RMS_EPS = 1e-6

kernel_name = "hybrid_dsa_pool_hmoe_step"


def rmsnorm(x, g):
    xf = x.astype(jnp.float32)
    xf = xf * lax.rsqrt(jnp.mean(xf * xf, axis=-1, keepdims=True) + RMS_EPS)
    return (xf * g.astype(jnp.float32)).astype(x.dtype)


def input_projection(x, ln_g, w_in):
    h = rmsnorm(x, ln_g)
    p = h @ w_in
    points = [int(c) for c in np.cumsum(IN_WIDTHS)[:-1]]
    q, k, v, qi, ki, wi, u, ga, gp = jnp.split(p, points, axis=-1)
    n, t = x.shape[:2]
    q = q.reshape(n, t, N_HEADS, HEAD_DIM)
    k = k.reshape(n, t, N_HEADS, HEAD_DIM)
    v = v.reshape(n, t, N_HEADS, HEAD_DIM)
    qi = qi.reshape(n, t, IDX_HEADS, IDX_DIM)
    return q, k, v, qi, ki, wi, u, ga, gp


def t5_bucket(dist):
    n = jnp.maximum(dist, 0)
    nf = jnp.maximum(n, 1).astype(jnp.float32)
    large = REL_MAX_EXACT + (jnp.log(nf / REL_MAX_EXACT) / math.log(REL_MAX_DIST / REL_MAX_EXACT)
                             * (REL_BUCKETS - REL_MAX_EXACT)).astype(jnp.int32)
    large = jnp.minimum(large, REL_BUCKETS - 1)
    return jnp.where(n < REL_MAX_EXACT, n, large)


def index_scores(qi, ki, wi):
    dots = jnp.einsum('nthd,nld->nthl', qi, ki).astype(jnp.float32) * (IDX_DIM ** -0.5)
    return jnp.einsum('nth,nthl->ntl', wi.astype(jnp.float32), jax.nn.relu(dots))


def select_topk(scores, mask, topk):
    masked = jnp.where(mask[None], scores, -jnp.inf)
    _, idx = lax.top_k(masked, topk)
    return idx


def attend(q, kg, vg, t_pos, idx, rel_bias):
    logits = jnp.einsum('nthd,ntkhd->nthk', q, kg).astype(jnp.float32) * (HEAD_DIM ** -0.5)
    dist = t_pos[None, :, None] - idx
    bias = rel_bias[t5_bucket(dist)].astype(jnp.float32)
    logits = logits + jnp.transpose(bias, (0, 1, 3, 2))
    valid = (dist >= 0)[:, :, None, :]
    logits = jnp.where(valid, logits, -jnp.inf)
    p = jax.nn.softmax(logits, axis=-1)
    return jnp.einsum('nthk,ntkhd->nthd', p.astype(vg.dtype), vg)


def sparse_attn_prompt(q, k, v, qi, ki, wi, rel_bias):
    b, s = q.shape[:2]
    topk = min(TOPK_MAX, s // 4)
    nblk = s // QBLK
    bidx = jnp.arange(b)[:, None, None]
    key_pos = jnp.arange(s)

    def blocks(a):
        return jnp.swapaxes(a.reshape((b, nblk, QBLK) + a.shape[2:]), 0, 1)

    def one_block(args):
        qs, qis, wis, start = args
        t = start + jnp.arange(QBLK)
        sc = index_scores(qis, ki, wis)
        idx = select_topk(sc, key_pos[None, :] <= t[:, None], topk)
        kg = k[bidx, idx]
        vg = v[bidx, idx]
        return attend(qs, kg, vg, t, idx, rel_bias)

    starts = jnp.arange(nblk) * QBLK
    out = lax.map(one_block, (blocks(q), blocks(qi), blocks(wi), starts))
    return jnp.swapaxes(out, 0, 1).reshape(b, s, ATTN_W)


def sparse_attn_sample(q, k_new, v_new, qi, ki_new, wi, cache_k, cache_v, cache_kidx, layer,
                       page_table, rel_bias):
    nb, t_new = q.shape[:2]
    past = page_table.shape[1] * PAGE_SIZE
    n_keys = past + t_new
    topk = min(TOPK_MAX, n_keys // 4)
    ki_past = cache_kidx[layer, page_table].reshape(nb, past, IDX_DIM)
    ki_all = jnp.concatenate([ki_past, ki_new], axis=1)
    t = past + jnp.arange(t_new)
    sc = index_scores(qi, ki_all, wi)
    idx = select_topk(sc, jnp.arange(n_keys)[None, :] <= t[:, None], topk)
    bidx = jnp.arange(nb)[:, None, None]
    pidx = jnp.minimum(idx, past - 1)
    phys = page_table[bidx, pidx // PAGE_SIZE]
    off = pidx % PAGE_SIZE
    nidx = jnp.clip(idx - past, 0, t_new - 1)
    from_past = (idx < past)[..., None, None]
    kg = jnp.where(from_past, cache_k[layer, phys, off], k_new[bidx, nidx])
    vg = jnp.where(from_past, cache_v[layer, phys, off], v_new[bidx, nidx])
    out = attend(q, kg, vg, t, idx, rel_bias)
    return out.reshape(nb, t_new, ATTN_W)


def multiscale_pool(u, n_ctx, w_pool_map, pool_scale):
    n, t_all = u.shape[:2]
    uf = u.astype(jnp.float32)
    c = jnp.concatenate([jnp.zeros((n, 1, POOL_W), jnp.float32), jnp.cumsum(uf, axis=1)], axis=1)
    j = jnp.arange(n_ctx, t_all)
    outs = []
    for g, w in enumerate(POOL_WINDOWS):
        sl = slice(g * POOL_GC, (g + 1) * POOL_GC)
        lo = jnp.maximum(j + 1 - w, 0)
        cnt = (j + 1 - lo).astype(jnp.float32)[:, None]
        mean = (c[:, j + 1, sl] - c[:, lo, sl]) / cnt
        outs.append(mean - uf[:, n_ctx:, sl])
    d = jnp.concatenate(outs, axis=-1).reshape(n, t_all - n_ctx, POOL_GROUPS, POOL_GC)
    y = jnp.einsum('ntgc,gce->ntge', d, w_pool_map.astype(jnp.float32)).reshape(n, t_all - n_ctx, POOL_W)
    return (y * pool_scale.astype(jnp.float32)).astype(u.dtype)


def merge_branches(x, attn, pool, ga, gp, w_br_attn, w_br_pool, w_out):
    m = jax.nn.sigmoid(ga) * (attn @ w_br_attn) + jax.nn.sigmoid(gp) * (pool @ w_br_pool)
    return x + m @ w_out


def grouped_swiglu(x, experts, gates, w_gate, w_up, w_down):
    n = x.shape[0]
    a = n * TOP_K_FINE
    e = experts.reshape(-1)
    tok = jnp.arange(a, dtype=jnp.int32) // TOP_K_FINE
    order = jnp.argsort(e)
    e_sorted = e[order]
    counts = jnp.bincount(e, length=N_EXPERTS)
    starts = jnp.cumsum(counts) - counts
    padded = ((counts + MOE_BLOCK - 1) // MOE_BLOCK) * MOE_BLOCK
    pad_ends = jnp.cumsum(padded)
    pad_starts = pad_ends - padded
    dest_sorted = pad_starts[e_sorted] + (jnp.arange(a) - starts[e_sorted])
    nblk = (a + MOE_BLOCK - 1) // MOE_BLOCK + N_EXPERTS
    slot_tok = jnp.full((nblk * MOE_BLOCK,), n, jnp.int32).at[dest_sorted].set(tok[order])
    block_expert = jnp.minimum(jnp.searchsorted(pad_ends, jnp.arange(nblk) * MOE_BLOCK, side='right'),
                               N_EXPERTS - 1)
    x_pad = jnp.concatenate([x, jnp.zeros((1, x.shape[1]), x.dtype)], axis=0)[slot_tok]
    x_pad = x_pad.reshape(nblk, MOE_BLOCK, x.shape[1])

    def run(args):
        xb, eb = args
        hdn = jax.nn.silu(xb @ w_gate[eb]) * (xb @ w_up[eb])
        return hdn @ w_down[eb]

    yb = lax.map(run, (x_pad, block_expert)).reshape(nblk * MOE_BLOCK, x.shape[1])
    dest = jnp.zeros((a,), jnp.int32).at[order].set(dest_sorted.astype(jnp.int32))
    y = yb[dest].reshape(n, TOP_K_FINE, x.shape[1])
    return jnp.sum(y * gates[..., None].astype(y.dtype), axis=1)


def hier_moe(h, w_coarse, b_coarse, w_fine, b_fine, w_gate, w_up, w_down):
    shp = h.shape
    x = h.reshape(-1, shp[-1])
    lc = (x @ w_coarse + b_coarse).astype(jnp.float32)
    pc = jax.nn.softmax(lc, axis=-1)
    grp = jnp.argmax(lc, axis=-1)
    p_grp = jnp.take_along_axis(pc, grp[:, None], axis=1)[:, 0]
    lf = jnp.einsum('nd,gde->nge', x, w_fine) + b_fine
    lf_sel = jnp.take_along_axis(lf, grp[:, None, None], axis=1)[:, 0].astype(jnp.float32)
    pf = jax.nn.softmax(lf_sel, axis=-1)
    top_p, top_i = lax.top_k(pf, TOP_K_FINE)
    gates = p_grp[:, None] * top_p / jnp.sum(top_p, axis=-1, keepdims=True)
    experts = grp[:, None].astype(jnp.int32) * EXPERTS_PER_GROUP + top_i
    y = grouped_swiglu(x, experts, gates, w_gate, w_up, w_down)
    return y.reshape(shp).astype(h.dtype)


def setup_inputs(seed: int = 0) -> dict:
    key = jax.random.key(seed)
    ks = jax.random.split(key, 32)
    n_pages = PAST_LEN // PAGE_SIZE
    n_phys = (DEC_BATCH * n_pages * 5) // 4

    def nrm(k, shape, scale):
        return jax.random.normal(k, shape, jnp.float32) * scale

    page_table = jax.random.permutation(ks[0], n_phys)[:DEC_BATCH * n_pages]
    page_table = page_table.reshape(DEC_BATCH, n_pages).astype(jnp.int32)
    return {
        "x_prompt": nrm(ks[1], (BATCH, SEQ, D_MODEL), 1.0),
        "x_sample": nrm(ks[2], (DEC_BATCH, DEC_SEQ, D_MODEL), 1.0),
        "cache_k": nrm(ks[3], (DEPTH, n_phys, PAGE_SIZE, N_HEADS, HEAD_DIM), 1.0),
        "cache_v": nrm(ks[4], (DEPTH, n_phys, PAGE_SIZE, N_HEADS, HEAD_DIM), 1.0),
        "cache_kidx": nrm(ks[5], (DEPTH, n_phys, PAGE_SIZE, IDX_DIM), 1.0),
        "state_pool": nrm(ks[6], (DEPTH, DEC_BATCH, POOL_CTX, POOL_W), 1.0),
        "page_table": page_table,
        "rel_bias": nrm(ks[7], (REL_BUCKETS, N_HEADS), 0.5),
        "ln1_g": 1.0 + nrm(ks[8], (DEPTH, D_MODEL), 0.05),
        "w_in": nrm(ks[9], (DEPTH, D_MODEL, D_IN), D_MODEL ** -0.5),
        "w_pool_map": nrm(ks[10], (DEPTH, POOL_GROUPS, POOL_GC, POOL_GC), POOL_GC ** -0.5),
        "pool_scale": 1.0 + nrm(ks[11], (DEPTH, POOL_W), 0.1),
        "w_br_attn": nrm(ks[12], (DEPTH, ATTN_W, D_MODEL), ATTN_W ** -0.5),
        "w_br_pool": nrm(ks[13], (DEPTH, POOL_W, D_MODEL), POOL_W ** -0.5),
        "w_out": nrm(ks[14], (DEPTH, D_MODEL, D_MODEL), D_MODEL ** -0.5),
        "ln2_g": 1.0 + nrm(ks[15], (DEPTH, D_MODEL), 0.05),
        "w_coarse": nrm(ks[16], (DEPTH, D_MODEL, N_GROUPS), D_MODEL ** -0.5),
        "b_coarse": nrm(ks[17], (DEPTH, N_GROUPS), 0.01),
        "w_fine": nrm(ks[18], (DEPTH, N_GROUPS, D_MODEL, EXPERTS_PER_GROUP), D_MODEL ** -0.5),
        "b_fine": nrm(ks[19], (DEPTH, N_GROUPS, EXPERTS_PER_GROUP), 0.01),
        "w_gate": nrm(ks[20], (DEPTH, N_EXPERTS, D_MODEL, D_EXPERT), D_MODEL ** -0.5),
        "w_up": nrm(ks[21], (DEPTH, N_EXPERTS, D_MODEL, D_EXPERT), D_MODEL ** -0.5),
        "w_down": nrm(ks[22], (DEPTH, N_EXPERTS, D_EXPERT, D_MODEL), D_EXPERT ** -0.5),
        "lnf_g": 1.0 + nrm(ks[23], (D_MODEL,), 0.05),
    }


def reference(x_prompt, x_sample, cache_k, cache_v, cache_kidx, state_pool, page_table, rel_bias,
              ln1_g, w_in, w_pool_map, pool_scale, w_br_attn, w_br_pool, w_out, ln2_g,
              w_coarse, b_coarse, w_fine, b_fine, w_gate, w_up, w_down, lnf_g):
    xp, xs = x_prompt, x_sample
    kp_l, vp_l, kip_l, pp_l = [], [], [], []
    ks_l, vs_l, kis_l, ps_l = [], [], [], []
    for d in range(DEPTH):
        q, k, v, qi, ki, wi, u, ga, gp = input_projection(xp, ln1_g[d], w_in[d])
        attn = sparse_attn_prompt(q, k, v, qi, ki, wi, rel_bias)
        pool = multiscale_pool(u, 0, w_pool_map[d], pool_scale[d])
        xp = merge_branches(xp, attn, pool, ga, gp, w_br_attn[d], w_br_pool[d], w_out[d])
        xp = xp + hier_moe(rmsnorm(xp, ln2_g[d]), w_coarse[d], b_coarse[d], w_fine[d], b_fine[d],
                           w_gate[d], w_up[d], w_down[d])
        kp_l.append(k)
        vp_l.append(v)
        kip_l.append(ki)
        pp_l.append(u[:, -POOL_CTX:])
        q, k, v, qi, ki, wi, u, ga, gp = input_projection(xs, ln1_g[d], w_in[d])
        attn = sparse_attn_sample(q, k, v, qi, ki, wi, cache_k, cache_v, cache_kidx, d,
                                  page_table, rel_bias)
        buf = jnp.concatenate([state_pool[d].astype(u.dtype), u], axis=1)
        pool = multiscale_pool(buf, POOL_CTX, w_pool_map[d], pool_scale[d])
        xs = merge_branches(xs, attn, pool, ga, gp, w_br_attn[d], w_br_pool[d], w_out[d])
        xs = xs + hier_moe(rmsnorm(xs, ln2_g[d]), w_coarse[d], b_coarse[d], w_fine[d], b_fine[d],
                           w_gate[d], w_up[d], w_down[d])
        ks_l.append(k)
        vs_l.append(v)
        kis_l.append(ki)
        ps_l.append(buf[:, -POOL_CTX:])
    y_prompt = rmsnorm(xp, lnf_g)
    y_sample = rmsnorm(xs, lnf_g)
    k_prompt = jnp.stack(kp_l)
    v_prompt = jnp.stack(vp_l)
    kidx_prompt = jnp.stack(kip_l)
    pool_prompt = jnp.stack(pp_l)
    k_sample = jnp.stack(ks_l)
    v_sample = jnp.stack(vs_l)
    kidx_sample = jnp.stack(kis_l)
    pool_sample = jnp.stack(ps_l)
    return (y_prompt, y_sample, k_prompt, v_prompt, kidx_prompt, pool_prompt,
            k_sample, v_sample, kidx_sample, pool_sample)
```

```python
import functools
import math

import jax
import jax.numpy as jnp
import numpy as np
from jax import lax
from jax.experimental import pallas as pl
from jax.experimental.pallas import tpu as pltpu

D_MODEL = 1024
N_HEADS = 8
HEAD_DIM = 64
ATTN_W = N_HEADS * HEAD_DIM
IDX_HEADS = 4
IDX_DIM = 64
TOPK_MAX = 256
PAGE_SIZE = 128
REL_BUCKETS = 32
REL_MAX_EXACT = 16
REL_MAX_DIST = 128
POOL_GROUPS = 4
POOL_GC = 128
POOL_W = POOL_GROUPS * POOL_GC
POOL_WINDOWS = (2, 4, 8, 16)
POOL_CTX = 15
IN_WIDTHS = (ATTN_W, ATTN_W, ATTN_W, IDX_HEADS * IDX_DIM, IDX_DIM, IDX_HEADS, POOL_W, D_MODEL, D_MODEL)
N_GROUPS = 4
EXPERTS_PER_GROUP = 8
N_EXPERTS = N_GROUPS * EXPERTS_PER_GROUP
TOP_K_FINE = 2
D_EXPERT = 512
MOE_BLOCK = 128
RMS_EPS = 1e-6

LANES = 128
HEAD_PAIR = 2 * HEAD_DIM
INT_MIN = -(2 ** 31)
NEG_BIG = -1e30
VMEM_LIMIT = 56 * 1024 * 1024


def _order_key(score):
    bits = pltpu.bitcast(score, jnp.int32)
    key = bits ^ ((bits >> 31) & 0x7FFFFFFF)
    return jnp.where(key == -1, 0, key)


def _bisect_threshold(count_ge, shape, topk):
    def body(i, lo):
        inc = jnp.left_shift(jnp.int32(1), 31 - i)
        cand = lo + inc
        return jnp.where(count_ge(cand) >= topk, cand, lo)

    return lax.fori_loop(0, 32, body, jnp.full(shape, INT_MIN, jnp.int32))


def _prompt_attn_kernel(qT_ref, qiT_ref, wiT_ref, k_ref, vT_ref, ki_ref, bias_ref, ltri_ref,
                        out_ref, keys_scr, qm_scr, m_scr, acc_scr, *, tile, topk):
    T = tile
    qb = pl.program_id(1)
    row = lax.broadcasted_iota(jnp.int32, (T, T), 0)
    col = lax.broadcasted_iota(jnp.int32, (T, T), 1)
    causal = row <= col

    qiT = qiT_ref[0]
    wT = wiT_ref[0]

    def score_keys(kb):
        kib = ki_ref[0, pl.ds(pl.multiple_of(kb * T, T), T), :]
        sc = None
        for h in range(IDX_HEADS):
            d = jnp.dot(kib, qiT[h * IDX_DIM:(h + 1) * IDX_DIM, :],
                        preferred_element_type=jnp.float32)
            term = wT[h:h + 1, :] * jnp.maximum(d, 0.0)
            sc = term if sc is None else sc + term
        return _order_key(sc)

    def score_body(kb, carry):
        keys_scr[kb] = score_keys(kb)
        return carry

    lax.fori_loop(0, qb, score_body, 0)
    keys_scr[qb] = jnp.where(causal, score_keys(qb), INT_MIN)

    def count_ge(cand):
        def body(kb, acc):
            hit = jnp.where(keys_scr[kb] >= cand, 1.0, 0.0)
            return acc + jnp.sum(hit.reshape(T // 8, 8, T), axis=0)
        acc = lax.fori_loop(0, qb + 1, body, jnp.zeros((8, T), jnp.float32))
        return jnp.sum(acc, axis=0, keepdims=True)

    thr = _bisect_threshold(count_ge, (1, T), float(topk))
    n_ge = count_ge(thr)
    has_ties = jnp.max(jnp.where(n_ge != float(topk), 1.0, 0.0)) > 0.0

    def count_gt_fn():
        def body(kb, acc):
            hit = jnp.where(keys_scr[kb] > thr, 1.0, 0.0)
            return acc + jnp.sum(hit.reshape(T // 8, 8, T), axis=0)
        acc = lax.fori_loop(0, qb + 1, body, jnp.zeros((8, T), jnp.float32))
        return jnp.sum(acc, axis=0, keepdims=True)

    n_tie_keep = float(topk) - lax.cond(has_ties, count_gt_fn,
                                        lambda: jnp.zeros((1, T), jnp.float32))

    zeros_half = jnp.zeros((HEAD_DIM, T), jnp.bfloat16)
    ones_half = jnp.ones((HEAD_DIM, T), jnp.bfloat16)
    for h in range(N_HEADS):
        pair, odd = divmod(h, 2)
        qh = qT_ref[0, h * HEAD_DIM:(h + 1) * HEAD_DIM, :]
        qm_scr[h] = jnp.concatenate([zeros_half, qh] if odd else [qh, zeros_half], axis=0)
    m_scr[...] = jnp.full(m_scr.shape, NEG_BIG, jnp.float32)
    acc_scr[...] = jnp.zeros(acc_scr.shape, jnp.float32)

    def attend_block(kb, tie_seen, near):
        keyb = keys_scr[kb]
        sel_all = keyb >= thr

        def tie_sel():
            eq = keyb == thr
            rank = tie_seen + jnp.dot(ltri_ref[...], jnp.where(eq, 1.0, 0.0).astype(jnp.bfloat16),
                                      preferred_element_type=jnp.float32)
            keep = (keyb > thr) | (eq & (rank < n_tie_keep))
            return jnp.where(keep, 1.0, 0.0), tie_seen + jnp.sum(jnp.where(eq, 1.0, 0.0),
                                                                axis=0, keepdims=True)

        sel_f, tie_seen = lax.cond(has_ties, tie_sel,
                                   lambda: (jnp.where(sel_all, 1.0, 0.0), tie_seen))
        sel = sel_f > 0.5
        if near == 0:
            sel = sel & causal
        row0 = pl.multiple_of(kb * T, T)
        for pair in range(N_HEADS // 2):
            kp = k_ref[0, pl.ds(row0, T), pair * HEAD_PAIR:(pair + 1) * HEAD_PAIR]
            vp = vT_ref[0, kb, pair * HEAD_PAIR:(pair + 1) * HEAD_PAIR, :]
            for odd in range(2):
                h = 2 * pair + odd
                s = jnp.dot(kp, qm_scr[h], preferred_element_type=jnp.float32)
                if near is not None:
                    s = s + bias_ref[near, h]
                s = jnp.where(sel, s, NEG_BIG)
                m_old = m_scr[h]
                m_new = jnp.maximum(m_old, jnp.max(s, axis=0, keepdims=True))
                p = jnp.exp(s - m_new).astype(jnp.bfloat16)
                alpha = jnp.exp(m_old - m_new)
                va = (jnp.concatenate([ones_half, vp[HEAD_DIM:]], axis=0) if odd
                      else jnp.concatenate([vp[:HEAD_DIM], ones_half], axis=0))
                acc_scr[h] = alpha * acc_scr[h] + jnp.dot(va, p, preferred_element_type=jnp.float32)
                m_scr[h] = m_new
        return tie_seen

    tie_seen = lax.fori_loop(0, jnp.maximum(qb - 1, 0),
                             lambda kb, c: attend_block(kb, c, None),
                             jnp.zeros((1, T), jnp.float32))
    tie_seen = lax.cond(qb >= 1, lambda c: attend_block(qb - 1, c, 1), lambda c: c, tie_seen)
    attend_block(qb, tie_seen, 0)

    for h in range(N_HEADS):
        a = acc_scr[h]
        if h % 2:
            res = a[HEAD_DIM:] / a[0:1]
        else:
            res = a[:HEAD_DIM] / a[HEAD_DIM:HEAD_DIM + 1]
        out_ref[0, h * HEAD_DIM:(h + 1) * HEAD_DIM, :] = res


def _rel_bias_by_distance(rel_bias, n):
    dist = jnp.arange(n, dtype=jnp.int32)
    nf = jnp.maximum(dist, 1).astype(jnp.float32)
    large = REL_MAX_EXACT + (jnp.log(nf / REL_MAX_EXACT) / math.log(REL_MAX_DIST / REL_MAX_EXACT)
                             * (REL_BUCKETS - REL_MAX_EXACT)).astype(jnp.int32)
    large = jnp.minimum(large, REL_BUCKETS - 1)
    bucket = jnp.where(dist < REL_MAX_EXACT, dist, large)
    return rel_bias[bucket].astype(jnp.float32)


def _prompt_attention(q, k, v, qi, ki, wi, rel_bias, tile=128):
    b, s = q.shape[:2]
    T = tile
    nkb = s // T
    topk = min(TOPK_MAX, s // 4)
    bf = jnp.bfloat16
    qT = jnp.swapaxes((q * HEAD_DIM ** -0.5).astype(bf), 1, 2)
    qiT = jnp.swapaxes((qi * IDX_DIM ** -0.5).astype(bf), 1, 2)
    wiT = jnp.swapaxes(wi.astype(jnp.float32), 1, 2)
    kb16 = k.astype(bf)
    vT = jnp.swapaxes(v.astype(bf).reshape(b, nkb, T, ATTN_W), 2, 3)
    ki16 = ki.astype(bf)
    bd = _rel_bias_by_distance(rel_bias, 2 * T)
    bd = bd - rel_bias[REL_BUCKETS - 1].astype(jnp.float32)[None, :]
    key_off = jnp.arange(T)[:, None]
    qry_off = jnp.arange(T)[None, :]
    tiles = []
    for near in range(2):
        d = jnp.clip(near * T + qry_off - key_off, 0, 2 * T - 1)
        tiles.append(jnp.transpose(bd[d], (2, 0, 1)))
    bias_tiles = jnp.stack(tiles)
    ltri = (key_off > qry_off).astype(bf)

    kern = functools.partial(_prompt_attn_kernel, tile=T, topk=topk)
    resident = dict(pipeline_mode=pl.Buffered(1))
    outT = pl.pallas_call(
        kern,
        grid=(b, nkb),
        in_specs=[
            pl.BlockSpec((1, ATTN_W, T), lambda i, j: (i, 0, j)),
            pl.BlockSpec((1, IDX_HEADS * IDX_DIM, T), lambda i, j: (i, 0, j)),
            pl.BlockSpec((1, IDX_HEADS, T), lambda i, j: (i, 0, j)),
            pl.BlockSpec((1, s, ATTN_W), lambda i, j: (i, 0, 0), **resident),
            pl.BlockSpec((1, nkb, ATTN_W, T), lambda i, j: (i, 0, 0, 0), **resident),
            pl.BlockSpec((1, s, IDX_DIM), lambda i, j: (i, 0, 0), **resident),
            pl.BlockSpec((2, N_HEADS, T, T), lambda i, j: (0, 0, 0, 0), **resident),
            pl.BlockSpec((T, T), lambda i, j: (0, 0), **resident),
        ],
        out_specs=pl.BlockSpec((1, ATTN_W, T), lambda i, j: (i, 0, j)),
        out_shape=jax.ShapeDtypeStruct((b, ATTN_W, s), jnp.float32),
        scratch_shapes=[
            pltpu.VMEM((nkb, T, T), jnp.int32),
            pltpu.VMEM((N_HEADS, HEAD_PAIR, T), bf),
            pltpu.VMEM((N_HEADS, 1, T), jnp.float32),
            pltpu.VMEM((N_HEADS, HEAD_PAIR, T), jnp.float32),
        ],
        compiler_params=pltpu.CompilerParams(
            dimension_semantics=("arbitrary", "arbitrary"), vmem_limit_bytes=VMEM_LIMIT),
        name="prompt_sparse_attention",
    )(qT, qiT, wiT, kb16, vT, ki16, bias_tiles, ltri)
    return jnp.swapaxes(outT, 1, 2)


def _sample_attn_kernel(pt_ref, qis_ref, wis_ref, kinew_ref, qbd_ref, knew_ref, vnew_ref,
                        ckidx_ref, ck_ref, cv_ref, biasl_ref, biasn_ref, utri_ref,
                        out_ref, keys_scr, thr_scr, keep_scr, tie_scr, flag_scr,
                        m_scr, l_scr, acc_scr, *, n_pages, t_new, topk):
    del pt_ref
    NP = n_pages
    phase = pl.program_id(1)
    p = pl.program_id(2)
    bf = jnp.bfloat16
    nt = (((1,), (1,)), ((), ()))

    def score_keys(kib):
        d = lax.dot_general(qis_ref[0], kib, nt, preferred_element_type=jnp.float32)
        w = wis_ref[0]
        sc = None
        for h in range(IDX_HEADS):
            term = w[:, h:h + 1] * jnp.maximum(d[h * t_new:(h + 1) * t_new], 0.0)
            sc = term if sc is None else sc + term
        return _order_key(sc)

    @pl.when(phase == 0)
    def _():
        keys_scr[p] = score_keys(ckidx_ref[0].astype(bf))

    @pl.when((phase == 0) & (p == NP - 1))
    def _():
        lane = lax.broadcasted_iota(jnp.int32, (t_new, PAGE_SIZE), 1)
        qrow = lax.broadcasted_iota(jnp.int32, (t_new, PAGE_SIZE), 0)
        keys_scr[NP] = jnp.where(lane <= qrow, score_keys(kinew_ref[0]), INT_MIN)

        def count(pred):
            def body(j, acc):
                return acc + jnp.where(pred(keys_scr[j]), 1.0, 0.0)
            acc = lax.fori_loop(0, NP + 1, body, jnp.zeros((t_new, PAGE_SIZE), jnp.float32))
            return jnp.sum(acc, axis=1, keepdims=True)

        thr = _bisect_threshold(lambda c: count(lambda kk: kk >= c), (t_new, 1), float(topk))
        n_ge = count(lambda kk: kk >= thr)
        n_gt = count(lambda kk: kk > thr)
        thr_scr[...] = thr
        keep_scr[...] = float(topk) - n_gt
        flag_scr[0] = (jnp.max(jnp.where(n_ge != float(topk), 1.0, 0.0)) > 0.0).astype(jnp.int32)
        tie_scr[...] = jnp.zeros(tie_scr.shape, jnp.float32)
        m_scr[...] = jnp.full(m_scr.shape, NEG_BIG, jnp.float32)
        l_scr[...] = jnp.zeros(l_scr.shape, jnp.float32)
        acc_scr[...] = jnp.zeros(acc_scr.shape, jnp.float32)

    def attend(keyb, kmat, vmat, bias):
        thr = thr_scr[...]

        def tie_sel():
            eq = keyb == thr
            eqf = jnp.where(eq, 1.0, 0.0)
            rank = tie_scr[...] + jnp.dot(eqf.astype(bf), utri_ref[...],
                                          preferred_element_type=jnp.float32)
            keep = (keyb > thr) | (eq & (rank < keep_scr[...]))
            tie_scr[...] = tie_scr[...] + jnp.sum(eqf, axis=1, keepdims=True)
            return jnp.where(keep, 1.0, 0.0)

        sel8 = lax.cond(flag_scr[0] != 0, tie_sel, lambda: jnp.where(keyb >= thr, 1.0, 0.0))
        sel = jnp.concatenate([sel8] * N_HEADS, axis=0) > 0.5
        s = lax.dot_general(qbd_ref[0], kmat, nt, preferred_element_type=jnp.float32)
        if bias is not None:
            s = s + bias
        s = jnp.where(sel, s, NEG_BIG)
        m_old = m_scr[...]
        m_new = jnp.maximum(m_old, jnp.max(s, axis=1, keepdims=True))
        pr = jnp.exp(s - m_new)
        alpha = jnp.exp(m_old - m_new)
        l_scr[...] = alpha * l_scr[...] + jnp.sum(pr, axis=1, keepdims=True)
        acc_scr[...] = alpha * acc_scr[...] + jnp.dot(pr.astype(bf), vmat,
                                                      preferred_element_type=jnp.float32)
        m_scr[...] = m_new

    @pl.when(phase == 1)
    def _():
        last = (p == NP - 1).astype(jnp.float32)
        attend(keys_scr[p], ck_ref[0].astype(bf), cv_ref[0].astype(bf), last * biasl_ref[...])

    @pl.when((phase == 1) & (p == NP - 1))
    def _():
        attend(keys_scr[NP], knew_ref[0], vnew_ref[0], biasn_ref[...])
        o = acc_scr[...] / l_scr[...]
        head_of_lane = lax.broadcasted_iota(jnp.int32, (t_new, ATTN_W), 1) // HEAD_DIM
        res = jnp.zeros((t_new, ATTN_W), jnp.float32)
        for h in range(N_HEADS):
            res = res + jnp.where(head_of_lane == h, o[h * t_new:(h + 1) * t_new], 0.0)
        out_ref[0] = res


def _sample_attention(q, k_new, v_new, qi, ki_new, wi, cache_k, cache_v, cache_kidx, page_table,
                      rel_bias):
    nb, t_new = q.shape[:2]
    n_pages = page_table.shape[1]
    n_phys = cache_k.shape[0]
    past = n_pages * PAGE_SIZE
    topk = min(TOPK_MAX, (past + t_new) // 4)
    bf = jnp.bfloat16
    rows = N_HEADS * t_new
    qis = (qi * IDX_DIM ** -0.5).astype(bf).reshape(nb, t_new, IDX_HEADS, IDX_DIM)
    qis = jnp.swapaxes(qis, 1, 2).reshape(nb, IDX_HEADS * t_new, IDX_DIM)
    q4 = jnp.swapaxes((q * HEAD_DIM ** -0.5).astype(bf).reshape(nb, t_new, N_HEADS, HEAD_DIM), 1, 2)
    eye = jnp.eye(N_HEADS, dtype=bf)
    qbd = (q4[:, :, :, None, :] * eye[None, :, None, :, None]).reshape(nb, rows, ATTN_W)

    def pad_rows(a):
        return jnp.pad(a.astype(bf), ((0, 0), (0, PAGE_SIZE - t_new), (0, 0)))

    bd = _rel_bias_by_distance(rel_bias, 2 * PAGE_SIZE)
    bd = bd - rel_bias[REL_BUCKETS - 1].astype(jnp.float32)[None, :]
    qo = jnp.arange(t_new)[:, None]
    co = jnp.arange(PAGE_SIZE)[None, :]
    d_last = jnp.clip(PAGE_SIZE + qo - co, 0, 2 * PAGE_SIZE - 1)
    d_new = jnp.clip(qo - co, 0, 2 * PAGE_SIZE - 1)
    bias_last = jnp.transpose(bd[d_last], (2, 0, 1)).reshape(rows, PAGE_SIZE)
    bias_new = jnp.transpose(bd[d_new], (2, 0, 1)).reshape(rows, PAGE_SIZE)
    utri = (jnp.arange(PAGE_SIZE)[:, None] < jnp.arange(PAGE_SIZE)[None, :]).astype(bf)

    kern = functools.partial(_sample_attn_kernel, n_pages=n_pages, t_new=t_new, topk=topk)
    seq = lambda i, ph, p, pt: (i, 0, 0)
    const2 = lambda i, ph, p, pt: (0, 0)

    def kidx_page(i, ph, p, pt):
        return (jnp.where(ph == 0, pt[i * n_pages + p], pt[i * n_pages + n_pages - 1]), 0, 0)

    def kv_page(i, ph, p, pt):
        return (jnp.where(ph == 1, pt[i * n_pages + p], pt[i * n_pages]), 0, 0)

    grid_spec = pltpu.PrefetchScalarGridSpec(
        num_scalar_prefetch=1,
        grid=(nb, 2, n_pages),
        in_specs=[
            pl.BlockSpec((1, IDX_HEADS * t_new, IDX_DIM), seq),
            pl.BlockSpec((1, t_new, IDX_HEADS), seq),
            pl.BlockSpec((1, PAGE_SIZE, IDX_DIM), seq),
            pl.BlockSpec((1, rows, ATTN_W), seq),
            pl.BlockSpec((1, PAGE_SIZE, ATTN_W), seq),
            pl.BlockSpec((1, PAGE_SIZE, ATTN_W), seq),
            pl.BlockSpec((1, PAGE_SIZE, IDX_DIM), kidx_page),
            pl.BlockSpec((1, PAGE_SIZE, ATTN_W), kv_page),
            pl.BlockSpec((1, PAGE_SIZE, ATTN_W), kv_page),
            pl.BlockSpec((rows, PAGE_SIZE), const2),
            pl.BlockSpec((rows, PAGE_SIZE), const2),
            pl.BlockSpec((PAGE_SIZE, PAGE_SIZE), const2),
        ],
        out_specs=pl.BlockSpec((1, t_new, ATTN_W), seq),
        scratch_shapes=[
            pltpu.VMEM((n_pages + 1, t_new, PAGE_SIZE), jnp.int32),
            pltpu.VMEM((t_new, 1), jnp.int32),
            pltpu.VMEM((t_new, 1), jnp.float32),
            pltpu.VMEM((t_new, 1), jnp.float32),
            pltpu.SMEM((1,), jnp.int32),
            pltpu.VMEM((rows, 1), jnp.float32),
            pltpu.VMEM((rows, 1), jnp.float32),
            pltpu.VMEM((rows, ATTN_W), jnp.float32),
        ],
    )
    return pl.pallas_call(
        kern,
        grid_spec=grid_spec,
        out_shape=jax.ShapeDtypeStruct((nb, t_new, ATTN_W), jnp.float32),
        compiler_params=pltpu.CompilerParams(
            dimension_semantics=("arbitrary", "arbitrary", "arbitrary")),
        name="sample_sparse_attention",
    )(page_table.reshape(-1).astype(jnp.int32), qis, wi.astype(jnp.float32), pad_rows(ki_new), qbd,
      pad_rows(k_new), pad_rows(v_new),
      cache_kidx.reshape(n_phys, PAGE_SIZE, IDX_DIM), cache_k.reshape(n_phys, PAGE_SIZE, ATTN_W),
      cache_v.reshape(n_phys, PAGE_SIZE, ATTN_W), bias_last, bias_new, utri)


def _rmsnorm(x, g):
    xf = x.astype(jnp.float32)
    xf = xf * lax.rsqrt(jnp.mean(xf * xf, axis=-1, keepdims=True) + RMS_EPS)
    return (xf * g.astype(jnp.float32)).astype(x.dtype)


def _input_projection(x, ln_g, w_in):
    h = _rmsnorm(x, ln_g)
    p = h @ w_in
    points = [int(c) for c in np.cumsum(IN_WIDTHS)[:-1]]
    return jnp.split(p, points, axis=-1)


def _multiscale_pool(u, n_ctx, w_pool_map, pool_scale):
    n, t_all = u.shape[:2]
    uf = u.astype(jnp.float32)
    c = jnp.concatenate([jnp.zeros((n, 1, POOL_W), jnp.float32), jnp.cumsum(uf, axis=1)], axis=1)
    j = jnp.arange(n_ctx, t_all)
    outs = []
    for g, w in enumerate(POOL_WINDOWS):
        sl = slice(g * POOL_GC, (g + 1) * POOL_GC)
        lo = jnp.maximum(j + 1 - w, 0)
        cnt = (j + 1 - lo).astype(jnp.float32)[:, None]
        mean = (c[:, j + 1, sl] - c[:, lo, sl]) / cnt
        outs.append(mean - uf[:, n_ctx:, sl])
    d = jnp.concatenate(outs, axis=-1).reshape(n, t_all - n_ctx, POOL_GROUPS, POOL_GC)
    y = jnp.einsum('ntgc,gce->ntge', d, w_pool_map.astype(jnp.float32)).reshape(n, t_all - n_ctx, POOL_W)
    return (y * pool_scale.astype(jnp.float32)).astype(u.dtype)


def _merge_branches(x, attn, pool, ga, gp, w_br_attn, w_br_pool, w_out):
    m = jax.nn.sigmoid(ga) * (attn @ w_br_attn) + jax.nn.sigmoid(gp) * (pool @ w_br_pool)
    return x + m @ w_out


def _grouped_swiglu(x, experts, gates, w_gate, w_up, w_down):
    n = x.shape[0]
    a = n * TOP_K_FINE
    e = experts.reshape(-1)
    tok = jnp.arange(a, dtype=jnp.int32) // TOP_K_FINE
    order = jnp.argsort(e)
    e_sorted = e[order]
    counts = jnp.bincount(e, length=N_EXPERTS)
    starts = jnp.cumsum(counts) - counts
    padded = ((counts + MOE_BLOCK - 1) // MOE_BLOCK) * MOE_BLOCK
    pad_ends = jnp.cumsum(padded)
    pad_starts = pad_ends - padded
    dest_sorted = pad_starts[e_sorted] + (jnp.arange(a) - starts[e_sorted])
    nblk = (a + MOE_BLOCK - 1) // MOE_BLOCK + N_EXPERTS
    slot_tok = jnp.full((nblk * MOE_BLOCK,), n, jnp.int32).at[dest_sorted].set(tok[order])
    block_expert = jnp.minimum(jnp.searchsorted(pad_ends, jnp.arange(nblk) * MOE_BLOCK, side='right'),
                               N_EXPERTS - 1)
    x_pad = jnp.concatenate([x, jnp.zeros((1, x.shape[1]), x.dtype)], axis=0)[slot_tok]
    x_pad = x_pad.reshape(nblk, MOE_BLOCK, x.shape[1])

    def run(args):
        xb, eb = args
        hdn = jax.nn.silu(xb @ w_gate[eb]) * (xb @ w_up[eb])
        return hdn @ w_down[eb]

    yb = lax.map(run, (x_pad, block_expert)).reshape(nblk * MOE_BLOCK, x.shape[1])
    dest = jnp.zeros((a,), jnp.int32).at[order].set(dest_sorted.astype(jnp.int32))
    y = yb[dest].reshape(n, TOP_K_FINE, x.shape[1])
    return jnp.sum(y * gates[..., None].astype(y.dtype), axis=1)


def _hier_moe(h, w_coarse, b_coarse, w_fine, b_fine, w_gate, w_up, w_down):
    shp = h.shape
    x = h.reshape(-1, shp[-1])
    lc = (x @ w_coarse + b_coarse).astype(jnp.float32)
    pc = jax.nn.softmax(lc, axis=-1)
    grp = jnp.argmax(lc, axis=-1)
    p_grp = jnp.take_along_axis(pc, grp[:, None], axis=1)[:, 0]
    lf = jnp.einsum('nd,gde->nge', x, w_fine) + b_fine
    lf_sel = jnp.take_along_axis(lf, grp[:, None, None], axis=1)[:, 0].astype(jnp.float32)
    pf = jax.nn.softmax(lf_sel, axis=-1)
    top_p, top_i = lax.top_k(pf, TOP_K_FINE)
    gates = p_grp[:, None] * top_p / jnp.sum(top_p, axis=-1, keepdims=True)
    experts = grp[:, None].astype(jnp.int32) * EXPERTS_PER_GROUP + top_i
    y = _grouped_swiglu(x, experts, gates, w_gate, w_up, w_down)
    return y.reshape(shp).astype(h.dtype)


def kernel(x_prompt, x_sample, cache_k, cache_v, cache_kidx, state_pool, page_table, rel_bias,
           ln1_g, w_in, w_pool_map, pool_scale, w_br_attn, w_br_pool, w_out, ln2_g,
           w_coarse, b_coarse, w_fine, b_fine, w_gate, w_up, w_down, lnf_g):
    d = 0
    xp, xs = x_prompt, x_sample
    nb, s = xp.shape[:2]
    db, tn = xs.shape[:2]
    q, k, v, qi, ki, wi, u, ga, gp = _input_projection(xp, ln1_g[d], w_in[d])
    attn = _prompt_attention(q, k, v, qi, ki, wi, rel_bias)
    pool = _multiscale_pool(u, 0, w_pool_map[d], pool_scale[d])
    xp = _merge_branches(xp, attn, pool, ga, gp, w_br_attn[d], w_br_pool[d], w_out[d])
    xp = xp + _hier_moe(_rmsnorm(xp, ln2_g[d]), w_coarse[d], b_coarse[d], w_fine[d], b_fine[d],
                        w_gate[d], w_up[d], w_down[d])
    k_prompt = k.reshape(1, nb, s, N_HEADS, HEAD_DIM)
    v_prompt = v.reshape(1, nb, s, N_HEADS, HEAD_DIM)
    kidx_prompt = ki[None]
    pool_prompt = u[None, :, -POOL_CTX:]
    q, k, v, qi, ki, wi, u, ga, gp = _input_projection(xs, ln1_g[d], w_in[d])
    attn = _sample_attention(q, k, v, qi, ki, wi, cache_k[d], cache_v[d], cache_kidx[d],
                             page_table, rel_bias)
    buf = jnp.concatenate([state_pool[d].astype(u.dtype), u], axis=1)
    pool = _multiscale_pool(buf, POOL_CTX, w_pool_map[d], pool_scale[d])
    xs = _merge_branches(xs, attn, pool, ga, gp, w_br_attn[d], w_br_pool[d], w_out[d])
    xs = xs + _hier_moe(_rmsnorm(xs, ln2_g[d]), w_coarse[d], b_coarse[d], w_fine[d], b_fine[d],
                        w_gate[d], w_up[d], w_down[d])
    y_prompt = _rmsnorm(xp, lnf_g)
    y_sample = _rmsnorm(xs, lnf_g)
    return (y_prompt, y_sample, k_prompt, v_prompt, kidx_prompt, pool_prompt,
            k.reshape(1, db, tn, N_HEADS, HEAD_DIM), v.reshape(1, db, tn, N_HEADS, HEAD_DIM),
            ki[None], buf[None, :, -POOL_CTX:])
```

```python
import functools
import math

import jax
import jax.numpy as jnp
from jax import lax
from jax.experimental import pallas as pl
from jax.experimental.pallas import tpu as pltpu

D_MODEL = 1024
N_HEADS = 8
HEAD_DIM = 64
ATTN_W = N_HEADS * HEAD_DIM
IDX_HEADS = 4
IDX_DIM = 64
IDX_W = IDX_HEADS * IDX_DIM
TOPK_MAX = 256
PAGE_SIZE = 128
REL_BUCKETS = 32
REL_MAX_EXACT = 16
REL_MAX_DIST = 128
POOL_GROUPS = 4
POOL_GC = 128
POOL_W = POOL_GROUPS * POOL_GC
POOL_WINDOWS = (2, 4, 8, 16)
POOL_CTX = 15
N_GROUPS = 4
EXPERTS_PER_GROUP = 8
N_EXPERTS = N_GROUPS * EXPERTS_PER_GROUP
TOP_K_FINE = 2
D_EXPERT = 512
RMS_EPS = 1e-6

LANES = 128
SUBLANES = 8
HEAD_PAIR = 2 * HEAD_DIM
INT_MIN = -(2 ** 31)
NEG_BIG = -1e30
VMEM_LIMIT = 56 * 1024 * 1024
ROW_TILE = 512
Q_TILE = 128
K_TILE = 512
MOE_ROWS = 256
POOL_HALO = 16
FINE_LANE0 = 8

_NT = (((1,), (1,)), ((), ()))
_BF = jnp.bfloat16
_F32 = jnp.float32


def _resident(shape, index_map):
    return pl.BlockSpec(shape, index_map, pipeline_mode=pl.Buffered(1))


def _order_key(score):
    bits = pltpu.bitcast(score, jnp.int32)
    key = bits ^ ((bits >> 31) & 0x7FFFFFFF)
    return jnp.where(key == -1, 0, key)


def _bisect_threshold(count_ge, shape, topk):
    def body(i, lo):
        inc = jnp.left_shift(jnp.int32(1), 31 - i)
        cand = lo + inc
        return jnp.where(count_ge(cand) >= topk, cand, lo)

    return lax.fori_loop(0, 32, body, jnp.full(shape, INT_MIN, jnp.int32))


def _rms_scale(x, g):
    ms = jnp.mean(x * x, axis=-1, keepdims=True)
    return x * lax.rsqrt(ms + RMS_EPS) * g


_ROW_SECTIONS = (("k", ATTN_W), ("v", ATTN_W), ("u", POOL_W), ("ga", D_MODEL), ("gp", D_MODEL),
                 ("ki", LANES))
_COL_SECTIONS = (("q", ATTN_W), ("qi", IDX_W), ("v", ATTN_W), ("wi", SUBLANES))


def _proj_kernel(x_ref, g_ref, wa_ref, wbt_ref,
                 k_ref, v_ref, u_ref, ga_ref, gp_ref, ki_ref, k16_ref, ki16_ref,
                 qt_ref, qit_ref, vt_ref, wit_ref):
    h = _rms_scale(x_ref[...], g_ref[...]).astype(_BF)
    outs = {}
    lo = 0
    for name, width in _ROW_SECTIONS:
        outs[name] = jnp.dot(h, wa_ref[:, lo:lo + width], preferred_element_type=_F32)
        lo += width
    k_ref[...] = outs["k"]
    v_ref[...] = outs["v"]
    u_ref[...] = outs["u"]
    ga_ref[...] = outs["ga"]
    gp_ref[...] = outs["gp"]
    ki_ref[...] = outs["ki"][:, :IDX_DIM]
    k16_ref[...] = outs["k"].astype(_BF)
    ki16_ref[...] = outs["ki"][:, :IDX_DIM].astype(_BF)
    lo = 0
    for name, width in _COL_SECTIONS:
        t = lax.dot_general(wbt_ref[lo:lo + width, :], h, _NT, preferred_element_type=_F32)
        lo += width
        if name == "q":
            qt_ref[...] = t.astype(_BF)
        elif name == "qi":
            qit_ref[...] = t.astype(_BF)
        elif name == "v":
            vt_ref[0] = t.astype(_BF)
        else:
            wit_ref[...] = t


def _project(x, ln_g, w_in):
    n = x.shape[0]
    tm = ROW_TILE
    assert n % tm == 0 and tm == K_TILE
    widths = (ATTN_W, ATTN_W, ATTN_W, IDX_W, IDX_DIM, IDX_HEADS, POOL_W, D_MODEL, D_MODEL)
    names = ("q", "k", "v", "qi", "ki", "wi", "u", "ga", "gp")
    cols, lo = {}, 0
    for name, width in zip(names, widths):
        cols[name] = w_in[:, lo:lo + width]
        lo += width
    ki_pad = jnp.pad(cols["ki"], ((0, 0), (0, LANES - IDX_DIM)))
    wa = jnp.concatenate([cols["k"], cols["v"], cols["u"], cols["ga"], cols["gp"], ki_pad],
                         axis=1).astype(_BF)
    wi_pad = jnp.pad(cols["wi"], ((0, 0), (0, SUBLANES - IDX_HEADS)))
    wbt = jnp.concatenate([cols["q"] * HEAD_DIM ** -0.5, cols["qi"] * IDX_DIM ** -0.5, cols["v"],
                           wi_pad], axis=1).T.astype(_BF)
    na, nb = wa.shape[1], wbt.shape[0]
    row = lambda w: pl.BlockSpec((tm, w), lambda i: (i, 0))
    colb = lambda h: pl.BlockSpec((h, tm), lambda i: (0, i))
    out_shapes = dict(
        k=((n, ATTN_W), _F32, row(ATTN_W)), v=((n, ATTN_W), _F32, row(ATTN_W)),
        u=((n, POOL_W), _F32, row(POOL_W)), ga=((n, D_MODEL), _F32, row(D_MODEL)),
        gp=((n, D_MODEL), _F32, row(D_MODEL)), ki=((n, IDX_DIM), _F32, row(IDX_DIM)),
        k16=((n, ATTN_W), _BF, row(ATTN_W)), ki16=((n, IDX_DIM), _BF, row(IDX_DIM)),
        qT=((ATTN_W, n), _BF, colb(ATTN_W)), qiT=((IDX_W, n), _BF, colb(IDX_W)),
        vT=((n // tm, ATTN_W, tm), _BF, pl.BlockSpec((1, ATTN_W, tm), lambda i: (i, 0, 0))),
        wiT=((SUBLANES, n), _F32, colb(SUBLANES)),
    )
    keys = list(out_shapes)
    res = pl.pallas_call(
        _proj_kernel,
        grid=(n // tm,),
        in_specs=[row(D_MODEL), _resident((1, D_MODEL), lambda i: (0, 0)),
                  _resident((D_MODEL, na), lambda i: (0, 0)),
                  _resident((nb, D_MODEL), lambda i: (0, 0))],
        out_specs=[out_shapes[k][2] for k in keys],
        out_shape=[jax.ShapeDtypeStruct(out_shapes[k][0], out_shapes[k][1]) for k in keys],
        compiler_params=pltpu.CompilerParams(dimension_semantics=("arbitrary",),
                                             vmem_limit_bytes=VMEM_LIMIT),
        name="input_projection",
    )(x, ln_g.reshape(1, D_MODEL).astype(_F32), wa, wbt)
    return dict(zip(keys, res))


def _prompt_attn_kernel(qT_ref, qiT_ref, wiT_ref, k_ref, vT_ref, ki_ref, bias_ref, ltri_ref,
                        out_ref, keys_scr, mask_scr, qm_scr, m_scr, acc_scr, *, topk):
    TQ, TK = Q_TILE, K_TILE
    sub = TK // TQ
    qb = pl.program_id(1)
    n_sb = qb // sub + 1
    n_far = jnp.maximum(qb - 1, 0) // sub
    key_off = lax.broadcasted_iota(jnp.int32, (TK, TQ), 0)
    qry_off = lax.broadcasted_iota(jnp.int32, (TK, TQ), 1)

    def causal_at(sb):
        return (sb * TK + key_off) <= (qb * TQ + qry_off)

    qiT = qiT_ref[...]
    wT = wiT_ref[...]

    def score_keys(sb):
        kib = ki_ref[pl.ds(pl.multiple_of(sb * TK, TK), TK), :]
        sc = None
        for h in range(IDX_HEADS):
            d = jnp.dot(kib, qiT[h * IDX_DIM:(h + 1) * IDX_DIM, :], preferred_element_type=_F32)
            term = wT[h:h + 1, :] * jnp.maximum(d, 0.0)
            sc = term if sc is None else sc + term
        return _order_key(sc)

    def score_body(sb, carry):
        keys_scr[sb] = score_keys(sb)
        return carry

    lax.fori_loop(0, n_sb - 1, score_body, 0)
    keys_scr[n_sb - 1] = jnp.where(causal_at(n_sb - 1), score_keys(n_sb - 1), INT_MIN)

    def count(pred):
        def body(sb, acc):
            hit = jnp.where(pred(keys_scr[sb]), 1.0, 0.0)
            return acc + jnp.sum(hit.reshape(TK // SUBLANES, SUBLANES, TQ), axis=0)
        acc = lax.fori_loop(0, n_sb, body, jnp.zeros((SUBLANES, TQ), _F32))
        return jnp.sum(acc, axis=0, keepdims=True)

    kf = float(topk)
    thr = _bisect_threshold(lambda c: count(lambda kk: kk >= c), (1, TQ), kf)
    n_ge = count(lambda kk: kk >= thr)
    has_ties = jnp.max(jnp.where(n_ge != kf, 1.0, 0.0)) > 0.0
    n_tie_keep = kf - lax.cond(has_ties, lambda: count(lambda kk: kk > thr),
                               lambda: jnp.zeros((1, TQ), _F32))

    zeros_half = jnp.zeros((HEAD_DIM, TQ), _BF)
    for h in range(N_HEADS):
        qh = qT_ref[h * HEAD_DIM:(h + 1) * HEAD_DIM, :]
        qm_scr[h] = jnp.concatenate([zeros_half, qh] if h % 2 else [qh, zeros_half], axis=0)
    m_scr[...] = jnp.full(m_scr.shape, NEG_BIG, _F32)
    acc_scr[...] = jnp.zeros(acc_scr.shape, _F32)
    ones_half = jnp.ones((HEAD_DIM, TK), _BF)

    def attend_block(sb, tie_seen, near):
        keyb = keys_scr[sb]

        def tie_sel():
            eq = keyb == thr
            eqf = jnp.where(eq, 1.0, 0.0)
            rank = tie_seen + jnp.dot(ltri_ref[...], eqf.astype(_BF), preferred_element_type=_F32)
            keep = (keyb > thr) | (eq & (rank < n_tie_keep))
            return jnp.where(keep, 0.0, NEG_BIG), tie_seen + jnp.sum(eqf, axis=0, keepdims=True)

        madd, tie_seen = lax.cond(has_ties, tie_sel,
                                  lambda: (jnp.where(keyb >= thr, 0.0, NEG_BIG), tie_seen))
        if near:
            madd = jnp.where(causal_at(sb), madd, NEG_BIG)
        mask_scr[...] = madd
        row0 = pl.multiple_of(sb * TK, TK)
        for pair in range(N_HEADS // 2):
            kp = k_ref[pl.ds(row0, TK), pair * HEAD_PAIR:(pair + 1) * HEAD_PAIR]
            vp = vT_ref[sb, pair * HEAD_PAIR:(pair + 1) * HEAD_PAIR, :]
            for odd in range(2):
                h = 2 * pair + odd
                s = jnp.dot(kp, qm_scr[h], preferred_element_type=_F32) + mask_scr[...]
                if near:
                    parts = []
                    for j in range(sub):
                        back = qb - (sb * sub + j)
                        w0 = (back == 0).astype(_F32)
                        w1 = (back == 1).astype(_F32)
                        parts.append(s[j * TQ:(j + 1) * TQ] + w0 * bias_ref[0, h] + w1 * bias_ref[1, h])
                    s = jnp.concatenate(parts, axis=0)
                m_old = m_scr[h]
                m_new = jnp.maximum(m_old, jnp.max(s, axis=0, keepdims=True))
                p = jnp.exp(s - m_new).astype(_BF)
                alpha = jnp.exp(m_old - m_new)
                va = (jnp.concatenate([ones_half, vp[HEAD_DIM:]], axis=0) if odd
                      else jnp.concatenate([vp[:HEAD_DIM], ones_half], axis=0))
                acc_scr[h] = alpha * acc_scr[h] + jnp.dot(va, p, preferred_element_type=_F32)
                m_scr[h] = m_new
        return tie_seen

    tie_seen = lax.fori_loop(0, n_far, lambda sb, c: attend_block(sb, c, False),
                             jnp.zeros((1, TQ), _F32))
    lax.fori_loop(n_far, n_sb, lambda sb, c: attend_block(sb, c, True), tie_seen)

    for pair in range(N_HEADS // 2):
        a0 = acc_scr[2 * pair]
        a1 = acc_scr[2 * pair + 1]
        res = jnp.concatenate([a0[:HEAD_DIM] / a0[HEAD_DIM:HEAD_DIM + 1],
                               a1[HEAD_DIM:] / a1[0:1]], axis=0)
        out_ref[:, pair * HEAD_PAIR:(pair + 1) * HEAD_PAIR] = res.T.astype(out_ref.dtype)


def _rel_bias_by_distance(rel_bias, n):
    dist = jnp.arange(n, dtype=jnp.int32)
    nf = jnp.maximum(dist, 1).astype(_F32)
    large = REL_MAX_EXACT + (jnp.log(nf / REL_MAX_EXACT) / math.log(REL_MAX_DIST / REL_MAX_EXACT)
                             * (REL_BUCKETS - REL_MAX_EXACT)).astype(jnp.int32)
    large = jnp.minimum(large, REL_BUCKETS - 1)
    bucket = jnp.where(dist < REL_MAX_EXACT, dist, large)
    return (rel_bias[bucket] - rel_bias[REL_BUCKETS - 1][None, :]).astype(_F32)


def _prompt_attention(proj, rel_bias, nb, s):
    TQ, TK = Q_TILE, K_TILE
    assert s % TK == 0
    nq, nsb = s // TQ, s // TK
    topk = min(TOPK_MAX, s // 4)
    bd = _rel_bias_by_distance(rel_bias, 2 * TQ)
    key_off = jnp.arange(TQ)[:, None]
    qry_off = jnp.arange(TQ)[None, :]
    tiles = [jnp.transpose(bd[jnp.clip(back * TQ + qry_off - key_off, 0, 2 * TQ - 1)], (2, 0, 1))
             for back in range(2)]
    bias_tiles = jnp.stack(tiles)
    ltri = (jnp.arange(TK)[:, None] > jnp.arange(TK)[None, :]).astype(_BF)

    kern = functools.partial(_prompt_attn_kernel, topk=topk)
    return pl.pallas_call(
        kern,
        grid=(nb, nq),
        in_specs=[
            pl.BlockSpec((ATTN_W, TQ), lambda i, j: (0, i * nq + j)),
            pl.BlockSpec((IDX_W, TQ), lambda i, j: (0, i * nq + j)),
            pl.BlockSpec((SUBLANES, TQ), lambda i, j: (0, i * nq + j)),
            _resident((s, ATTN_W), lambda i, j: (i, 0)),
            _resident((nsb, ATTN_W, TK), lambda i, j: (i, 0, 0)),
            _resident((s, IDX_DIM), lambda i, j: (i, 0)),
            _resident((2, N_HEADS, TQ, TQ), lambda i, j: (0, 0, 0, 0)),
            _resident((TK, TK), lambda i, j: (0, 0)),
        ],
        out_specs=pl.BlockSpec((TQ, ATTN_W), lambda i, j: (i * nq + j, 0)),
        out_shape=jax.ShapeDtypeStruct((nb * s, ATTN_W), _BF),
        scratch_shapes=[
            pltpu.VMEM((nsb, TK, TQ), jnp.int32),
            pltpu.VMEM((TK, TQ), _F32),
            pltpu.VMEM((N_HEADS, HEAD_PAIR, TQ), _BF),
            pltpu.VMEM((N_HEADS, 1, TQ), _F32),
            pltpu.VMEM((N_HEADS, HEAD_PAIR, TQ), _F32),
        ],
        compiler_params=pltpu.CompilerParams(
            dimension_semantics=("arbitrary", "arbitrary"), vmem_limit_bytes=VMEM_LIMIT),
        name="prompt_sparse_attention",
    )(proj["qT"], proj["qiT"], proj["wiT"], proj["k16"], proj["vT"], proj["ki16"], bias_tiles, ltri)


def _sample_attn_kernel(pt_ref, qis_ref, wis_ref, kinew_ref, qbd_ref, knew_ref, vnew_ref,
                        ckidx_ref, ck_ref, cv_ref, biasl_ref, biasn_ref, utri_ref,
                        out_ref, keys_scr, thr_scr, keep_scr, tie_scr, flag_scr,
                        m_scr, l_scr, acc_scr, *, n_pages, t_new, topk):
    del pt_ref
    NP = n_pages
    phase = pl.program_id(1)
    p = pl.program_id(2)

    def score_keys(kib):
        d = lax.dot_general(qis_ref[0], kib, _NT, preferred_element_type=_F32)
        w = wis_ref[0]
        sc = None
        for h in range(IDX_HEADS):
            term = w[:, h:h + 1] * jnp.maximum(d[h * t_new:(h + 1) * t_new], 0.0)
            sc = term if sc is None else sc + term
        return _order_key(sc)

    @pl.when(phase == 0)
    def _():
        keys_scr[p] = score_keys(ckidx_ref[0].astype(_BF))

    @pl.when((phase == 0) & (p == NP - 1))
    def _():
        lane = lax.broadcasted_iota(jnp.int32, (t_new, PAGE_SIZE), 1)
        qrow = lax.broadcasted_iota(jnp.int32, (t_new, PAGE_SIZE), 0)
        keys_scr[NP] = jnp.where(lane <= qrow, score_keys(kinew_ref[0]), INT_MIN)

        def count(pred):
            def body(j, acc):
                return acc + jnp.where(pred(keys_scr[j]), 1.0, 0.0)
            acc = lax.fori_loop(0, NP + 1, body, jnp.zeros((t_new, PAGE_SIZE), _F32))
            return jnp.sum(acc, axis=1, keepdims=True)

        kf = float(topk)
        thr = _bisect_threshold(lambda c: count(lambda kk: kk >= c), (t_new, 1), kf)
        n_ge = count(lambda kk: kk >= thr)
        n_gt = count(lambda kk: kk > thr)
        thr_scr[...] = thr
        keep_scr[...] = kf - n_gt
        flag_scr[0] = (jnp.max(jnp.where(n_ge != kf, 1.0, 0.0)) > 0.0).astype(jnp.int32)
        tie_scr[...] = jnp.zeros(tie_scr.shape, _F32)
        m_scr[...] = jnp.full(m_scr.shape, NEG_BIG, _F32)
        l_scr[...] = jnp.zeros(l_scr.shape, _F32)
        acc_scr[...] = jnp.zeros(acc_scr.shape, _F32)

    def attend(keyb, kmat, vmat, bias):
        thr = thr_scr[...]

        def tie_sel():
            eq = keyb == thr
            eqf = jnp.where(eq, 1.0, 0.0)
            rank = tie_scr[...] + jnp.dot(eqf.astype(_BF), utri_ref[...],
                                          preferred_element_type=_F32)
            keep = (keyb > thr) | (eq & (rank < keep_scr[...]))
            tie_scr[...] = tie_scr[...] + jnp.sum(eqf, axis=1, keepdims=True)
            return jnp.where(keep, 1.0, 0.0)

        sel8 = lax.cond(flag_scr[0] != 0, tie_sel, lambda: jnp.where(keyb >= thr, 1.0, 0.0))
        sel = jnp.concatenate([sel8] * N_HEADS, axis=0) > 0.5
        s = lax.dot_general(qbd_ref[0], kmat, _NT, preferred_element_type=_F32)
        s = jnp.where(sel, s + bias, NEG_BIG)
        m_old = m_scr[...]
        m_new = jnp.maximum(m_old, jnp.max(s, axis=1, keepdims=True))
        pr = jnp.exp(s - m_new)
        alpha = jnp.exp(m_old - m_new)
        l_scr[...] = alpha * l_scr[...] + jnp.sum(pr, axis=1, keepdims=True)
        acc_scr[...] = alpha * acc_scr[...] + jnp.dot(pr.astype(_BF), vmat,
                                                      preferred_element_type=_F32)
        m_scr[...] = m_new

    @pl.when(phase == 1)
    def _():
        last = (p == NP - 1).astype(_F32)
        attend(keys_scr[p], ck_ref[0].astype(_BF), cv_ref[0].astype(_BF), last * biasl_ref[...])

    @pl.when((phase == 1) & (p == NP - 1))
    def _():
        attend(keys_scr[NP], knew_ref[0], vnew_ref[0], biasn_ref[...])
        o = acc_scr[...] / l_scr[...]
        head_of_lane = lax.broadcasted_iota(jnp.int32, (t_new, ATTN_W), 1) // HEAD_DIM
        res = jnp.zeros((t_new, ATTN_W), _F32)
        for h in range(N_HEADS):
            res = res + jnp.where(head_of_lane == h, o[h * t_new:(h + 1) * t_new], 0.0)
        out_ref[0] = res.astype(out_ref.dtype)


def _sample_attention(q, k_new, v_new, qi, ki_new, wi, cache_k, cache_v, cache_kidx, page_table,
                      rel_bias):
    nb, t_new = q.shape[:2]
    n_pages = page_table.shape[1]
    n_phys = cache_k.shape[0]
    past = n_pages * PAGE_SIZE
    topk = min(TOPK_MAX, (past + t_new) // 4)
    rows = N_HEADS * t_new
    qis = jnp.swapaxes(qi.reshape(nb, t_new, IDX_HEADS, IDX_DIM), 1, 2)
    qis = qis.reshape(nb, IDX_HEADS * t_new, IDX_DIM)
    q4 = jnp.swapaxes(q.reshape(nb, t_new, N_HEADS, HEAD_DIM), 1, 2)
    eye = jnp.eye(N_HEADS, dtype=_BF)
    qbd = (q4[:, :, :, None, :] * eye[None, :, None, :, None]).reshape(nb, rows, ATTN_W)

    def pad_rows(a):
        return jnp.pad(a, ((0, 0), (0, PAGE_SIZE - t_new), (0, 0)))

    bd = _rel_bias_by_distance(rel_bias, 2 * PAGE_SIZE)
    qo = jnp.arange(t_new)[:, None]
    co = jnp.arange(PAGE_SIZE)[None, :]
    d_last = jnp.clip(PAGE_SIZE + qo - co, 0, 2 * PAGE_SIZE - 1)
    d_new = jnp.clip(qo - co, 0, 2 * PAGE_SIZE - 1)
    bias_last = jnp.transpose(bd[d_last], (2, 0, 1)).reshape(rows, PAGE_SIZE)
    bias_new = jnp.transpose(bd[d_new], (2, 0, 1)).reshape(rows, PAGE_SIZE)
    utri = (jnp.arange(PAGE_SIZE)[:, None] < jnp.arange(PAGE_SIZE)[None, :]).astype(_BF)

    kern = functools.partial(_sample_attn_kernel, n_pages=n_pages, t_new=t_new, topk=topk)
    seq = lambda i, ph, p, pt: (i, 0, 0)
    const2 = lambda i, ph, p, pt: (0, 0)

    def kidx_page(i, ph, p, pt):
        return (jnp.where(ph == 0, pt[i * n_pages + p], pt[i * n_pages + n_pages - 1]), 0, 0)

    def kv_page(i, ph, p, pt):
        return (jnp.where(ph == 1, pt[i * n_pages + p], pt[i * n_pages]), 0, 0)

    grid_spec = pltpu.PrefetchScalarGridSpec(
        num_scalar_prefetch=1,
        grid=(nb, 2, n_pages),
        in_specs=[
            pl.BlockSpec((1, IDX_HEADS * t_new, IDX_DIM), seq),
            pl.BlockSpec((1, t_new, IDX_HEADS), seq),
            pl.BlockSpec((1, PAGE_SIZE, IDX_DIM), seq),
            pl.BlockSpec((1, rows, ATTN_W), seq),
            pl.BlockSpec((1, PAGE_SIZE, ATTN_W), seq),
            pl.BlockSpec((1, PAGE_SIZE, ATTN_W), seq),
            pl.BlockSpec((1, PAGE_SIZE, IDX_DIM), kidx_page),
            pl.BlockSpec((1, PAGE_SIZE, ATTN_W), kv_page),
            pl.BlockSpec((1, PAGE_SIZE, ATTN_W), kv_page),
            pl.BlockSpec((rows, PAGE_SIZE), const2),
            pl.BlockSpec((rows, PAGE_SIZE), const2),
            pl.BlockSpec((PAGE_SIZE, PAGE_SIZE), const2),
        ],
        out_specs=pl.BlockSpec((1, t_new, ATTN_W), seq),
        scratch_shapes=[
            pltpu.VMEM((n_pages + 1, t_new, PAGE_SIZE), jnp.int32),
            pltpu.VMEM((t_new, 1), jnp.int32),
            pltpu.VMEM((t_new, 1), _F32),
            pltpu.VMEM((t_new, 1), _F32),
            pltpu.SMEM((1,), jnp.int32),
            pltpu.VMEM((rows, 1), _F32),
            pltpu.VMEM((rows, 1), _F32),
            pltpu.VMEM((rows, ATTN_W), _F32),
        ],
    )
    return pl.pallas_call(
        kern,
        grid_spec=grid_spec,
        out_shape=jax.ShapeDtypeStruct((nb, t_new, ATTN_W), _BF),
        compiler_params=pltpu.CompilerParams(
            dimension_semantics=("arbitrary", "arbitrary", "arbitrary")),
        name="sample_sparse_attention",
    )(page_table.reshape(-1).astype(jnp.int32), qis, wi.astype(_F32), pad_rows(ki_new), qbd,
      pad_rows(k_new), pad_rows(v_new),
      cache_kidx.reshape(n_phys, PAGE_SIZE, IDX_DIM), cache_k.reshape(n_phys, PAGE_SIZE, ATTN_W),
      cache_v.reshape(n_phys, PAGE_SIZE, ATTN_W), bias_last, bias_new, utri)


def _pool_diff_kernel(u_ref, halo_ref, d_ref, *, truncate_start):
    tm = u_ref.shape[0]
    i = pl.program_id(1)
    main = u_ref[...]
    halo = jnp.where(i == 0, 0.0, halo_ref[...])
    ext = jnp.concatenate([halo, main], axis=0)
    pos = i * tm + lax.broadcasted_iota(jnp.int32, (tm, POOL_GC), 0)
    for g, w in enumerate(POOL_WINDOWS):
        sl = slice(g * POOL_GC, (g + 1) * POOL_GC)
        acc = ext[:, sl]
        shift = 1
        while shift < w:
            acc = acc + pltpu.roll(acc, shift, 0)
            shift *= 2
        wsum = acc[POOL_HALO:]
        cnt = jnp.minimum(pos + 1, w).astype(_F32) if truncate_start else float(w)
        d_ref[:, sl] = (wsum / cnt - main[:, sl]).astype(d_ref.dtype)


def _pool_diff(u, n_seq, seq_rows, truncate_start):
    tm = min(ROW_TILE, seq_rows)
    assert seq_rows % tm == 0 and tm % POOL_HALO == 0
    nt = seq_rows // tm
    hb = tm // POOL_HALO
    kern = functools.partial(_pool_diff_kernel, truncate_start=truncate_start)
    return pl.pallas_call(
        kern,
        grid=(n_seq, nt),
        in_specs=[pl.BlockSpec((tm, POOL_W), lambda b, i: (b * nt + i, 0)),
                  pl.BlockSpec((POOL_HALO, POOL_W),
                               lambda b, i: (jnp.maximum((b * nt + i) * hb - 1, 0), 0))],
        out_specs=pl.BlockSpec((tm, POOL_W), lambda b, i: (b * nt + i, 0)),
        out_shape=jax.ShapeDtypeStruct(u.shape, _BF),
        compiler_params=pltpu.CompilerParams(dimension_semantics=("arbitrary", "arbitrary")),
        name="pool_window_diff",
    )(u, u)


def _merge_route_kernel(x_ref, attn_ref, d_ref, ga_ref, gp_ref, wmap_ref, pscale_ref, wba_ref,
                        wbp_ref, wout_ref, g2_ref, wr_ref, br_ref, ltri_ref,
                        x1_ref, xn_ref, eid_ref, gate_ref, rank_ref, cnt_ref):
    tm = x_ref.shape[0]
    d = d_ref[...]
    pooled = [jnp.dot(d[:, g * POOL_GC:(g + 1) * POOL_GC], wmap_ref[g], preferred_element_type=_F32)
              for g in range(POOL_GROUPS)]
    pool = (jnp.concatenate(pooled, axis=1) * pscale_ref[...]).astype(_BF)
    a = jnp.dot(attn_ref[...], wba_ref[...], preferred_element_type=_F32)
    pp = jnp.dot(pool, wbp_ref[...], preferred_element_type=_F32)
    m = jax.nn.sigmoid(ga_ref[...]) * a + jax.nn.sigmoid(gp_ref[...]) * pp
    x1 = x_ref[...] + jnp.dot(m.astype(_BF), wout_ref[...], preferred_element_type=_F32)
    x1_ref[...] = x1
    xn = _rms_scale(x1, g2_ref[...]).astype(_BF)
    xn_ref[...] = xn

    logit = jnp.dot(xn, wr_ref[...], preferred_element_type=_F32) + br_ref[...]
    lane = lax.broadcasted_iota(jnp.int32, (tm, LANES), 1).astype(_F32)
    far = float(LANES)

    def first_lane_of(hit):
        return jnp.min(jnp.where(hit, lane, far), axis=1, keepdims=True)

    lc = jnp.where(lane < N_GROUPS, logit, NEG_BIG)
    mc = jnp.max(lc, axis=1, keepdims=True)
    p_grp = 1.0 / jnp.sum(jnp.exp(lc - mc), axis=1, keepdims=True)
    grp = first_lane_of(lc == mc)
    lo = FINE_LANE0 + EXPERTS_PER_GROUP * grp
    in_grp = (lane >= lo) & (lane < lo + EXPERTS_PER_GROUP)
    lf = jnp.where(in_grp, logit, NEG_BIG)
    ef = jnp.exp(lf - jnp.max(lf, axis=1, keepdims=True))
    pf = jnp.where(in_grp, ef / jnp.sum(ef, axis=1, keepdims=True), -1.0)
    p1 = jnp.max(pf, axis=1, keepdims=True)
    l1 = first_lane_of(pf == p1)
    pf2 = jnp.where(lane == l1, -1.0, pf)
    p2 = jnp.max(pf2, axis=1, keepdims=True)
    l2 = first_lane_of(pf2 == p2)
    e1 = l1 - FINE_LANE0
    e2 = l2 - FINE_LANE0
    g1 = p_grp * p1 / (p1 + p2)
    g2 = p_grp * p2 / (p1 + p2)
    eid_ref[...] = jnp.where(lane == 0, e1, jnp.where(lane == 1, e2, 0.0)).astype(jnp.int32)
    gate_ref[...] = jnp.where(lane == 0, g1, jnp.where(lane == 1, g2, 0.0))

    onehot = jnp.where((lane == e1) | (lane == e2), 1.0, 0.0)
    before = jnp.dot(ltri_ref[...], onehot.astype(_BF), preferred_element_type=_F32)
    r1 = jnp.sum(jnp.where(lane == e1, before, 0.0), axis=1, keepdims=True)
    r2 = jnp.sum(jnp.where(lane == e2, before, 0.0), axis=1, keepdims=True)
    rank_ref[...] = jnp.where(lane == 0, r1, jnp.where(lane == 1, r2, 0.0)).astype(jnp.int32)
    cnt_ref[0] = jnp.broadcast_to(jnp.sum(onehot, axis=0, keepdims=True), (SUBLANES, LANES))


def _merge_route(x, attn, d, ga, gp, w):
    n = x.shape[0]
    tm = ROW_TILE
    assert n % tm == 0
    nt = n // tm
    row = lambda width: pl.BlockSpec((tm, width), lambda i: (i, 0))
    full = lambda a: _resident(a.shape, lambda i: (0,) * a.ndim)
    weights = (w["wmap"], w["pscale"], w["wba"], w["wbp"], w["wout"], w["g2"], w["wr"], w["br"],
               w["ltri"])
    return pl.pallas_call(
        _merge_route_kernel,
        grid=(nt,),
        in_specs=[row(D_MODEL), row(ATTN_W), row(POOL_W), row(D_MODEL), row(D_MODEL)]
                 + [full(a) for a in weights],
        out_specs=[row(D_MODEL), row(D_MODEL), row(LANES), row(LANES), row(LANES),
                   pl.BlockSpec((1, SUBLANES, LANES), lambda i: (i, 0, 0))],
        out_shape=[jax.ShapeDtypeStruct((n, D_MODEL), _F32), jax.ShapeDtypeStruct((n, D_MODEL), _BF),
                   jax.ShapeDtypeStruct((n, LANES), jnp.int32), jax.ShapeDtypeStruct((n, LANES), _F32),
                   jax.ShapeDtypeStruct((n, LANES), jnp.int32),
                   jax.ShapeDtypeStruct((nt, SUBLANES, LANES), _F32)],
        compiler_params=pltpu.CompilerParams(dimension_semantics=("arbitrary",),
                                             vmem_limit_bytes=VMEM_LIMIT),
        name="merge_norm_route",
    )(x, attn, d, ga, gp, *weights)


def _experts_kernel(be_ref, nu_ref, x_ref, wg_ref, wu_ref, wd_ref, y_ref):
    del be_ref
    i = pl.program_id(0)

    @pl.when(i < nu_ref[0])
    def _():
        x = x_ref[...]
        hg = jnp.dot(x, wg_ref[0], preferred_element_type=_F32)
        hu = jnp.dot(x, wu_ref[0], preferred_element_type=_F32)
        hdn = (hg * jax.nn.sigmoid(hg) * hu).astype(_BF)
        y_ref[...] = jnp.dot(hdn, wd_ref[0], preferred_element_type=_F32).astype(y_ref.dtype)

    @pl.when(i >= nu_ref[0])
    def _():
        y_ref[...] = jnp.zeros(y_ref.shape, y_ref.dtype)


def _experts(x_pad, block_expert, n_used, wg, wu, wd):
    nblk = block_expert.shape[0]
    bm = MOE_ROWS
    used = lambda i, be, nu: (jnp.minimum(i, jnp.maximum(nu[0] - 1, 0)), 0)
    grid_spec = pltpu.PrefetchScalarGridSpec(
        num_scalar_prefetch=2,
        grid=(nblk,),
        in_specs=[pl.BlockSpec((bm, D_MODEL), used),
                  pl.BlockSpec((1, D_MODEL, D_EXPERT), lambda i, be, nu: (be[i], 0, 0)),
                  pl.BlockSpec((1, D_MODEL, D_EXPERT), lambda i, be, nu: (be[i], 0, 0)),
                  pl.BlockSpec((1, D_EXPERT, D_MODEL), lambda i, be, nu: (be[i], 0, 0))],
        out_specs=pl.BlockSpec((bm, D_MODEL), lambda i, be, nu: (i, 0)),
    )
    return pl.pallas_call(
        _experts_kernel,
        grid_spec=grid_spec,
        out_shape=jax.ShapeDtypeStruct((nblk * bm, D_MODEL), _BF),
        compiler_params=pltpu.CompilerParams(dimension_semantics=("arbitrary",),
                                             vmem_limit_bytes=VMEM_LIMIT),
        name="grouped_swiglu_experts",
    )(block_expert, n_used, x_pad, wg, wu, wd)


def _route_layout(eid, rank, tile_cnt):
    n = eid.shape[0]
    bm = MOE_ROWS
    cnt = tile_cnt[:, 0, :N_EXPERTS].astype(jnp.int32)
    tile_start = jnp.cumsum(cnt, axis=0) - cnt
    counts = jnp.sum(cnt, axis=0)
    padded = ((counts + bm - 1) // bm) * bm
    pad_ends = jnp.cumsum(padded)
    pad_starts = pad_ends - padded
    tile_of = jnp.arange(n, dtype=jnp.int32) // ROW_TILE
    dest = pad_starts[eid] + tile_start[tile_of[:, None], eid] + rank
    nblk = (n * TOP_K_FINE) // bm + N_EXPERTS
    block_expert = jnp.minimum(jnp.searchsorted(pad_ends, jnp.arange(nblk) * bm, side="right"),
                               N_EXPERTS - 1).astype(jnp.int32)
    n_used = (pad_ends[-1] // bm).astype(jnp.int32).reshape(1)
    return dest, block_expert, n_used, nblk


def _combine_norm_kernel(x1_ref, y1_ref, y2_ref, gate_ref, g_ref, out_ref):
    gate = gate_ref[...]
    x2 = (x1_ref[...] + gate[:, 0:1] * y1_ref[...].astype(_F32)
          + gate[:, 1:2] * y2_ref[...].astype(_F32))
    out_ref[...] = _rms_scale(x2, g_ref[...])


def _combine_norm(x1, y1, y2, gate, g):
    n = x1.shape[0]
    tm = ROW_TILE
    row = lambda width: pl.BlockSpec((tm, width), lambda i: (i, 0))
    return pl.pallas_call(
        _combine_norm_kernel,
        grid=(n // tm,),
        in_specs=[row(D_MODEL), row(D_MODEL), row(D_MODEL), row(LANES),
                  _resident((1, D_MODEL), lambda i: (0, 0))],
        out_specs=row(D_MODEL),
        out_shape=jax.ShapeDtypeStruct((n, D_MODEL), _F32),
        compiler_params=pltpu.CompilerParams(dimension_semantics=("arbitrary",)),
        name="combine_final_norm",
    )(x1, y1, y2, gate, g.reshape(1, D_MODEL).astype(_F32))


def kernel(x_prompt, x_sample, cache_k, cache_v, cache_kidx, state_pool, page_table, rel_bias,
           ln1_g, w_in, w_pool_map, pool_scale, w_br_attn, w_br_pool, w_out, ln2_g,
           w_coarse, b_coarse, w_fine, b_fine, w_gate, w_up, w_down, lnf_g):
    layer = 0
    nb, s = x_prompt.shape[:2]
    db, tn = x_sample.shape[:2]
    n_p, n_s = nb * s, db * tn
    rel_bias = rel_bias.astype(_F32)

    wr = jnp.zeros((D_MODEL, LANES), _F32)
    wr = wr.at[:, :N_GROUPS].set(w_coarse[layer])
    wr = wr.at[:, FINE_LANE0:FINE_LANE0 + N_EXPERTS].set(
        jnp.transpose(w_fine[layer], (1, 0, 2)).reshape(D_MODEL, N_EXPERTS))
    br = jnp.zeros((1, LANES), _F32)
    br = br.at[0, :N_GROUPS].set(b_coarse[layer])
    br = br.at[0, FINE_LANE0:FINE_LANE0 + N_EXPERTS].set(b_fine[layer].reshape(-1))
    mw = dict(
        wmap=w_pool_map[layer].astype(_BF), pscale=pool_scale[layer].reshape(1, POOL_W).astype(_F32),
        wba=w_br_attn[layer].astype(_BF), wbp=w_br_pool[layer].astype(_BF),
        wout=w_out[layer].astype(_BF), g2=ln2_g[layer].reshape(1, D_MODEL).astype(_F32),
        wr=wr.astype(_BF), br=br,
        ltri=(jnp.arange(ROW_TILE)[:, None] > jnp.arange(ROW_TILE)[None, :]).astype(_BF))

    pp = _project(x_prompt.reshape(n_p, D_MODEL), ln1_g[layer], w_in[layer])
    attn_p = _prompt_attention(pp, rel_bias, nb, s)
    d_p = _pool_diff(pp["u"], nb, s, True)
    x1_p, xn_p, eid_p, gate_p, rank_p, cnt_p = _merge_route(
        x_prompt.reshape(n_p, D_MODEL), attn_p, d_p, pp["ga"], pp["gp"], mw)

    ps = _project(x_sample.reshape(n_s, D_MODEL), ln1_g[layer], w_in[layer])
    q_s = ps["qT"].T.reshape(db, tn, ATTN_W)
    qi_s = ps["qiT"].T.reshape(db, tn, IDX_W)
    wi_s = ps["wiT"][:IDX_HEADS].T.reshape(db, tn, IDX_HEADS)
    attn_s = _sample_attention(q_s, ps["k16"].reshape(db, tn, ATTN_W),
                               ps["v"].astype(_BF).reshape(db, tn, ATTN_W), qi_s,
                               ps["ki16"].reshape(db, tn, IDX_DIM), wi_s,
                               cache_k[layer], cache_v[layer], cache_kidx[layer], page_table, rel_bias)
    u_s = ps["u"].reshape(db, tn, POOL_W)
    buf = jnp.concatenate([jnp.zeros((db, 1, POOL_W), _F32), state_pool[layer].astype(_F32), u_s],
                          axis=1)
    grp_rows = 1 + POOL_CTX + tn
    d_s = _pool_diff(buf.reshape(db * grp_rows, POOL_W), 1, db * grp_rows, False)
    d_s = d_s.reshape(db, grp_rows, POOL_W)[:, 1 + POOL_CTX:].reshape(n_s, POOL_W)
    x1_s, xn_s, eid_s, gate_s, rank_s, cnt_s = _merge_route(
        x_sample.reshape(n_s, D_MODEL), attn_s.reshape(n_s, ATTN_W), d_s, ps["ga"], ps["gp"], mw)

    n_all = n_p + n_s
    eid = jnp.concatenate([eid_p[:, :TOP_K_FINE], eid_s[:, :TOP_K_FINE]], axis=0)
    rank = jnp.concatenate([rank_p[:, :TOP_K_FINE], rank_s[:, :TOP_K_FINE]], axis=0)
    gate = jnp.concatenate([gate_p, gate_s], axis=0)
    xn = jnp.concatenate([xn_p, xn_s], axis=0)
    x1 = jnp.concatenate([x1_p, x1_s], axis=0)
    dest, block_expert, n_used, nblk = _route_layout(eid, rank, jnp.concatenate([cnt_p, cnt_s], axis=0))
    tok = jnp.broadcast_to(jnp.arange(n_all, dtype=jnp.int32)[:, None], dest.shape)
    slot_tok = jnp.full((nblk * MOE_ROWS,), n_all, jnp.int32).at[dest.reshape(-1)].set(tok.reshape(-1))
    x_pad = jnp.concatenate([xn, jnp.zeros((1, D_MODEL), _BF)], axis=0)[slot_tok]
    yb = _experts(x_pad, block_expert, n_used, w_gate[layer].astype(_BF), w_up[layer].astype(_BF),
                  w_down[layer].astype(_BF))
    y = _combine_norm(x1, yb[dest[:, 0]], yb[dest[:, 1]], gate, lnf_g)

    y_prompt = y[:n_p].reshape(nb, s, D_MODEL)
    y_sample = y[n_p:].reshape(db, tn, D_MODEL)
    head = lambda a, n, t: a.reshape(1, n, t, N_HEADS, HEAD_DIM)
    return (y_prompt, y_sample,
            head(pp["k"], nb, s), head(pp["v"], nb, s), pp["ki"].reshape(1, nb, s, IDX_DIM),
            pp["u"].reshape(nb, s, POOL_W)[None, :, -POOL_CTX:],
            head(ps["k"], db, tn), head(ps["v"], db, tn), ps["ki"].reshape(1, db, tn, IDX_DIM),
            buf[None, :, -POOL_CTX:])
```

```python
import functools
import math

import jax
import jax.numpy as jnp
from jax import lax
from jax.experimental import pallas as pl
from jax.experimental.pallas import tpu as pltpu

D_MODEL = 1024
N_HEADS = 8
HEAD_DIM = 64
ATTN_W = N_HEADS * HEAD_DIM
IDX_HEADS = 4
IDX_DIM = 64
IDX_W = IDX_HEADS * IDX_DIM
TOPK_MAX = 256
PAGE_SIZE = 128
REL_BUCKETS = 32
REL_MAX_EXACT = 16
REL_MAX_DIST = 128
POOL_GROUPS = 4
POOL_GC = 128
POOL_W = POOL_GROUPS * POOL_GC
POOL_WINDOWS = (2, 4, 8, 16)
POOL_CTX = 15
N_GROUPS = 4
EXPERTS_PER_GROUP = 8
N_EXPERTS = N_GROUPS * EXPERTS_PER_GROUP
TOP_K_FINE = 2
D_EXPERT = 512
RMS_EPS = 1e-6

LANES = 128
SUBLANES = 8
HEAD_PAIR = 2 * HEAD_DIM
INT_MIN = -(2 ** 31)
NEG_BIG = -1e30
VMEM_LIMIT = 56 * 1024 * 1024
ROW_TILE = 512
Q_TILE = 128
K_TILE = 512
COUNT_ROWS = 64
SUM_ROWS = 16
PAGES_PER_STEP = 8
MOE_ROWS = 256
POOL_HALO = 16
FINE_LANE0 = 8

_NT = (((1,), (1,)), ((), ()))
_BF = jnp.bfloat16
_F32 = jnp.float32


def _resident(shape, index_map):
    return pl.BlockSpec(shape, index_map, pipeline_mode=pl.Buffered(1))


def _order_key(score):
    bits = pltpu.bitcast(score, jnp.int32)
    key = bits ^ ((bits >> 31) & 0x7FFFFFFF)
    return jnp.where(key == -1, 0, key)


def _bisect_threshold(count_ge, shape, topk):
    def body(i, lo):
        inc = jnp.left_shift(jnp.int32(1), 31 - i)
        cand = lo + inc
        return jnp.where(count_ge(cand) >= topk, cand, lo)

    return lax.fori_loop(0, 32, body, jnp.full(shape, INT_MIN, jnp.int32))


def _fold_rows(x, op, rows=SUBLANES):
    while x.shape[0] > rows:
        half = x.shape[0] // 2
        x = op(x[:half], x[half:])
    return x


def _rms_scale(x, g):
    ms = jnp.mean(x * x, axis=-1, keepdims=True)
    return x * lax.rsqrt(ms + RMS_EPS) * g


_ROW_SECTIONS = (("k", ATTN_W), ("v", ATTN_W), ("u", POOL_W), ("ga", D_MODEL), ("gp", D_MODEL),
                 ("ki", LANES))
_COL_SECTIONS = (("q", ATTN_W), ("qi", IDX_W), ("v", ATTN_W), ("wi", SUBLANES))


def _proj_kernel(x_ref, g_ref, wa_ref, wbt_ref,
                 k_ref, v_ref, u_ref, ga_ref, gp_ref, ki_ref, k16_ref, ki16_ref,
                 qt_ref, qit_ref, vt_ref, wit_ref):
    h = _rms_scale(x_ref[...], g_ref[...]).astype(_BF)
    outs = {}
    lo = 0
    for name, width in _ROW_SECTIONS:
        outs[name] = jnp.dot(h, wa_ref[:, lo:lo + width], preferred_element_type=_F32)
        lo += width
    k_ref[...] = outs["k"]
    v_ref[...] = outs["v"]
    u_ref[...] = outs["u"]
    ga_ref[...] = outs["ga"]
    gp_ref[...] = outs["gp"]
    ki_ref[...] = outs["ki"][:, :IDX_DIM]
    k16_ref[...] = outs["k"].astype(_BF)
    ki16_ref[...] = outs["ki"][:, :IDX_DIM].astype(_BF)
    lo = 0
    for name, width in _COL_SECTIONS:
        t = lax.dot_general(wbt_ref[lo:lo + width, :], h, _NT, preferred_element_type=_F32)
        lo += width
        if name == "q":
            qt_ref[...] = t.astype(_BF)
        elif name == "qi":
            qit_ref[...] = t.astype(_BF)
        elif name == "v":
            vt_ref[0] = t.astype(_BF)
        else:
            wit_ref[...] = t


def _project(x, ln_g, w_in):
    n = x.shape[0]
    tm = ROW_TILE
    assert n % tm == 0 and tm == K_TILE
    widths = (ATTN_W, ATTN_W, ATTN_W, IDX_W, IDX_DIM, IDX_HEADS, POOL_W, D_MODEL, D_MODEL)
    names = ("q", "k", "v", "qi", "ki", "wi", "u", "ga", "gp")
    cols, lo = {}, 0
    for name, width in zip(names, widths):
        cols[name] = w_in[:, lo:lo + width]
        lo += width
    ki_pad = jnp.pad(cols["ki"], ((0, 0), (0, LANES - IDX_DIM)))
    wa = jnp.concatenate([cols["k"], cols["v"], cols["u"], cols["ga"], cols["gp"], ki_pad],
                         axis=1).astype(_BF)
    wi_pad = jnp.pad(cols["wi"], ((0, 0), (0, SUBLANES - IDX_HEADS)))
    wbt = jnp.concatenate([cols["q"] * HEAD_DIM ** -0.5, cols["qi"] * IDX_DIM ** -0.5, cols["v"],
                           wi_pad], axis=1).T.astype(_BF)
    na, nb = wa.shape[1], wbt.shape[0]
    row = lambda w: pl.BlockSpec((tm, w), lambda i: (i, 0))
    colb = lambda h: pl.BlockSpec((h, tm), lambda i: (0, i))
    out_shapes = dict(
        k=((n, ATTN_W), _F32, row(ATTN_W)), v=((n, ATTN_W), _F32, row(ATTN_W)),
        u=((n, POOL_W), _F32, row(POOL_W)), ga=((n, D_MODEL), _F32, row(D_MODEL)),
        gp=((n, D_MODEL), _F32, row(D_MODEL)), ki=((n, IDX_DIM), _F32, row(IDX_DIM)),
        k16=((n, ATTN_W), _BF, row(ATTN_W)), ki16=((n, IDX_DIM), _BF, row(IDX_DIM)),
        qT=((ATTN_W, n), _BF, colb(ATTN_W)), qiT=((IDX_W, n), _BF, colb(IDX_W)),
        vT=((n // tm, ATTN_W, tm), _BF, pl.BlockSpec((1, ATTN_W, tm), lambda i: (i, 0, 0))),
        wiT=((SUBLANES, n), _F32, colb(SUBLANES)),
    )
    keys = list(out_shapes)
    res = pl.pallas_call(
        _proj_kernel,
        grid=(n // tm,),
        in_specs=[row(D_MODEL), _resident((1, D_MODEL), lambda i: (0, 0)),
                  _resident((D_MODEL, na), lambda i: (0, 0)),
                  _resident((nb, D_MODEL), lambda i: (0, 0))],
        out_specs=[out_shapes[k][2] for k in keys],
        out_shape=[jax.ShapeDtypeStruct(out_shapes[k][0], out_shapes[k][1]) for k in keys],
        compiler_params=pltpu.CompilerParams(dimension_semantics=("arbitrary",),
                                             vmem_limit_bytes=VMEM_LIMIT),
        name="input_projection",
    )(x, ln_g.reshape(1, D_MODEL).astype(_F32), wa, wbt)
    return dict(zip(keys, res))


def _prompt_attn_kernel(qT_ref, qiT_ref, wiT_ref, k_ref, vT_ref, ki_ref, bias_ref, ltri_ref,
                        out_ref, keys_scr, mask_scr, qm_scr, m_scr, acc_scr, *, topk):
    TQ, TK = Q_TILE, K_TILE
    sub = TK // TQ
    qb = pl.program_id(1)
    n_sb = qb // sub + 1
    n_far = jnp.maximum(qb - 1, 0) // sub
    key_off = lax.broadcasted_iota(jnp.int32, (TK, TQ), 0)
    qry_off = lax.broadcasted_iota(jnp.int32, (TK, TQ), 1)

    def causal_at(sb):
        return (sb * TK + key_off) <= (qb * TQ + qry_off)

    qiT = qiT_ref[...]
    wT = wiT_ref[...]

    def score_keys(sb):
        kib = ki_ref[pl.ds(pl.multiple_of(sb * TK, TK), TK), :]
        sc = None
        for h in range(IDX_HEADS):
            d = jnp.dot(kib, qiT[h * IDX_DIM:(h + 1) * IDX_DIM, :], preferred_element_type=_F32)
            term = wT[h:h + 1, :] * jnp.maximum(d, 0.0)
            sc = term if sc is None else sc + term
        return _order_key(sc)

    def score_body(sb, carry):
        keys_scr[sb] = score_keys(sb)
        return carry

    lax.fori_loop(0, n_sb - 1, score_body, 0)
    keys_scr[n_sb - 1] = jnp.where(causal_at(n_sb - 1), score_keys(n_sb - 1), INT_MIN)

    def count(pred):
        def body(sb, acc):
            hit = jnp.where(pred(keys_scr[sb]), 1.0, 0.0)
            return acc + jnp.sum(hit.reshape(TK // COUNT_ROWS, COUNT_ROWS, TQ), axis=0)
        acc = lax.fori_loop(0, n_sb, body, jnp.zeros((COUNT_ROWS, TQ), _F32))
        return jnp.sum(_fold_rows(acc, jnp.add), axis=0, keepdims=True)

    kf = float(topk)
    thr = _bisect_threshold(lambda c: count(lambda kk: kk >= c), (1, TQ), kf)
    n_ge = count(lambda kk: kk >= thr)
    has_ties = jnp.max(jnp.where(n_ge != kf, 1.0, 0.0)) > 0.0
    n_tie_keep = kf - lax.cond(has_ties, lambda: count(lambda kk: kk > thr),
                               lambda: jnp.zeros((1, TQ), _F32))

    zeros_half = jnp.zeros((HEAD_DIM, TQ), _BF)
    for pair in range(N_HEADS // 2):
        q0 = qT_ref[(2 * pair) * HEAD_DIM:(2 * pair + 1) * HEAD_DIM, :]
        q1 = qT_ref[(2 * pair + 1) * HEAD_DIM:(2 * pair + 2) * HEAD_DIM, :]
        qm_scr[pair] = jnp.concatenate([jnp.concatenate([q0, zeros_half], axis=1),
                                        jnp.concatenate([zeros_half, q1], axis=1)], axis=0)
    m_scr[...] = jnp.full(m_scr.shape, NEG_BIG, _F32)
    acc_scr[...] = jnp.zeros(acc_scr.shape, _F32)
    ones_rows = jnp.ones((SUM_ROWS, TK), _BF)

    def attend_block(sb, tie_seen, near):
        keyb = keys_scr[sb]

        def tie_sel():
            eq = keyb == thr
            eqf = jnp.where(eq, 1.0, 0.0)
            rank = tie_seen + jnp.dot(ltri_ref[...], eqf.astype(_BF), preferred_element_type=_F32)
            keep = (keyb > thr) | (eq & (rank < n_tie_keep))
            return jnp.where(keep, 0.0, NEG_BIG), tie_seen + jnp.sum(eqf, axis=0, keepdims=True)

        madd, tie_seen = lax.cond(has_ties, tie_sel,
                                  lambda: (jnp.where(keyb >= thr, 0.0, NEG_BIG), tie_seen))
        if near:
            madd = jnp.where(causal_at(sb), madd, NEG_BIG)
        mask_scr[...] = madd
        row0 = pl.multiple_of(sb * TK, TK)
        for pair in range(N_HEADS // 2):
            kp = k_ref[pl.ds(row0, TK), pair * HEAD_PAIR:(pair + 1) * HEAD_PAIR]
            vp = vT_ref[sb, pair * HEAD_PAIR:(pair + 1) * HEAD_PAIR, :]
            s2 = jnp.dot(kp, qm_scr[pair], preferred_element_type=_F32)
            halves = []
            for odd in range(2):
                h = 2 * pair + odd
                s = s2[:, odd * TQ:(odd + 1) * TQ] + mask_scr[...]
                if near:
                    parts = []
                    for j in range(sub):
                        back = qb - (sb * sub + j)
                        w0 = (back == 0).astype(_F32)
                        w1 = (back == 1).astype(_F32)
                        parts.append(s[j * TQ:(j + 1) * TQ] + w0 * bias_ref[0, h] + w1 * bias_ref[1, h])
                    s = jnp.concatenate(parts, axis=0)
                halves.append(s)
            m_old = m_scr[pair]
            m_blk = jnp.concatenate(
                [jnp.max(_fold_rows(s, jnp.maximum), axis=0, keepdims=True) for s in halves], axis=1)
            m_new = jnp.maximum(m_old, m_blk)
            p2 = jnp.concatenate([jnp.exp(s - m_new[:, odd * TQ:(odd + 1) * TQ]).astype(_BF)
                                  for odd, s in enumerate(halves)], axis=1)
            alpha = jnp.exp(m_old - m_new)
            va = jnp.concatenate([vp, ones_rows], axis=0)
            acc_scr[pair] = alpha * acc_scr[pair] + jnp.dot(va, p2, preferred_element_type=_F32)
            m_scr[pair] = m_new
        return tie_seen

    tie_seen = lax.fori_loop(0, n_far, lambda sb, c: attend_block(sb, c, False),
                             jnp.zeros((1, TQ), _F32))
    lax.fori_loop(n_far, n_sb, lambda sb, c: attend_block(sb, c, True), tie_seen)

    for pair in range(N_HEADS // 2):
        a = acc_scr[pair]
        res = jnp.concatenate(
            [a[:HEAD_DIM, :TQ] / a[HEAD_PAIR:HEAD_PAIR + 1, :TQ],
             a[HEAD_DIM:HEAD_PAIR, TQ:] / a[HEAD_PAIR:HEAD_PAIR + 1, TQ:]], axis=0)
        out_ref[:, pair * HEAD_PAIR:(pair + 1) * HEAD_PAIR] = res.T.astype(out_ref.dtype)


def _rel_bias_by_distance(rel_bias, n):
    dist = jnp.arange(n, dtype=jnp.int32)
    nf = jnp.maximum(dist, 1).astype(_F32)
    large = REL_MAX_EXACT + (jnp.log(nf / REL_MAX_EXACT) / math.log(REL_MAX_DIST / REL_MAX_EXACT)
                             * (REL_BUCKETS - REL_MAX_EXACT)).astype(jnp.int32)
    large = jnp.minimum(large, REL_BUCKETS - 1)
    bucket = jnp.where(dist < REL_MAX_EXACT, dist, large)
    return (rel_bias[bucket] - rel_bias[REL_BUCKETS - 1][None, :]).astype(_F32)


def _prompt_attention(proj, rel_bias, nb, s):
    TQ, TK = Q_TILE, K_TILE
    assert s % TK == 0
    nq, nsb = s // TQ, s // TK
    topk = min(TOPK_MAX, s // 4)
    bd = _rel_bias_by_distance(rel_bias, 2 * TQ)
    key_off = jnp.arange(TQ)[:, None]
    qry_off = jnp.arange(TQ)[None, :]
    tiles = [jnp.transpose(bd[jnp.clip(back * TQ + qry_off - key_off, 0, 2 * TQ - 1)], (2, 0, 1))
             for back in range(2)]
    bias_tiles = jnp.stack(tiles)
    ltri = (jnp.arange(TK)[:, None] > jnp.arange(TK)[None, :]).astype(_BF)

    kern = functools.partial(_prompt_attn_kernel, topk=topk)
    return pl.pallas_call(
        kern,
        grid=(nb, nq),
        in_specs=[
            pl.BlockSpec((ATTN_W, TQ), lambda i, j: (0, i * nq + j)),
            pl.BlockSpec((IDX_W, TQ), lambda i, j: (0, i * nq + j)),
            pl.BlockSpec((SUBLANES, TQ), lambda i, j: (0, i * nq + j)),
            _resident((s, ATTN_W), lambda i, j: (i, 0)),
            _resident((nsb, ATTN_W, TK), lambda i, j: (i, 0, 0)),
            _resident((s, IDX_DIM), lambda i, j: (i, 0)),
            _resident((2, N_HEADS, TQ, TQ), lambda i, j: (0, 0, 0, 0)),
            _resident((TK, TK), lambda i, j: (0, 0)),
        ],
        out_specs=pl.BlockSpec((TQ, ATTN_W), lambda i, j: (i * nq + j, 0)),
        out_shape=jax.ShapeDtypeStruct((nb * s, ATTN_W), _BF),
        scratch_shapes=[
            pltpu.VMEM((nsb, TK, TQ), jnp.int32),
            pltpu.VMEM((TK, TQ), _F32),
            pltpu.VMEM((N_HEADS // 2, HEAD_PAIR, 2 * TQ), _BF),
            pltpu.VMEM((N_HEADS // 2, 1, 2 * TQ), _F32),
            pltpu.VMEM((N_HEADS // 2, HEAD_PAIR + SUM_ROWS, 2 * TQ), _F32),
        ],
        compiler_params=pltpu.CompilerParams(
            dimension_semantics=("arbitrary", "arbitrary"), vmem_limit_bytes=VMEM_LIMIT),
        name="prompt_sparse_attention",
    )(proj["qT"], proj["qiT"], proj["wiT"], proj["k16"], proj["vT"], proj["ki16"], bias_tiles, ltri)


def _sample_attn_kernel(pt_ref, qis_ref, wis_ref, kinew_ref, qflat_ref, knew_ref, vnew_ref,
                        *rest, n_pages, t_new, topk):
    del pt_ref
    G = PAGES_PER_STEP
    kidx_refs, k_refs, v_refs = rest[:G], rest[G:2 * G], rest[2 * G:3 * G]
    (biasl_ref, biasn_ref, hmask_ref, expand_ref, utri_ref, out_ref, keys_scr, thr_scr, keep_scr,
     tie_scr, flag_scr, m_scr, l_scr, acc_scr) = rest[3 * G:]
    NP = n_pages
    n_steps = NP // G
    phase = pl.program_id(1)
    p = pl.program_id(2)

    def score_keys(kib):
        d = lax.dot_general(qis_ref[0], kib, _NT, preferred_element_type=_F32)
        w = wis_ref[0]
        sc = None
        for h in range(IDX_HEADS):
            term = w[:, h:h + 1] * jnp.maximum(d[h * t_new:(h + 1) * t_new], 0.0)
            sc = term if sc is None else sc + term
        return _order_key(sc)

    @pl.when(phase == 0)
    def _():
        for j in range(G):
            keys_scr[p * G + j] = score_keys(kidx_refs[j][0].astype(_BF))

    @pl.when((phase == 0) & (p == n_steps - 1))
    def _():
        lane = lax.broadcasted_iota(jnp.int32, (t_new, PAGE_SIZE), 1)
        qrow = lax.broadcasted_iota(jnp.int32, (t_new, PAGE_SIZE), 0)
        keys_scr[NP] = jnp.where(lane <= qrow, score_keys(kinew_ref[0]), INT_MIN)

        def count(pred):
            def body(j, acc):
                return acc + jnp.where(pred(keys_scr[j]), 1.0, 0.0)
            acc = lax.fori_loop(0, NP + 1, body, jnp.zeros((t_new, PAGE_SIZE), _F32))
            return jnp.sum(acc, axis=1, keepdims=True)

        kf = float(topk)
        thr = _bisect_threshold(lambda c: count(lambda kk: kk >= c), (t_new, 1), kf)
        n_ge = count(lambda kk: kk >= thr)
        n_gt = count(lambda kk: kk > thr)
        thr_scr[...] = thr
        keep_scr[...] = kf - n_gt
        flag_scr[0] = (jnp.max(jnp.where(n_ge != kf, 1.0, 0.0)) > 0.0).astype(jnp.int32)
        tie_scr[...] = jnp.zeros(tie_scr.shape, _F32)
        m_scr[...] = jnp.full(m_scr.shape, NEG_BIG, _F32)
        l_scr[...] = jnp.zeros(l_scr.shape, _F32)
        acc_scr[...] = jnp.zeros(acc_scr.shape, _F32)

    def attend(keyb, kmat, vmat, bias):
        thr = thr_scr[...]

        def tie_sel():
            eq = keyb == thr
            eqf = jnp.where(eq, 1.0, 0.0)
            rank = tie_scr[...] + jnp.dot(eqf.astype(_BF), utri_ref[...],
                                          preferred_element_type=_F32)
            keep = (keyb > thr) | (eq & (rank < keep_scr[...]))
            tie_scr[...] = tie_scr[...] + jnp.sum(eqf, axis=1, keepdims=True)
            return jnp.where(keep, 1.0, 0.0)

        sel8 = lax.cond(flag_scr[0] != 0, tie_sel, lambda: jnp.where(keyb >= thr, 1.0, 0.0))
        selx = jnp.dot(sel8.astype(_BF), expand_ref[...], preferred_element_type=_F32)
        madd8 = jnp.where(selx > 0.5, 0.0, NEG_BIG)
        madd = jnp.concatenate([madd8] * N_HEADS, axis=0) + hmask_ref[...]
        s = lax.dot_general(qflat_ref[0], kmat, _NT, preferred_element_type=_F32) + madd
        if bias is not None:
            s = s + bias
        m_old = m_scr[...]
        m_new = jnp.maximum(m_old, jnp.max(s, axis=1, keepdims=True))
        pr = jnp.exp(s - m_new)
        alpha = jnp.exp(m_old - m_new)
        l_scr[...] = alpha * l_scr[...] + jnp.sum(pr, axis=1, keepdims=True)
        acc_scr[...] = alpha * acc_scr[...] + jnp.dot(pr.astype(_BF), vmat,
                                                      preferred_element_type=_F32)
        m_scr[...] = m_new

    @pl.when(phase == 1)
    def _():
        for j in range(G):
            bias = None
            if j == G - 1:
                bias = (p == n_steps - 1).astype(_F32) * biasl_ref[...]
            attend(keys_scr[p * G + j], k_refs[j][0].astype(_BF), v_refs[j][0].astype(_BF), bias)

    @pl.when((phase == 1) & (p == n_steps - 1))
    def _():
        attend(keys_scr[NP], knew_ref[0], vnew_ref[0], biasn_ref[...])
        out_ref[0] = (acc_scr[...] / l_scr[...]).astype(out_ref.dtype)


def _sample_attention(q, k_new, v_new, qi, ki_new, wi, cache_k, cache_v, cache_kidx, page_table,
                      rel_bias):
    nb, t_new = q.shape[:2]
    n_pages = page_table.shape[1]
    n_phys = cache_k.shape[0]
    past = n_pages * PAGE_SIZE
    topk = min(TOPK_MAX, (past + t_new) // 4)
    rows = N_HEADS * t_new
    qis = jnp.swapaxes(qi.reshape(nb, t_new, IDX_HEADS, IDX_DIM), 1, 2)
    qis = qis.reshape(nb, IDX_HEADS * t_new, IDX_DIM)
    G = PAGES_PER_STEP
    assert n_pages % G == 0
    n_steps = n_pages // G
    page_rows = PAGE_SIZE * N_HEADS
    qflat = jnp.swapaxes(q.reshape(nb, t_new, N_HEADS, HEAD_DIM), 1, 2).reshape(nb, rows, HEAD_DIM)

    def pad_rows(a, n):
        return jnp.pad(a, ((0, 0), (0, n - a.shape[1]), (0, 0)))

    bd = _rel_bias_by_distance(rel_bias, 2 * PAGE_SIZE)
    qo = jnp.arange(t_new)[:, None]
    co = jnp.arange(PAGE_SIZE)[None, :]
    d_last = jnp.clip(PAGE_SIZE + qo - co, 0, 2 * PAGE_SIZE - 1)
    d_new = jnp.clip(qo - co, 0, 2 * PAGE_SIZE - 1)
    per_head_lanes = lambda b: jnp.repeat(jnp.transpose(b, (2, 0, 1)).reshape(rows, PAGE_SIZE),
                                          N_HEADS, axis=1)
    bias_last = per_head_lanes(bd[d_last])
    bias_new = per_head_lanes(bd[d_new])
    row_head = jnp.arange(rows)[:, None] // t_new
    lane_head = jnp.arange(page_rows)[None, :] % N_HEADS
    hmask = jnp.where(row_head == lane_head, 0.0, NEG_BIG).astype(_F32)
    expand = (jnp.arange(PAGE_SIZE)[:, None] == jnp.arange(page_rows)[None, :] // N_HEADS).astype(_BF)
    utri = (jnp.arange(PAGE_SIZE)[:, None] < jnp.arange(PAGE_SIZE)[None, :]).astype(_BF)

    kern = functools.partial(_sample_attn_kernel, n_pages=n_pages, t_new=t_new, topk=topk)
    seq = lambda i, ph, p, pt: (i, 0, 0)
    const2 = lambda i, ph, p, pt: (0, 0)

    def kidx_page(j):
        return lambda i, ph, p, pt: (
            jnp.where(ph == 0, pt[i * n_pages + p * G + j], pt[i * n_pages + n_pages - G + j]), 0, 0)

    def kv_page(j):
        return lambda i, ph, p, pt: (
            jnp.where(ph == 1, pt[i * n_pages + p * G + j], pt[i * n_pages + j]), 0, 0)

    grid_spec = pltpu.PrefetchScalarGridSpec(
        num_scalar_prefetch=1,
        grid=(nb, 2, n_steps),
        in_specs=[
            pl.BlockSpec((1, IDX_HEADS * t_new, IDX_DIM), seq),
            pl.BlockSpec((1, t_new, IDX_HEADS), seq),
            pl.BlockSpec((1, PAGE_SIZE, IDX_DIM), seq),
            pl.BlockSpec((1, rows, HEAD_DIM), seq),
            pl.BlockSpec((1, page_rows, HEAD_DIM), seq),
            pl.BlockSpec((1, page_rows, HEAD_DIM), seq),
        ] + [pl.BlockSpec((1, PAGE_SIZE, IDX_DIM), kidx_page(j)) for j in range(G)]
          + [pl.BlockSpec((1, page_rows, HEAD_DIM), kv_page(j)) for j in range(G)]
          + [pl.BlockSpec((1, page_rows, HEAD_DIM), kv_page(j)) for j in range(G)]
          + [
            pl.BlockSpec((rows, page_rows), const2),
            pl.BlockSpec((rows, page_rows), const2),
            pl.BlockSpec((rows, page_rows), const2),
            pl.BlockSpec((PAGE_SIZE, page_rows), const2),
            pl.BlockSpec((PAGE_SIZE, PAGE_SIZE), const2),
        ],
        out_specs=pl.BlockSpec((1, rows, HEAD_DIM), seq),
        scratch_shapes=[
            pltpu.VMEM((n_pages + 1, t_new, PAGE_SIZE), jnp.int32),
            pltpu.VMEM((t_new, 1), jnp.int32),
            pltpu.VMEM((t_new, 1), _F32),
            pltpu.VMEM((t_new, 1), _F32),
            pltpu.SMEM((1,), jnp.int32),
            pltpu.VMEM((rows, 1), _F32),
            pltpu.VMEM((rows, 1), _F32),
            pltpu.VMEM((rows, HEAD_DIM), _F32),
        ],
    )
    kidx_pages = cache_kidx.reshape(n_phys, PAGE_SIZE, IDX_DIM)
    k_pages = cache_k.reshape(n_phys, page_rows, HEAD_DIM)
    v_pages = cache_v.reshape(n_phys, page_rows, HEAD_DIM)
    out = pl.pallas_call(
        kern,
        grid_spec=grid_spec,
        out_shape=jax.ShapeDtypeStruct((nb, rows, HEAD_DIM), _BF),
        compiler_params=pltpu.CompilerParams(
            dimension_semantics=("arbitrary", "arbitrary", "arbitrary"),
            vmem_limit_bytes=VMEM_LIMIT),
        name="sample_sparse_attention",
    )(page_table.reshape(-1).astype(jnp.int32), qis, wi.astype(_F32), pad_rows(ki_new, PAGE_SIZE),
      qflat, pad_rows(k_new.reshape(nb, t_new * N_HEADS, HEAD_DIM), page_rows),
      pad_rows(v_new.reshape(nb, t_new * N_HEADS, HEAD_DIM), page_rows),
      *([kidx_pages] * G), *([k_pages] * G), *([v_pages] * G),
      bias_last, bias_new, hmask, expand, utri)
    return jnp.swapaxes(out.reshape(nb, N_HEADS, t_new, HEAD_DIM), 1, 2).reshape(nb, t_new, ATTN_W)


def _pool_diff_kernel(u_ref, halo_ref, d_ref, *, truncate_start):
    tm = u_ref.shape[0]
    i = pl.program_id(1)
    main = u_ref[...]
    halo = jnp.where(i == 0, 0.0, halo_ref[...])
    ext = jnp.concatenate([halo, main], axis=0)
    pos = i * tm + lax.broadcasted_iota(jnp.int32, (tm, POOL_GC), 0)
    for g, w in enumerate(POOL_WINDOWS):
        sl = slice(g * POOL_GC, (g + 1) * POOL_GC)
        acc = ext[:, sl]
        shift = 1
        while shift < w:
            acc = acc + pltpu.roll(acc, shift, 0)
            shift *= 2
        wsum = acc[POOL_HALO:]
        cnt = jnp.minimum(pos + 1, w).astype(_F32) if truncate_start else float(w)
        d_ref[:, sl] = (wsum / cnt - main[:, sl]).astype(d_ref.dtype)


def _pool_diff(u, n_seq, seq_rows, truncate_start):
    tm = min(ROW_TILE, seq_rows)
    assert seq_rows % tm == 0 and tm % POOL_HALO == 0
    nt = seq_rows // tm
    hb = tm // POOL_HALO
    kern = functools.partial(_pool_diff_kernel, truncate_start=truncate_start)
    return pl.pallas_call(
        kern,
        grid=(n_seq, nt),
        in_specs=[pl.BlockSpec((tm, POOL_W), lambda b, i: (b * nt + i, 0)),
                  pl.BlockSpec((POOL_HALO, POOL_W),
                               lambda b, i: (jnp.maximum((b * nt + i) * hb - 1, 0), 0))],
        out_specs=pl.BlockSpec((tm, POOL_W), lambda b, i: (b * nt + i, 0)),
        out_shape=jax.ShapeDtypeStruct(u.shape, _BF),
        compiler_params=pltpu.CompilerParams(dimension_semantics=("arbitrary", "arbitrary")),
        name="pool_window_diff",
    )(u, u)


def _merge_route_kernel(x_ref, attn_ref, d_ref, ga_ref, gp_ref, wmap_ref, pscale_ref, wba_ref,
                        wbp_ref, wout_ref, g2_ref, wr_ref, br_ref, ltri_ref,
                        x1_ref, xn_ref, eid_ref, gate_ref, rank_ref, cnt_ref):
    tm = x_ref.shape[0]
    d = d_ref[...]
    pooled = [jnp.dot(d[:, g * POOL_GC:(g + 1) * POOL_GC], wmap_ref[g], preferred_element_type=_F32)
              for g in range(POOL_GROUPS)]
    pool = (jnp.concatenate(pooled, axis=1) * pscale_ref[...]).astype(_BF)
    a = jnp.dot(attn_ref[...], wba_ref[...], preferred_element_type=_F32)
    pp = jnp.dot(pool, wbp_ref[...], preferred_element_type=_F32)
    m = jax.nn.sigmoid(ga_ref[...]) * a + jax.nn.sigmoid(gp_ref[...]) * pp
    x1 = x_ref[...] + jnp.dot(m.astype(_BF), wout_ref[...], preferred_element_type=_F32)
    x1_ref[...] = x1
    xn = _rms_scale(x1, g2_ref[...]).astype(_BF)
    xn_ref[...] = xn

    logit = jnp.dot(xn, wr_ref[...], preferred_element_type=_F32) + br_ref[...]
    lane = lax.broadcasted_iota(jnp.int32, (tm, LANES), 1).astype(_F32)
    far = float(LANES)

    def first_lane_of(hit):
        return jnp.min(jnp.where(hit, lane, far), axis=1, keepdims=True)

    lc = jnp.where(lane < N_GROUPS, logit, NEG_BIG)
    mc = jnp.max(lc, axis=1, keepdims=True)
    p_grp = 1.0 / jnp.sum(jnp.exp(lc - mc), axis=1, keepdims=True)
    grp = first_lane_of(lc == mc)
    lo = FINE_LANE0 + EXPERTS_PER_GROUP * grp
    in_grp = (lane >= lo) & (lane < lo + EXPERTS_PER_GROUP)
    lf = jnp.where(in_grp, logit, NEG_BIG)
    ef = jnp.exp(lf - jnp.max(lf, axis=1, keepdims=True))
    pf = jnp.where(in_grp, ef / jnp.sum(ef, axis=1, keepdims=True), -1.0)
    p1 = jnp.max(pf, axis=1, keepdims=True)
    l1 = first_lane_of(pf == p1)
    pf2 = jnp.where(lane == l1, -1.0, pf)
    p2 = jnp.max(pf2, axis=1, keepdims=True)
    l2 = first_lane_of(pf2 == p2)
    e1 = l1 - FINE_LANE0
    e2 = l2 - FINE_LANE0
    g1 = p_grp * p1 / (p1 + p2)
    g2 = p_grp * p2 / (p1 + p2)
    eid_ref[...] = jnp.where(lane == 0, e1, jnp.where(lane == 1, e2, 0.0)).astype(jnp.int32)
    gate_ref[...] = jnp.where(lane == 0, g1, jnp.where(lane == 1, g2, 0.0))

    onehot = jnp.where((lane == e1) | (lane == e2), 1.0, 0.0)
    before = jnp.dot(ltri_ref[...], onehot.astype(_BF), preferred_element_type=_F32)
    r1 = jnp.sum(jnp.where(lane == e1, before, 0.0), axis=1, keepdims=True)
    r2 = jnp.sum(jnp.where(lane == e2, before, 0.0), axis=1, keepdims=True)
    rank_ref[...] = jnp.where(lane == 0, r1, jnp.where(lane == 1, r2, 0.0)).astype(jnp.int32)
    cnt_ref[0] = jnp.broadcast_to(jnp.sum(onehot, axis=0, keepdims=True), (SUBLANES, LANES))


def _merge_route(x, attn, d, ga, gp, w):
    n = x.shape[0]
    tm = ROW_TILE
    assert n % tm == 0
    nt = n // tm
    row = lambda width: pl.BlockSpec((tm, width), lambda i: (i, 0))
    full = lambda a: _resident(a.shape, lambda i: (0,) * a.ndim)
    weights = (w["wmap"], w["pscale"], w["wba"], w["wbp"], w["wout"], w["g2"], w["wr"], w["br"],
               w["ltri"])
    return pl.pallas_call(
        _merge_route_kernel,
        grid=(nt,),
        in_specs=[row(D_MODEL), row(ATTN_W), row(POOL_W), row(D_MODEL), row(D_MODEL)]
                 + [full(a) for a in weights],
        out_specs=[row(D_MODEL), row(D_MODEL), row(LANES), row(LANES), row(LANES),
                   pl.BlockSpec((1, SUBLANES, LANES), lambda i: (i, 0, 0))],
        out_shape=[jax.ShapeDtypeStruct((n, D_MODEL), _F32), jax.ShapeDtypeStruct((n, D_MODEL), _BF),
                   jax.ShapeDtypeStruct((n, LANES), jnp.int32), jax.ShapeDtypeStruct((n, LANES), _F32),
                   jax.ShapeDtypeStruct((n, LANES), jnp.int32),
                   jax.ShapeDtypeStruct((nt, SUBLANES, LANES), _F32)],
        compiler_params=pltpu.CompilerParams(dimension_semantics=("arbitrary",),
                                             vmem_limit_bytes=VMEM_LIMIT),
        name="merge_norm_route",
    )(x, attn, d, ga, gp, *weights)


def _experts_kernel(be_ref, nu_ref, x_ref, wg_ref, wu_ref, wd_ref, y_ref):
    del be_ref
    i = pl.program_id(0)

    @pl.when(i < nu_ref[0])
    def _():
        x = x_ref[...]
        hg = jnp.dot(x, wg_ref[0], preferred_element_type=_F32)
        hu = jnp.dot(x, wu_ref[0], preferred_element_type=_F32)
        hdn = (hg * jax.nn.sigmoid(hg) * hu).astype(_BF)
        y_ref[...] = jnp.dot(hdn, wd_ref[0], preferred_element_type=_F32).astype(y_ref.dtype)

    @pl.when(i >= nu_ref[0])
    def _():
        y_ref[...] = jnp.zeros(y_ref.shape, y_ref.dtype)


def _experts(x_pad, block_expert, n_used, wg, wu, wd):
    nblk = block_expert.shape[0]
    bm = MOE_ROWS
    used = lambda i, be, nu: (jnp.minimum(i, jnp.maximum(nu[0] - 1, 0)), 0)
    grid_spec = pltpu.PrefetchScalarGridSpec(
        num_scalar_prefetch=2,
        grid=(nblk,),
        in_specs=[pl.BlockSpec((bm, D_MODEL), used),
                  pl.BlockSpec((1, D_MODEL, D_EXPERT), lambda i, be, nu: (be[i], 0, 0)),
                  pl.BlockSpec((1, D_MODEL, D_EXPERT), lambda i, be, nu: (be[i], 0, 0)),
                  pl.BlockSpec((1, D_EXPERT, D_MODEL), lambda i, be, nu: (be[i], 0, 0))],
        out_specs=pl.BlockSpec((bm, D_MODEL), lambda i, be, nu: (i, 0)),
    )
    return pl.pallas_call(
        _experts_kernel,
        grid_spec=grid_spec,
        out_shape=jax.ShapeDtypeStruct((nblk * bm, D_MODEL), _BF),
        compiler_params=pltpu.CompilerParams(dimension_semantics=("arbitrary",),
                                             vmem_limit_bytes=VMEM_LIMIT),
        name="grouped_swiglu_experts",
    )(block_expert, n_used, x_pad, wg, wu, wd)


def _route_layout(eid, rank, tile_cnt):
    n = eid.shape[0]
    bm = MOE_ROWS
    cnt = tile_cnt[:, 0, :N_EXPERTS].astype(jnp.int32)
    tile_start = jnp.cumsum(cnt, axis=0) - cnt
    counts = jnp.sum(cnt, axis=0)
    padded = ((counts + bm - 1) // bm) * bm
    pad_ends = jnp.cumsum(padded)
    pad_starts = pad_ends - padded
    tile_of = jnp.arange(n, dtype=jnp.int32) // ROW_TILE
    dest = pad_starts[eid] + tile_start[tile_of[:, None], eid] + rank
    nblk = (n * TOP_K_FINE) // bm + N_EXPERTS
    block_expert = jnp.minimum(jnp.searchsorted(pad_ends, jnp.arange(nblk) * bm, side="right"),
                               N_EXPERTS - 1).astype(jnp.int32)
    n_used = (pad_ends[-1] // bm).astype(jnp.int32).reshape(1)
    return dest, block_expert, n_used, nblk


def _combine_norm_kernel(x1_ref, y1_ref, y2_ref, gate_ref, g_ref, out_ref):
    gate = gate_ref[...]
    x2 = (x1_ref[...] + gate[:, 0:1] * y1_ref[...].astype(_F32)
          + gate[:, 1:2] * y2_ref[...].astype(_F32))
    out_ref[...] = _rms_scale(x2, g_ref[...])


def _combine_norm(x1, y1, y2, gate, g):
    n = x1.shape[0]
    tm = ROW_TILE
    row = lambda width: pl.BlockSpec((tm, width), lambda i: (i, 0))
    return pl.pallas_call(
        _combine_norm_kernel,
        grid=(n // tm,),
        in_specs=[row(D_MODEL), row(D_MODEL), row(D_MODEL), row(LANES),
                  _resident((1, D_MODEL), lambda i: (0, 0))],
        out_specs=row(D_MODEL),
        out_shape=jax.ShapeDtypeStruct((n, D_MODEL), _F32),
        compiler_params=pltpu.CompilerParams(dimension_semantics=("arbitrary",)),
        name="combine_final_norm",
    )(x1, y1, y2, gate, g.reshape(1, D_MODEL).astype(_F32))


def kernel(x_prompt, x_sample, cache_k, cache_v, cache_kidx, state_pool, page_table, rel_bias,
           ln1_g, w_in, w_pool_map, pool_scale, w_br_attn, w_br_pool, w_out, ln2_g,
           w_coarse, b_coarse, w_fine, b_fine, w_gate, w_up, w_down, lnf_g):
    layer = 0
    nb, s = x_prompt.shape[:2]
    db, tn = x_sample.shape[:2]
    n_p, n_s = nb * s, db * tn
    rel_bias = rel_bias.astype(_F32)

    wr = jnp.zeros((D_MODEL, LANES), _F32)
    wr = wr.at[:, :N_GROUPS].set(w_coarse[layer])
    wr = wr.at[:, FINE_LANE0:FINE_LANE0 + N_EXPERTS].set(
        jnp.transpose(w_fine[layer], (1, 0, 2)).reshape(D_MODEL, N_EXPERTS))
    br = jnp.zeros((1, LANES), _F32)
    br = br.at[0, :N_GROUPS].set(b_coarse[layer])
    br = br.at[0, FINE_LANE0:FINE_LANE0 + N_EXPERTS].set(b_fine[layer].reshape(-1))
    mw = dict(
        wmap=w_pool_map[layer].astype(_BF), pscale=pool_scale[layer].reshape(1, POOL_W).astype(_F32),
        wba=w_br_attn[layer].astype(_BF), wbp=w_br_pool[layer].astype(_BF),
        wout=w_out[layer].astype(_BF), g2=ln2_g[layer].reshape(1, D_MODEL).astype(_F32),
        wr=wr.astype(_BF), br=br,
        ltri=(jnp.arange(ROW_TILE)[:, None] > jnp.arange(ROW_TILE)[None, :]).astype(_BF))

    pp = _project(x_prompt.reshape(n_p, D_MODEL), ln1_g[layer], w_in[layer])
    attn_p = _prompt_attention(pp, rel_bias, nb, s)
    d_p = _pool_diff(pp["u"], nb, s, True)
    x1_p, xn_p, eid_p, gate_p, rank_p, cnt_p = _merge_route(
        x_prompt.reshape(n_p, D_MODEL), attn_p, d_p, pp["ga"], pp["gp"], mw)

    ps = _project(x_sample.reshape(n_s, D_MODEL), ln1_g[layer], w_in[layer])
    q_s = ps["qT"].T.reshape(db, tn, ATTN_W)
    qi_s = ps["qiT"].T.reshape(db, tn, IDX_W)
    wi_s = ps["wiT"][:IDX_HEADS].T.reshape(db, tn, IDX_HEADS)
    attn_s = _sample_attention(q_s, ps["k16"].reshape(db, tn, ATTN_W),
                               ps["v"].astype(_BF).reshape(db, tn, ATTN_W), qi_s,
                               ps["ki16"].reshape(db, tn, IDX_DIM), wi_s,
                               cache_k[layer], cache_v[layer], cache_kidx[layer], page_table, rel_bias)
    u_s = ps["u"].reshape(db, tn, POOL_W)
    buf = jnp.concatenate([jnp.zeros((db, 1, POOL_W), _F32), state_pool[layer].astype(_F32), u_s],
                          axis=1)
    grp_rows = 1 + POOL_CTX + tn
    d_s = _pool_diff(buf.reshape(db * grp_rows, POOL_W), 1, db * grp_rows, False)
    d_s = d_s.reshape(db, grp_rows, POOL_W)[:, 1 + POOL_CTX:].reshape(n_s, POOL_W)
    x1_s, xn_s, eid_s, gate_s, rank_s, cnt_s = _merge_route(
        x_sample.reshape(n_s, D_MODEL), attn_s.reshape(n_s, ATTN_W), d_s, ps["ga"], ps["gp"], mw)

    n_all = n_p + n_s
    eid = jnp.concatenate([eid_p[:, :TOP_K_FINE], eid_s[:, :TOP_K_FINE]], axis=0)
    rank = jnp.concatenate([rank_p[:, :TOP_K_FINE], rank_s[:, :TOP_K_FINE]], axis=0)
    gate = jnp.concatenate([gate_p, gate_s], axis=0)
    xn = jnp.concatenate([xn_p, xn_s], axis=0)
    x1 = jnp.concatenate([x1_p, x1_s], axis=0)
    dest, block_expert, n_used, nblk = _route_layout(eid, rank, jnp.concatenate([cnt_p, cnt_s], axis=0))
    tok = jnp.broadcast_to(jnp.arange(n_all, dtype=jnp.int32)[:, None], dest.shape)
    slot_tok = jnp.full((nblk * MOE_ROWS,), n_all, jnp.int32).at[dest.reshape(-1)].set(tok.reshape(-1))
    x_pad = jnp.concatenate([xn, jnp.zeros((1, D_MODEL), _BF)], axis=0)[slot_tok]
    yb = _experts(x_pad, block_expert, n_used, w_gate[layer].astype(_BF), w_up[layer].astype(_BF),
                  w_down[layer].astype(_BF))
    y = _combine_norm(x1, yb[dest[:, 0]], yb[dest[:, 1]], gate, lnf_g)

    y_prompt = y[:n_p].reshape(nb, s, D_MODEL)
    y_sample = y[n_p:].reshape(db, tn, D_MODEL)
    head = lambda a, n, t: a.reshape(1, n, t, N_HEADS, HEAD_DIM)
    return (y_prompt, y_sample,
            head(pp["k"], nb, s), head(pp["v"], nb, s), pp["ki"].reshape(1, nb, s, IDX_DIM),
            pp["u"].reshape(nb, s, POOL_W)[None, :, -POOL_CTX:],
            head(ps["k"], db, tn), head(ps["v"], db, tn), ps["ki"].reshape(1, db, tn, IDX_DIM),
            buf[None, :, -POOL_CTX:])
```

```python
import functools
import math

import jax
import jax.numpy as jnp
from jax import lax
from jax.experimental import pallas as pl
from jax.experimental.pallas import tpu as pltpu

D_MODEL = 1024
N_HEADS = 8
HEAD_DIM = 64
ATTN_W = N_HEADS * HEAD_DIM
IDX_HEADS = 4
IDX_DIM = 64
IDX_W = IDX_HEADS * IDX_DIM
TOPK_MAX = 256
PAGE_SIZE = 128
REL_BUCKETS = 32
REL_MAX_EXACT = 16
REL_MAX_DIST = 128
POOL_GROUPS = 4
POOL_GC = 128
POOL_W = POOL_GROUPS * POOL_GC
POOL_WINDOWS = (2, 4, 8, 16)
POOL_CTX = 15
N_GROUPS = 4
EXPERTS_PER_GROUP = 8
N_EXPERTS = N_GROUPS * EXPERTS_PER_GROUP
TOP_K_FINE = 2
D_EXPERT = 512
RMS_EPS = 1e-6

LANES = 128
SUBLANES = 8
HEAD_PAIR = 2 * HEAD_DIM
INT_MIN = -(2 ** 31)
NEG_BIG = -1e30
VMEM_LIMIT = 56 * 1024 * 1024
ROW_TILE = 512
Q_TILE = 128
K_TILE = 512
COUNT_ROWS = 64
SUM_ROWS = 16
PAGES_PER_STEP = 8
MOE_ROWS = 256
POOL_HALO = 16
FINE_LANE0 = 8

_NT = (((1,), (1,)), ((), ()))
_BF = jnp.bfloat16
_F32 = jnp.float32


def _resident(shape, index_map):
    return pl.BlockSpec(shape, index_map, pipeline_mode=pl.Buffered(1))


def _order_key(score):
    bits = pltpu.bitcast(score, jnp.int32)
    key = bits ^ ((bits >> 31) & 0x7FFFFFFF)
    return jnp.where(key == -1, 0, key)


def _bisect_threshold(count_ge, shape, topk):
    def body(i, lo):
        inc = jnp.left_shift(jnp.int32(1), 31 - i)
        cand = lo + inc
        return jnp.where(count_ge(cand) >= topk, cand, lo)

    return lax.fori_loop(0, 32, body, jnp.full(shape, INT_MIN, jnp.int32))


def _fold_rows(x, op, rows=SUBLANES):
    while x.shape[0] > rows:
        half = x.shape[0] // 2
        x = op(x[:half], x[half:])
    return x


def _rms_scale(x, g):
    ms = jnp.mean(x * x, axis=-1, keepdims=True)
    return x * lax.rsqrt(ms + RMS_EPS) * g


_ROW_SECTIONS = (("k", ATTN_W), ("v", ATTN_W), ("u", POOL_W), ("ga", D_MODEL), ("gp", D_MODEL),
                 ("ki", LANES))
_COL_SECTIONS = (("q", ATTN_W), ("qi", IDX_W), ("v", ATTN_W), ("wi", SUBLANES))


def _proj_kernel(x_ref, g_ref, wa_ref, wbt_ref,
                 k_ref, v_ref, u_ref, ga_ref, gp_ref, ki_ref, k16_ref, ki16_ref,
                 qt_ref, qit_ref, vt_ref, wit_ref):
    h = _rms_scale(x_ref[...], g_ref[...]).astype(_BF)
    outs = {}
    lo = 0
    for name, width in _ROW_SECTIONS:
        outs[name] = jnp.dot(h, wa_ref[:, lo:lo + width], preferred_element_type=_F32)
        lo += width
    k_ref[...] = outs["k"]
    v_ref[...] = outs["v"]
    u_ref[...] = outs["u"]
    ga_ref[...] = outs["ga"]
    gp_ref[...] = outs["gp"]
    ki_ref[...] = outs["ki"][:, :IDX_DIM]
    k16_ref[...] = outs["k"].astype(_BF)
    ki16_ref[...] = outs["ki"][:, :IDX_DIM].astype(_BF)
    lo = 0
    for name, width in _COL_SECTIONS:
        t = lax.dot_general(wbt_ref[lo:lo + width, :], h, _NT, preferred_element_type=_F32)
        lo += width
        if name == "q":
            qt_ref[...] = t.astype(_BF)
        elif name == "qi":
            qit_ref[...] = t.astype(_BF)
        elif name == "v":
            vt_ref[0] = t.astype(_BF)
        else:
            wit_ref[...] = t


def _project(x, ln_g, w_in):
    n = x.shape[0]
    tm = ROW_TILE
    assert n % tm == 0 and tm == K_TILE
    widths = (ATTN_W, ATTN_W, ATTN_W, IDX_W, IDX_DIM, IDX_HEADS, POOL_W, D_MODEL, D_MODEL)
    names = ("q", "k", "v", "qi", "ki", "wi", "u", "ga", "gp")
    cols, lo = {}, 0
    for name, width in zip(names, widths):
        cols[name] = w_in[:, lo:lo + width]
        lo += width
    ki_pad = jnp.pad(cols["ki"], ((0, 0), (0, LANES - IDX_DIM)))
    wa = jnp.concatenate([cols["k"], cols["v"], cols["u"], cols["ga"], cols["gp"], ki_pad],
                         axis=1).astype(_BF)
    wi_pad = jnp.pad(cols["wi"], ((0, 0), (0, SUBLANES - IDX_HEADS)))
    wbt = jnp.concatenate([cols["q"] * HEAD_DIM ** -0.5, cols["qi"] * IDX_DIM ** -0.5, cols["v"],
                           wi_pad], axis=1).T.astype(_BF)
    na, nb = wa.shape[1], wbt.shape[0]
    row = lambda w: pl.BlockSpec((tm, w), lambda i: (i, 0))
    colb = lambda h: pl.BlockSpec((h, tm), lambda i: (0, i))
    out_shapes = dict(
        k=((n, ATTN_W), _F32, row(ATTN_W)), v=((n, ATTN_W), _F32, row(ATTN_W)),
        u=((n, POOL_W), _F32, row(POOL_W)), ga=((n, D_MODEL), _F32, row(D_MODEL)),
        gp=((n, D_MODEL), _F32, row(D_MODEL)), ki=((n, IDX_DIM), _F32, row(IDX_DIM)),
        k16=((n, ATTN_W), _BF, row(ATTN_W)), ki16=((n, IDX_DIM), _BF, row(IDX_DIM)),
        qT=((ATTN_W, n), _BF, colb(ATTN_W)), qiT=((IDX_W, n), _BF, colb(IDX_W)),
        vT=((n // tm, ATTN_W, tm), _BF, pl.BlockSpec((1, ATTN_W, tm), lambda i: (i, 0, 0))),
        wiT=((SUBLANES, n), _F32, colb(SUBLANES)),
    )
    keys = list(out_shapes)
    res = pl.pallas_call(
        _proj_kernel,
        grid=(n // tm,),
        in_specs=[row(D_MODEL), _resident((1, D_MODEL), lambda i: (0, 0)),
                  _resident((D_MODEL, na), lambda i: (0, 0)),
                  _resident((nb, D_MODEL), lambda i: (0, 0))],
        out_specs=[out_shapes[k][2] for k in keys],
        out_shape=[jax.ShapeDtypeStruct(out_shapes[k][0], out_shapes[k][1]) for k in keys],
        compiler_params=pltpu.CompilerParams(dimension_semantics=("arbitrary",),
                                             vmem_limit_bytes=VMEM_LIMIT),
        name="input_projection",
    )(x, ln_g.reshape(1, D_MODEL).astype(_F32), wa, wbt)
    return dict(zip(keys, res))


def _prompt_attn_kernel(qT_ref, qiT_ref, wiT_ref, k_ref, vT_ref, ki_ref, bias_ref, ltri_ref,
                        out_ref, keys_scr, mask_scr, qm_scr, m_scr, acc_scr, *, topk):
    TQ, TK = Q_TILE, K_TILE
    sub = TK // TQ
    qb = pl.program_id(1)
    n_sb = qb // sub + 1
    n_far = jnp.maximum(qb - 1, 0) // sub
    key_off = lax.broadcasted_iota(jnp.int32, (TK, TQ), 0)
    qry_off = lax.broadcasted_iota(jnp.int32, (TK, TQ), 1)

    def causal_at(sb):
        return (sb * TK + key_off) <= (qb * TQ + qry_off)

    qiT = qiT_ref[...]
    wT = wiT_ref[...]

    def score_keys(sb):
        kib = ki_ref[pl.ds(pl.multiple_of(sb * TK, TK), TK), :]
        sc = None
        for h in range(IDX_HEADS):
            d = jnp.dot(kib, qiT[h * IDX_DIM:(h + 1) * IDX_DIM, :], preferred_element_type=_F32)
            term = wT[h:h + 1, :] * jnp.maximum(d, 0.0)
            sc = term if sc is None else sc + term
        return _order_key(sc)

    def score_body(sb, carry):
        keys_scr[sb] = score_keys(sb)
        return carry

    lax.fori_loop(0, n_sb - 1, score_body, 0)
    keys_scr[n_sb - 1] = jnp.where(causal_at(n_sb - 1), score_keys(n_sb - 1), INT_MIN)

    def count(pred):
        def body(sb, acc):
            hit = jnp.where(pred(keys_scr[sb]), 1.0, 0.0)
            return acc + jnp.sum(hit.reshape(TK // COUNT_ROWS, COUNT_ROWS, TQ), axis=0)
        acc = lax.fori_loop(0, n_sb, body, jnp.zeros((COUNT_ROWS, TQ), _F32))
        return jnp.sum(_fold_rows(acc, jnp.add), axis=0, keepdims=True)

    kf = float(topk)
    thr = _bisect_threshold(lambda c: count(lambda kk: kk >= c), (1, TQ), kf)
    n_ge = count(lambda kk: kk >= thr)
    has_ties = jnp.max(jnp.where(n_ge != kf, 1.0, 0.0)) > 0.0
    n_tie_keep = kf - lax.cond(has_ties, lambda: count(lambda kk: kk > thr),
                               lambda: jnp.zeros((1, TQ), _F32))

    zeros_half = jnp.zeros((HEAD_DIM, TQ), _BF)
    for pair in range(N_HEADS // 2):
        q0 = qT_ref[(2 * pair) * HEAD_DIM:(2 * pair + 1) * HEAD_DIM, :]
        q1 = qT_ref[(2 * pair + 1) * HEAD_DIM:(2 * pair + 2) * HEAD_DIM, :]
        qm_scr[pair] = jnp.concatenate([jnp.concatenate([q0, zeros_half], axis=1),
                                        jnp.concatenate([zeros_half, q1], axis=1)], axis=0)
    m_scr[...] = jnp.full(m_scr.shape, NEG_BIG, _F32)
    acc_scr[...] = jnp.zeros(acc_scr.shape, _F32)
    ones_rows = jnp.ones((SUM_ROWS, TK), _BF)

    def attend_block(sb, tie_seen, near, ties):
        row0 = pl.multiple_of(sb * TK, TK)

        def logits(pair):
            kp = k_ref[pl.ds(row0, TK), pair * HEAD_PAIR:(pair + 1) * HEAD_PAIR]
            return jnp.dot(kp, qm_scr[pair], preferred_element_type=_F32)

        n_pairs = N_HEADS // 2
        s_next = logits(0)
        keyb = keys_scr[sb]
        if ties:
            eq = keyb == thr
            eqf = jnp.where(eq, 1.0, 0.0)
            rank = tie_seen + jnp.dot(ltri_ref[...], eqf.astype(_BF), preferred_element_type=_F32)
            keep = (keyb > thr) | (eq & (rank < n_tie_keep))
            madd = jnp.where(keep, 0.0, NEG_BIG)
            tie_seen = tie_seen + jnp.sum(eqf, axis=0, keepdims=True)
        else:
            madd = jnp.where(keyb >= thr, 0.0, NEG_BIG)
        if near:
            madd = jnp.where(causal_at(sb), madd, NEG_BIG)
        mask_scr[...] = madd
        for pair in range(n_pairs):
            s2 = s_next
            if pair + 1 < n_pairs:
                s_next = logits(pair + 1)
            vp = vT_ref[sb, pair * HEAD_PAIR:(pair + 1) * HEAD_PAIR, :]
            halves = []
            for odd in range(2):
                h = 2 * pair + odd
                s = s2[:, odd * TQ:(odd + 1) * TQ] + mask_scr[...]
                if near:
                    parts = []
                    for j in range(sub):
                        back = qb - (sb * sub + j)
                        w0 = (back == 0).astype(_F32)
                        w1 = (back == 1).astype(_F32)
                        parts.append(s[j * TQ:(j + 1) * TQ] + w0 * bias_ref[0, h] + w1 * bias_ref[1, h])
                    s = jnp.concatenate(parts, axis=0)
                halves.append(s)
            m_old = m_scr[pair]
            m_blk = jnp.concatenate(
                [jnp.max(_fold_rows(s, jnp.maximum), axis=0, keepdims=True) for s in halves], axis=1)
            m_new = jnp.maximum(m_old, m_blk)
            p2 = jnp.concatenate([jnp.exp(s - m_new[:, odd * TQ:(odd + 1) * TQ]).astype(_BF)
                                  for odd, s in enumerate(halves)], axis=1)
            alpha = jnp.exp(m_old - m_new)
            va = jnp.concatenate([vp, ones_rows], axis=0)
            acc_scr[pair] = alpha * acc_scr[pair] + jnp.dot(va, p2, preferred_element_type=_F32)
            m_scr[pair] = m_new
        return tie_seen

    def attend_all(ties):
        def run():
            seen = lax.fori_loop(0, n_far, lambda sb, c: attend_block(sb, c, False, ties),
                                 jnp.zeros((1, TQ), _F32))
            lax.fori_loop(n_far, n_sb, lambda sb, c: attend_block(sb, c, True, ties), seen)
        return run

    lax.cond(has_ties, attend_all(True), attend_all(False))

    for pair in range(N_HEADS // 2):
        a = acc_scr[pair]
        res = jnp.concatenate(
            [a[:HEAD_DIM, :TQ] / a[HEAD_PAIR:HEAD_PAIR + 1, :TQ],
             a[HEAD_DIM:HEAD_PAIR, TQ:] / a[HEAD_PAIR:HEAD_PAIR + 1, TQ:]], axis=0)
        out_ref[:, pair * HEAD_PAIR:(pair + 1) * HEAD_PAIR] = res.T.astype(out_ref.dtype)


def _rel_bias_by_distance(rel_bias, n):
    dist = jnp.arange(n, dtype=jnp.int32)
    nf = jnp.maximum(dist, 1).astype(_F32)
    large = REL_MAX_EXACT + (jnp.log(nf / REL_MAX_EXACT) / math.log(REL_MAX_DIST / REL_MAX_EXACT)
                             * (REL_BUCKETS - REL_MAX_EXACT)).astype(jnp.int32)
    large = jnp.minimum(large, REL_BUCKETS - 1)
    bucket = jnp.where(dist < REL_MAX_EXACT, dist, large)
    return (rel_bias[bucket] - rel_bias[REL_BUCKETS - 1][None, :]).astype(_F32)


def _prompt_attention(proj, rel_bias, nb, s):
    TQ, TK = Q_TILE, K_TILE
    assert s % TK == 0
    nq, nsb = s // TQ, s // TK
    topk = min(TOPK_MAX, s // 4)
    bd = _rel_bias_by_distance(rel_bias, 2 * TQ)
    key_off = jnp.arange(TQ)[:, None]
    qry_off = jnp.arange(TQ)[None, :]
    tiles = [jnp.transpose(bd[jnp.clip(back * TQ + qry_off - key_off, 0, 2 * TQ - 1)], (2, 0, 1))
             for back in range(2)]
    bias_tiles = jnp.stack(tiles)
    ltri = (jnp.arange(TK)[:, None] > jnp.arange(TK)[None, :]).astype(_BF)

    kern = functools.partial(_prompt_attn_kernel, topk=topk)
    return pl.pallas_call(
        kern,
        grid=(nb, nq),
        in_specs=[
            pl.BlockSpec((ATTN_W, TQ), lambda i, j: (0, i * nq + j)),
            pl.BlockSpec((IDX_W, TQ), lambda i, j: (0, i * nq + j)),
            pl.BlockSpec((SUBLANES, TQ), lambda i, j: (0, i * nq + j)),
            _resident((s, ATTN_W), lambda i, j: (i, 0)),
            _resident((nsb, ATTN_W, TK), lambda i, j: (i, 0, 0)),
            _resident((s, IDX_DIM), lambda i, j: (i, 0)),
            _resident((2, N_HEADS, TQ, TQ), lambda i, j: (0, 0, 0, 0)),
            _resident((TK, TK), lambda i, j: (0, 0)),
        ],
        out_specs=pl.BlockSpec((TQ, ATTN_W), lambda i, j: (i * nq + j, 0)),
        out_shape=jax.ShapeDtypeStruct((nb * s, ATTN_W), _BF),
        scratch_shapes=[
            pltpu.VMEM((nsb, TK, TQ), jnp.int32),
            pltpu.VMEM((TK, TQ), _F32),
            pltpu.VMEM((N_HEADS // 2, HEAD_PAIR, 2 * TQ), _BF),
            pltpu.VMEM((N_HEADS // 2, 1, 2 * TQ), _F32),
            pltpu.VMEM((N_HEADS // 2, HEAD_PAIR + SUM_ROWS, 2 * TQ), _F32),
        ],
        compiler_params=pltpu.CompilerParams(
            dimension_semantics=("arbitrary", "arbitrary"), vmem_limit_bytes=VMEM_LIMIT),
        name="prompt_sparse_attention",
    )(proj["qT"], proj["qiT"], proj["wiT"], proj["k16"], proj["vT"], proj["ki16"], bias_tiles, ltri)


def _sample_attn_kernel(pt_ref, qis_ref, wis_ref, kinew_ref, qbd_ref, knew_ref, vnew_ref,
                        *rest, n_pages, t_new, topk):
    del pt_ref
    G = PAGES_PER_STEP
    kidx_refs, k_refs, v_refs = rest[:G], rest[G:2 * G], rest[2 * G:3 * G]
    (biasl_ref, biasn_ref, utri_ref, out_ref, keys_scr, thr_scr, keep_scr,
     tie_scr, flag_scr, m_scr, l_scr, acc_scr) = rest[3 * G:]
    NP = n_pages
    n_steps = NP // G
    phase = pl.program_id(1)
    p = pl.program_id(2)

    def score_keys(kit):
        d = jnp.dot(qis_ref[0], kit, preferred_element_type=_F32)
        w = wis_ref[0]
        sc = None
        for h in range(IDX_HEADS):
            term = w[:, h:h + 1] * jnp.maximum(d[h * t_new:(h + 1) * t_new], 0.0)
            sc = term if sc is None else sc + term
        return _order_key(sc)

    @pl.when(phase == 0)
    def _():
        for j in range(G):
            keys_scr[p * G + j] = score_keys(kidx_refs[j][0].astype(_BF))

    @pl.when((phase == 0) & (p == n_steps - 1))
    def _():
        lane = lax.broadcasted_iota(jnp.int32, (t_new, PAGE_SIZE), 1)
        qrow = lax.broadcasted_iota(jnp.int32, (t_new, PAGE_SIZE), 0)
        keys_scr[NP] = jnp.where(lane <= qrow, score_keys(kinew_ref[0]), INT_MIN)

        def count(pred):
            def body(j, acc):
                return acc + jnp.where(pred(keys_scr[j]), 1.0, 0.0)
            acc = lax.fori_loop(0, NP + 1, body, jnp.zeros((t_new, PAGE_SIZE), _F32))
            return jnp.sum(acc, axis=1, keepdims=True)

        kf = float(topk)
        thr = _bisect_threshold(lambda c: count(lambda kk: kk >= c), (t_new, 1), kf)
        n_ge = count(lambda kk: kk >= thr)
        n_gt = count(lambda kk: kk > thr)
        thr_scr[...] = thr
        keep_scr[...] = kf - n_gt
        flag_scr[0] = (jnp.max(jnp.where(n_ge != kf, 1.0, 0.0)) > 0.0).astype(jnp.int32)
        tie_scr[...] = jnp.zeros(tie_scr.shape, _F32)
        m_scr[...] = jnp.full(m_scr.shape, NEG_BIG, _F32)
        l_scr[...] = jnp.zeros(l_scr.shape, _F32)
        acc_scr[...] = jnp.zeros(acc_scr.shape, _F32)

    def attend(keyb, kmat, vmat, bias):
        thr = thr_scr[...]

        def tie_sel():
            eq = keyb == thr
            eqf = jnp.where(eq, 1.0, 0.0)
            rank = tie_scr[...] + jnp.dot(eqf.astype(_BF), utri_ref[...],
                                          preferred_element_type=_F32)
            keep = (keyb > thr) | (eq & (rank < keep_scr[...]))
            tie_scr[...] = tie_scr[...] + jnp.sum(eqf, axis=1, keepdims=True)
            return jnp.where(keep, 1.0, 0.0)

        sel8 = lax.cond(flag_scr[0] != 0, tie_sel, lambda: jnp.where(keyb >= thr, 1.0, 0.0))
        madd = jnp.concatenate([jnp.where(sel8 > 0.5, 0.0, NEG_BIG)] * N_HEADS, axis=0)
        s = jnp.dot(qbd_ref[0], kmat, preferred_element_type=_F32) + madd
        if bias is not None:
            s = s + bias
        m_old = m_scr[...]
        m_new = jnp.maximum(m_old, jnp.max(s, axis=1, keepdims=True))
        pr = jnp.exp(s - m_new)
        alpha = jnp.exp(m_old - m_new)
        l_scr[...] = alpha * l_scr[...] + jnp.sum(pr, axis=1, keepdims=True)
        acc_scr[...] = alpha * acc_scr[...] + lax.dot_general(
            pr.astype(_BF), vmat, _NT, preferred_element_type=_F32)
        m_scr[...] = m_new

    @pl.when(phase == 1)
    def _():
        for j in range(G):
            bias = None
            if j == G - 1:
                bias = (p == n_steps - 1).astype(_F32) * biasl_ref[...]
            attend(keys_scr[p * G + j], k_refs[j][0].astype(_BF), v_refs[j][0].astype(_BF), bias)

    @pl.when((phase == 1) & (p == n_steps - 1))
    def _():
        attend(keys_scr[NP], knew_ref[0], vnew_ref[0], biasn_ref[...])
        o = acc_scr[...] / l_scr[...]
        head_of_lane = lax.broadcasted_iota(jnp.int32, (t_new, ATTN_W), 1) // HEAD_DIM
        res = jnp.zeros((t_new, ATTN_W), _F32)
        for h in range(N_HEADS):
            res = res + jnp.where(head_of_lane == h, o[h * t_new:(h + 1) * t_new], 0.0)
        out_ref[0] = res.astype(out_ref.dtype)


def _sample_attention(q, k_new, v_new, qi, ki_new, wi, cache_k, cache_v, cache_kidx, page_table,
                      rel_bias):
    nb, t_new = q.shape[:2]
    n_pages = page_table.shape[1]
    n_phys = cache_k.shape[0]
    past = n_pages * PAGE_SIZE
    topk = min(TOPK_MAX, (past + t_new) // 4)
    rows = N_HEADS * t_new
    qis = jnp.swapaxes(qi.reshape(nb, t_new, IDX_HEADS, IDX_DIM), 1, 2)
    qis = qis.reshape(nb, IDX_HEADS * t_new, IDX_DIM)
    G = PAGES_PER_STEP
    assert n_pages % G == 0
    n_steps = n_pages // G
    q4 = jnp.swapaxes(q.reshape(nb, t_new, N_HEADS, HEAD_DIM), 1, 2)
    eye = jnp.eye(N_HEADS, dtype=_BF)
    qbd = (q4[:, :, :, None, :] * eye[None, :, None, :, None]).reshape(nb, rows, ATTN_W)

    def keys_minor(a):
        return jnp.pad(jnp.swapaxes(a, 1, 2), ((0, 0), (0, 0), (0, PAGE_SIZE - t_new)))

    bd = _rel_bias_by_distance(rel_bias, 2 * PAGE_SIZE)
    qo = jnp.arange(t_new)[:, None]
    co = jnp.arange(PAGE_SIZE)[None, :]
    d_last = jnp.clip(PAGE_SIZE + qo - co, 0, 2 * PAGE_SIZE - 1)
    d_new = jnp.clip(qo - co, 0, 2 * PAGE_SIZE - 1)
    bias_last = jnp.transpose(bd[d_last], (2, 0, 1)).reshape(rows, PAGE_SIZE)
    bias_new = jnp.transpose(bd[d_new], (2, 0, 1)).reshape(rows, PAGE_SIZE)
    utri = (jnp.arange(PAGE_SIZE)[:, None] < jnp.arange(PAGE_SIZE)[None, :]).astype(_BF)

    kern = functools.partial(_sample_attn_kernel, n_pages=n_pages, t_new=t_new, topk=topk)
    seq = lambda i, ph, p, pt: (i, 0, 0)
    const2 = lambda i, ph, p, pt: (0, 0)

    def kidx_page(j):
        return lambda i, ph, p, pt: (
            jnp.where(ph == 0, pt[i * n_pages + p * G + j], pt[i * n_pages + n_pages - G + j]), 0, 0)

    def kv_page(j):
        return lambda i, ph, p, pt: (
            jnp.where(ph == 1, pt[i * n_pages + p * G + j], pt[i * n_pages + j]), 0, 0)

    grid_spec = pltpu.PrefetchScalarGridSpec(
        num_scalar_prefetch=1,
        grid=(nb, 2, n_steps),
        in_specs=[
            pl.BlockSpec((1, IDX_HEADS * t_new, IDX_DIM), seq),
            pl.BlockSpec((1, t_new, IDX_HEADS), seq),
            pl.BlockSpec((1, IDX_DIM, PAGE_SIZE), seq),
            pl.BlockSpec((1, rows, ATTN_W), seq),
            pl.BlockSpec((1, ATTN_W, PAGE_SIZE), seq),
            pl.BlockSpec((1, ATTN_W, PAGE_SIZE), seq),
        ] + [pl.BlockSpec((1, IDX_DIM, PAGE_SIZE), kidx_page(j)) for j in range(G)]
          + [pl.BlockSpec((1, ATTN_W, PAGE_SIZE), kv_page(j)) for j in range(G)]
          + [pl.BlockSpec((1, ATTN_W, PAGE_SIZE), kv_page(j)) for j in range(G)]
          + [
            pl.BlockSpec((rows, PAGE_SIZE), const2),
            pl.BlockSpec((rows, PAGE_SIZE), const2),
            pl.BlockSpec((PAGE_SIZE, PAGE_SIZE), const2),
        ],
        out_specs=pl.BlockSpec((1, t_new, ATTN_W), seq),
        scratch_shapes=[
            pltpu.VMEM((n_pages + 1, t_new, PAGE_SIZE), jnp.int32),
            pltpu.VMEM((t_new, 1), jnp.int32),
            pltpu.VMEM((t_new, 1), _F32),
            pltpu.VMEM((t_new, 1), _F32),
            pltpu.SMEM((1,), jnp.int32),
            pltpu.VMEM((rows, 1), _F32),
            pltpu.VMEM((rows, 1), _F32),
            pltpu.VMEM((rows, ATTN_W), _F32),
        ],
    )
    kidx_pages = jnp.swapaxes(cache_kidx, 1, 2)
    k_pages = jnp.transpose(cache_k, (0, 2, 3, 1)).reshape(n_phys, ATTN_W, PAGE_SIZE)
    v_pages = jnp.transpose(cache_v, (0, 2, 3, 1)).reshape(n_phys, ATTN_W, PAGE_SIZE)
    return pl.pallas_call(
        kern,
        grid_spec=grid_spec,
        out_shape=jax.ShapeDtypeStruct((nb, t_new, ATTN_W), _BF),
        compiler_params=pltpu.CompilerParams(
            dimension_semantics=("arbitrary", "arbitrary", "arbitrary"),
            vmem_limit_bytes=VMEM_LIMIT),
        name="sample_sparse_attention",
    )(page_table.reshape(-1).astype(jnp.int32), qis, wi.astype(_F32), keys_minor(ki_new), qbd,
      keys_minor(k_new), keys_minor(v_new),
      *([kidx_pages] * G), *([k_pages] * G), *([v_pages] * G), bias_last, bias_new, utri)


def _pool_diff_kernel(u_ref, halo_ref, d_ref, *, truncate_start):
    tm = u_ref.shape[0]
    i = pl.program_id(1)
    main = u_ref[...]
    halo = jnp.where(i == 0, 0.0, halo_ref[...])
    ext = jnp.concatenate([halo, main], axis=0)
    pos = i * tm + lax.broadcasted_iota(jnp.int32, (tm, POOL_GC), 0)
    for g, w in enumerate(POOL_WINDOWS):
        sl = slice(g * POOL_GC, (g + 1) * POOL_GC)
        acc = ext[:, sl]
        shift = 1
        while shift < w:
            acc = acc + pltpu.roll(acc, shift, 0)
            shift *= 2
        wsum = acc[POOL_HALO:]
        cnt = jnp.minimum(pos + 1, w).astype(_F32) if truncate_start else float(w)
        d_ref[:, sl] = (wsum / cnt - main[:, sl]).astype(d_ref.dtype)


def _pool_diff(u, n_seq, seq_rows, truncate_start):
    tm = min(ROW_TILE, seq_rows)
    assert seq_rows % tm == 0 and tm % POOL_HALO == 0
    nt = seq_rows // tm
    hb = tm // POOL_HALO
    kern = functools.partial(_pool_diff_kernel, truncate_start=truncate_start)
    return pl.pallas_call(
        kern,
        grid=(n_seq, nt),
        in_specs=[pl.BlockSpec((tm, POOL_W), lambda b, i: (b * nt + i, 0)),
                  pl.BlockSpec((POOL_HALO, POOL_W),
                               lambda b, i: (jnp.maximum((b * nt + i) * hb - 1, 0), 0))],
        out_specs=pl.BlockSpec((tm, POOL_W), lambda b, i: (b * nt + i, 0)),
        out_shape=jax.ShapeDtypeStruct(u.shape, _BF),
        compiler_params=pltpu.CompilerParams(dimension_semantics=("arbitrary", "arbitrary")),
        name="pool_window_diff",
    )(u, u)


def _merge_route_kernel(x_ref, attn_ref, d_ref, ga_ref, gp_ref, wmap_ref, pscale_ref, wba_ref,
                        wbp_ref, wout_ref, g2_ref, wr_ref, br_ref, ltri_ref,
                        x1_ref, xn_ref, eid_ref, gate_ref, rank_ref, cnt_ref):
    tm = x_ref.shape[0]
    d = d_ref[...]
    pooled = [jnp.dot(d[:, g * POOL_GC:(g + 1) * POOL_GC], wmap_ref[g], preferred_element_type=_F32)
              for g in range(POOL_GROUPS)]
    pool = (jnp.concatenate(pooled, axis=1) * pscale_ref[...]).astype(_BF)
    a = jnp.dot(attn_ref[...], wba_ref[...], preferred_element_type=_F32)
    pp = jnp.dot(pool, wbp_ref[...], preferred_element_type=_F32)
    m = jax.nn.sigmoid(ga_ref[...]) * a + jax.nn.sigmoid(gp_ref[...]) * pp
    x1 = x_ref[...] + jnp.dot(m.astype(_BF), wout_ref[...], preferred_element_type=_F32)
    x1_ref[...] = x1
    xn = _rms_scale(x1, g2_ref[...]).astype(_BF)
    xn_ref[...] = xn

    logit = jnp.dot(xn, wr_ref[...], preferred_element_type=_F32) + br_ref[...]
    lane = lax.broadcasted_iota(jnp.int32, (tm, LANES), 1).astype(_F32)
    far = float(LANES)

    def first_lane_of(hit):
        return jnp.min(jnp.where(hit, lane, far), axis=1, keepdims=True)

    lc = jnp.where(lane < N_GROUPS, logit, NEG_BIG)
    mc = jnp.max(lc, axis=1, keepdims=True)
    p_grp = 1.0 / jnp.sum(jnp.exp(lc - mc), axis=1, keepdims=True)
    grp = first_lane_of(lc == mc)
    lo = FINE_LANE0 + EXPERTS_PER_GROUP * grp
    in_grp = (lane >= lo) & (lane < lo + EXPERTS_PER_GROUP)
    lf = jnp.where(in_grp, logit, NEG_BIG)
    ef = jnp.exp(lf - jnp.max(lf, axis=1, keepdims=True))
    pf = jnp.where(in_grp, ef / jnp.sum(ef, axis=1, keepdims=True), -1.0)
    p1 = jnp.max(pf, axis=1, keepdims=True)
    l1 = first_lane_of(pf == p1)
    pf2 = jnp.where(lane == l1, -1.0, pf)
    p2 = jnp.max(pf2, axis=1, keepdims=True)
    l2 = first_lane_of(pf2 == p2)
    e1 = l1 - FINE_LANE0
    e2 = l2 - FINE_LANE0
    g1 = p_grp * p1 / (p1 + p2)
    g2 = p_grp * p2 / (p1 + p2)
    eid_ref[...] = jnp.where(lane == 0, e1, jnp.where(lane == 1, e2, 0.0)).astype(jnp.int32)
    gate_ref[...] = jnp.where(lane == 0, g1, jnp.where(lane == 1, g2, 0.0))

    onehot = jnp.where((lane == e1) | (lane == e2), 1.0, 0.0)
    before = jnp.dot(ltri_ref[...], onehot.astype(_BF), preferred_element_type=_F32)
    r1 = jnp.sum(jnp.where(lane == e1, before, 0.0), axis=1, keepdims=True)
    r2 = jnp.sum(jnp.where(lane == e2, before, 0.0), axis=1, keepdims=True)
    rank_ref[...] = jnp.where(lane == 0, r1, jnp.where(lane == 1, r2, 0.0)).astype(jnp.int32)
    cnt_ref[0] = jnp.broadcast_to(jnp.sum(onehot, axis=0, keepdims=True), (SUBLANES, LANES))


def _merge_route(x, attn, d, ga, gp, w):
    n = x.shape[0]
    tm = ROW_TILE
    assert n % tm == 0
    nt = n // tm
    row = lambda width: pl.BlockSpec((tm, width), lambda i: (i, 0))
    full = lambda a: _resident(a.shape, lambda i: (0,) * a.ndim)
    weights = (w["wmap"], w["pscale"], w["wba"], w["wbp"], w["wout"], w["g2"], w["wr"], w["br"],
               w["ltri"])
    return pl.pallas_call(
        _merge_route_kernel,
        grid=(nt,),
        in_specs=[row(D_MODEL), row(ATTN_W), row(POOL_W), row(D_MODEL), row(D_MODEL)]
                 + [full(a) for a in weights],
        out_specs=[row(D_MODEL), row(D_MODEL), row(LANES), row(LANES), row(LANES),
                   pl.BlockSpec((1, SUBLANES, LANES), lambda i: (i, 0, 0))],
        out_shape=[jax.ShapeDtypeStruct((n, D_MODEL), _F32), jax.ShapeDtypeStruct((n, D_MODEL), _BF),
                   jax.ShapeDtypeStruct((n, LANES), jnp.int32), jax.ShapeDtypeStruct((n, LANES), _F32),
                   jax.ShapeDtypeStruct((n, LANES), jnp.int32),
                   jax.ShapeDtypeStruct((nt, SUBLANES, LANES), _F32)],
        compiler_params=pltpu.CompilerParams(dimension_semantics=("arbitrary",),
                                             vmem_limit_bytes=VMEM_LIMIT),
        name="merge_norm_route",
    )(x, attn, d, ga, gp, *weights)


def _experts_kernel(be_ref, nu_ref, x_ref, wg_ref, wu_ref, wd_ref, y_ref):
    del be_ref
    i = pl.program_id(0)

    @pl.when(i < nu_ref[0])
    def _():
        x = x_ref[...]
        hg = jnp.dot(x, wg_ref[0], preferred_element_type=_F32)
        hu = jnp.dot(x, wu_ref[0], preferred_element_type=_F32)
        hdn = (hg * jax.nn.sigmoid(hg) * hu).astype(_BF)
        y_ref[...] = jnp.dot(hdn, wd_ref[0], preferred_element_type=_F32).astype(y_ref.dtype)

    @pl.when(i >= nu_ref[0])
    def _():
        y_ref[...] = jnp.zeros(y_ref.shape, y_ref.dtype)


def _experts(x_pad, block_expert, n_used, wg, wu, wd):
    nblk = block_expert.shape[0]
    bm = MOE_ROWS
    used = lambda i, be, nu: (jnp.minimum(i, jnp.maximum(nu[0] - 1, 0)), 0)
    grid_spec = pltpu.PrefetchScalarGridSpec(
        num_scalar_prefetch=2,
        grid=(nblk,),
        in_specs=[pl.BlockSpec((bm, D_MODEL), used),
                  pl.BlockSpec((1, D_MODEL, D_EXPERT), lambda i, be, nu: (be[i], 0, 0)),
                  pl.BlockSpec((1, D_MODEL, D_EXPERT), lambda i, be, nu: (be[i], 0, 0)),
                  pl.BlockSpec((1, D_EXPERT, D_MODEL), lambda i, be, nu: (be[i], 0, 0))],
        out_specs=pl.BlockSpec((bm, D_MODEL), lambda i, be, nu: (i, 0)),
    )
    return pl.pallas_call(
        _experts_kernel,
        grid_spec=grid_spec,
        out_shape=jax.ShapeDtypeStruct((nblk * bm, D_MODEL), _BF),
        compiler_params=pltpu.CompilerParams(dimension_semantics=("arbitrary",),
                                             vmem_limit_bytes=VMEM_LIMIT),
        name="grouped_swiglu_experts",
    )(block_expert, n_used, x_pad, wg, wu, wd)


def _route_layout(eid, rank, tile_cnt):
    n = eid.shape[0]
    bm = MOE_ROWS
    cnt = tile_cnt[:, 0, :N_EXPERTS].astype(jnp.int32)
    tile_start = jnp.cumsum(cnt, axis=0) - cnt
    counts = jnp.sum(cnt, axis=0)
    padded = ((counts + bm - 1) // bm) * bm
    pad_ends = jnp.cumsum(padded)
    pad_starts = pad_ends - padded
    tile_of = jnp.arange(n, dtype=jnp.int32) // ROW_TILE
    dest = pad_starts[eid] + tile_start[tile_of[:, None], eid] + rank
    nblk = (n * TOP_K_FINE) // bm + N_EXPERTS
    block_expert = jnp.minimum(jnp.searchsorted(pad_ends, jnp.arange(nblk) * bm, side="right"),
                               N_EXPERTS - 1).astype(jnp.int32)
    n_used = (pad_ends[-1] // bm).astype(jnp.int32).reshape(1)
    return dest, block_expert, n_used, nblk


def _combine_norm_kernel(x1_ref, y1_ref, y2_ref, gate_ref, g_ref, out_ref):
    gate = gate_ref[...]
    x2 = (x1_ref[...] + gate[:, 0:1] * y1_ref[...].astype(_F32)
          + gate[:, 1:2] * y2_ref[...].astype(_F32))
    out_ref[...] = _rms_scale(x2, g_ref[...])


def _combine_norm(x1, y1, y2, gate, g):
    n = x1.shape[0]
    tm = ROW_TILE
    row = lambda width: pl.BlockSpec((tm, width), lambda i: (i, 0))
    return pl.pallas_call(
        _combine_norm_kernel,
        grid=(n // tm,),
        in_specs=[row(D_MODEL), row(D_MODEL), row(D_MODEL), row(LANES),
                  _resident((1, D_MODEL), lambda i: (0, 0))],
        out_specs=row(D_MODEL),
        out_shape=jax.ShapeDtypeStruct((n, D_MODEL), _F32),
        compiler_params=pltpu.CompilerParams(dimension_semantics=("arbitrary",)),
        name="combine_final_norm",
    )(x1, y1, y2, gate, g.reshape(1, D_MODEL).astype(_F32))


def kernel(x_prompt, x_sample, cache_k, cache_v, cache_kidx, state_pool, page_table, rel_bias,
           ln1_g, w_in, w_pool_map, pool_scale, w_br_attn, w_br_pool, w_out, ln2_g,
           w_coarse, b_coarse, w_fine, b_fine, w_gate, w_up, w_down, lnf_g):
    layer = 0
    nb, s = x_prompt.shape[:2]
    db, tn = x_sample.shape[:2]
    n_p, n_s = nb * s, db * tn
    rel_bias = rel_bias.astype(_F32)

    wr = jnp.zeros((D_MODEL, LANES), _F32)
    wr = wr.at[:, :N_GROUPS].set(w_coarse[layer])
    wr = wr.at[:, FINE_LANE0:FINE_LANE0 + N_EXPERTS].set(
        jnp.transpose(w_fine[layer], (1, 0, 2)).reshape(D_MODEL, N_EXPERTS))
    br = jnp.zeros((1, LANES), _F32)
    br = br.at[0, :N_GROUPS].set(b_coarse[layer])
    br = br.at[0, FINE_LANE0:FINE_LANE0 + N_EXPERTS].set(b_fine[layer].reshape(-1))
    mw = dict(
        wmap=w_pool_map[layer].astype(_BF), pscale=pool_scale[layer].reshape(1, POOL_W).astype(_F32),
        wba=w_br_attn[layer].astype(_BF), wbp=w_br_pool[layer].astype(_BF),
        wout=w_out[layer].astype(_BF), g2=ln2_g[layer].reshape(1, D_MODEL).astype(_F32),
        wr=wr.astype(_BF), br=br,
        ltri=(jnp.arange(ROW_TILE)[:, None] > jnp.arange(ROW_TILE)[None, :]).astype(_BF))

    pp = _project(x_prompt.reshape(n_p, D_MODEL), ln1_g[layer], w_in[layer])
    attn_p = _prompt_attention(pp, rel_bias, nb, s)
    d_p = _pool_diff(pp["u"], nb, s, True)
    x1_p, xn_p, eid_p, gate_p, rank_p, cnt_p = _merge_route(
        x_prompt.reshape(n_p, D_MODEL), attn_p, d_p, pp["ga"], pp["gp"], mw)

    ps = _project(x_sample.reshape(n_s, D_MODEL), ln1_g[layer], w_in[layer])
    q_s = ps["qT"].T.reshape(db, tn, ATTN_W)
    qi_s = ps["qiT"].T.reshape(db, tn, IDX_W)
    wi_s = ps["wiT"][:IDX_HEADS].T.reshape(db, tn, IDX_HEADS)
    attn_s = _sample_attention(q_s, ps["k16"].reshape(db, tn, ATTN_W),
                               ps["v"].astype(_BF).reshape(db, tn, ATTN_W), qi_s,
                               ps["ki16"].reshape(db, tn, IDX_DIM), wi_s,
                               cache_k[layer], cache_v[layer], cache_kidx[layer], page_table, rel_bias)
    u_s = ps["u"].reshape(db, tn, POOL_W)
    buf = jnp.concatenate([jnp.zeros((db, 1, POOL_W), _F32), state_pool[layer].astype(_F32), u_s],
                          axis=1)
    grp_rows = 1 + POOL_CTX + tn
    d_s = _pool_diff(buf.reshape(db * grp_rows, POOL_W), 1, db * grp_rows, False)
    d_s = d_s.reshape(db, grp_rows, POOL_W)[:, 1 + POOL_CTX:].reshape(n_s, POOL_W)
    x1_s, xn_s, eid_s, gate_s, rank_s, cnt_s = _merge_route(
        x_sample.reshape(n_s, D_MODEL), attn_s.reshape(n_s, ATTN_W), d_s, ps["ga"], ps["gp"], mw)

    n_all = n_p + n_s
    eid = jnp.concatenate([eid_p[:, :TOP_K_FINE], eid_s[:, :TOP_K_FINE]], axis=0)
    rank = jnp.concatenate([rank_p[:, :TOP_K_FINE], rank_s[:, :TOP_K_FINE]], axis=0)
    gate = jnp.concatenate([gate_p, gate_s], axis=0)
    xn = jnp.concatenate([xn_p, xn_s], axis=0)
    x1 = jnp.concatenate([x1_p, x1_s], axis=0)
    dest, block_expert, n_used, nblk = _route_layout(eid, rank, jnp.concatenate([cnt_p, cnt_s], axis=0))
    tok = jnp.broadcast_to(jnp.arange(n_all, dtype=jnp.int32)[:, None], dest.shape)
    slot_tok = jnp.full((nblk * MOE_ROWS,), n_all, jnp.int32).at[dest.reshape(-1)].set(tok.reshape(-1))
    x_pad = jnp.concatenate([xn, jnp.zeros((1, D_MODEL), _BF)], axis=0)[slot_tok]
    yb = _experts(x_pad, block_expert, n_used, w_gate[layer].astype(_BF), w_up[layer].astype(_BF),
                  w_down[layer].astype(_BF))
    y = _combine_norm(x1, yb[dest[:, 0]], yb[dest[:, 1]], gate, lnf_g)

    y_prompt = y[:n_p].reshape(nb, s, D_MODEL)
    y_sample = y[n_p:].reshape(db, tn, D_MODEL)
    head = lambda a, n, t: a.reshape(1, n, t, N_HEADS, HEAD_DIM)
    return (y_prompt, y_sample,
            head(pp["k"], nb, s), head(pp["v"], nb, s), pp["ki"].reshape(1, nb, s, IDX_DIM),
            pp["u"].reshape(nb, s, POOL_W)[None, :, -POOL_CTX:],
            head(ps["k"], db, tn), head(ps["v"], db, tn), ps["ki"].reshape(1, db, tn, IDX_DIM),
            buf[None, :, -POOL_CTX:])
```

```python
import functools
import math

import jax
import jax.numpy as jnp
from jax import lax
from jax.experimental import pallas as pl
from jax.experimental.pallas import tpu as pltpu

D_MODEL = 1024
N_HEADS = 8
HEAD_DIM = 64
ATTN_W = N_HEADS * HEAD_DIM
IDX_HEADS = 4
IDX_DIM = 64
IDX_W = IDX_HEADS * IDX_DIM
TOPK_MAX = 256
PAGE_SIZE = 128
REL_BUCKETS = 32
REL_MAX_EXACT = 16
REL_MAX_DIST = 128
POOL_GROUPS = 4
POOL_GC = 128
POOL_W = POOL_GROUPS * POOL_GC
POOL_WINDOWS = (2, 4, 8, 16)
POOL_CTX = 15
N_GROUPS = 4
EXPERTS_PER_GROUP = 8
N_EXPERTS = N_GROUPS * EXPERTS_PER_GROUP
TOP_K_FINE = 2
D_EXPERT = 512
RMS_EPS = 1e-6

LANES = 128
SUBLANES = 8
HEAD_PAIR = 2 * HEAD_DIM
INT_MIN = -(2 ** 31)
NEG_BIG = -1e30
VMEM_LIMIT = 56 * 1024 * 1024
ROW_TILE = 512
Q_TILE = 128
K_TILE = 512
COUNT_ROWS = 64
SUM_ROWS = 16
PAGES_PER_STEP = 8
MOE_ROWS = 256
POOL_HALO = 16
FINE_LANE0 = 8

_NT = (((1,), (1,)), ((), ()))
_BF = jnp.bfloat16
_F32 = jnp.float32


def _resident(shape, index_map):
    return pl.BlockSpec(shape, index_map, pipeline_mode=pl.Buffered(1))


def _order_key(score):
    bits = pltpu.bitcast(score, jnp.int32)
    key = bits ^ ((bits >> 31) & 0x7FFFFFFF)
    return jnp.where(key == -1, 0, key)


def _bisect_threshold(count_ge, shape, topk, n_keys):
    def body(i, carry):
        lo, n_lo = carry
        inc = jnp.left_shift(jnp.int32(1), 31 - i)
        cand = lo + inc
        n_cand = count_ge(cand)
        ok = n_cand >= topk
        return jnp.where(ok, cand, lo), jnp.where(ok, n_cand, n_lo)

    init = (jnp.full(shape, INT_MIN, jnp.int32), jnp.full(shape, n_keys, _F32))
    return lax.fori_loop(0, 32, body, init)


def _fold_rows(x, op, rows=SUBLANES):
    while x.shape[0] > rows:
        half = x.shape[0] // 2
        x = op(x[:half], x[half:])
    return x


def _rms_scale(x, g):
    ms = jnp.mean(x * x, axis=-1, keepdims=True)
    return x * lax.rsqrt(ms + RMS_EPS) * g


_ROW_SECTIONS = (("k", ATTN_W), ("v", ATTN_W), ("u", POOL_W), ("ga", D_MODEL), ("gp", D_MODEL),
                 ("ki", LANES))
_COL_SECTIONS = (("q", ATTN_W), ("qi", IDX_W), ("v", ATTN_W), ("wi", SUBLANES))


def _proj_kernel(x_ref, g_ref, wa_ref, wbt_ref,
                 k_ref, v_ref, u_ref, ga_ref, gp_ref, ki_ref, k16_ref, ki16_ref,
                 qt_ref, qit_ref, vt_ref, wit_ref):
    h = _rms_scale(x_ref[...], g_ref[...]).astype(_BF)
    outs = {}
    lo = 0
    for name, width in _ROW_SECTIONS:
        outs[name] = jnp.dot(h, wa_ref[:, lo:lo + width], preferred_element_type=_F32)
        lo += width
    k_ref[...] = outs["k"]
    v_ref[...] = outs["v"]
    u_ref[...] = outs["u"]
    ga_ref[...] = outs["ga"]
    gp_ref[...] = outs["gp"]
    ki_ref[...] = outs["ki"][:, :IDX_DIM]
    k16_ref[...] = outs["k"].astype(_BF)
    ki16_ref[...] = outs["ki"][:, :IDX_DIM].astype(_BF)
    lo = 0
    for name, width in _COL_SECTIONS:
        t = lax.dot_general(wbt_ref[lo:lo + width, :], h, _NT, preferred_element_type=_F32)
        lo += width
        if name == "q":
            qt_ref[...] = t.astype(_BF)
        elif name == "qi":
            qit_ref[...] = t.astype(_BF)
        elif name == "v":
            vt_ref[0] = t.astype(_BF)
        else:
            wit_ref[...] = t


def _project(x, ln_g, w_in):
    n = x.shape[0]
    tm = ROW_TILE
    assert n % tm == 0 and tm == K_TILE
    widths = (ATTN_W, ATTN_W, ATTN_W, IDX_W, IDX_DIM, IDX_HEADS, POOL_W, D_MODEL, D_MODEL)
    names = ("q", "k", "v", "qi", "ki", "wi", "u", "ga", "gp")
    cols, lo = {}, 0
    for name, width in zip(names, widths):
        cols[name] = w_in[:, lo:lo + width]
        lo += width
    ki_pad = jnp.pad(cols["ki"], ((0, 0), (0, LANES - IDX_DIM)))
    wa = jnp.concatenate([cols["k"], cols["v"], cols["u"], cols["ga"], cols["gp"], ki_pad],
                         axis=1).astype(_BF)
    wi_pad = jnp.pad(cols["wi"], ((0, 0), (0, SUBLANES - IDX_HEADS)))
    wbt = jnp.concatenate([cols["q"] * HEAD_DIM ** -0.5, cols["qi"] * IDX_DIM ** -0.5, cols["v"],
                           wi_pad], axis=1).T.astype(_BF)
    na, nb = wa.shape[1], wbt.shape[0]
    row = lambda w: pl.BlockSpec((tm, w), lambda i: (i, 0))
    colb = lambda h: pl.BlockSpec((h, tm), lambda i: (0, i))
    out_shapes = dict(
        k=((n, ATTN_W), _F32, row(ATTN_W)), v=((n, ATTN_W), _F32, row(ATTN_W)),
        u=((n, POOL_W), _F32, row(POOL_W)), ga=((n, D_MODEL), _F32, row(D_MODEL)),
        gp=((n, D_MODEL), _F32, row(D_MODEL)), ki=((n, IDX_DIM), _F32, row(IDX_DIM)),
        k16=((n, ATTN_W), _BF, row(ATTN_W)), ki16=((n, IDX_DIM), _BF, row(IDX_DIM)),
        qT=((ATTN_W, n), _BF, colb(ATTN_W)), qiT=((IDX_W, n), _BF, colb(IDX_W)),
        vT=((n // tm, ATTN_W, tm), _BF, pl.BlockSpec((1, ATTN_W, tm), lambda i: (i, 0, 0))),
        wiT=((SUBLANES, n), _F32, colb(SUBLANES)),
    )
    keys = list(out_shapes)
    res = pl.pallas_call(
        _proj_kernel,
        grid=(n // tm,),
        in_specs=[row(D_MODEL), _resident((1, D_MODEL), lambda i: (0, 0)),
                  _resident((D_MODEL, na), lambda i: (0, 0)),
                  _resident((nb, D_MODEL), lambda i: (0, 0))],
        out_specs=[out_shapes[k][2] for k in keys],
        out_shape=[jax.ShapeDtypeStruct(out_shapes[k][0], out_shapes[k][1]) for k in keys],
        compiler_params=pltpu.CompilerParams(dimension_semantics=("arbitrary",),
                                             vmem_limit_bytes=VMEM_LIMIT),
        name="input_projection",
    )(x, ln_g.reshape(1, D_MODEL).astype(_F32), wa, wbt)
    return dict(zip(keys, res))


def _prompt_attn_kernel(qT_ref, qiT_ref, wiT_ref, k_ref, vT_ref, ki_ref, bias_ref, ltri_ref,
                        out_ref, keys_scr, mask_scr, qm_scr, m_scr, acc_scr, *, topk):
    TQ, TK = Q_TILE, K_TILE
    sub = TK // TQ
    qb = pl.program_id(1)
    n_sb = qb // sub + 1
    n_far = jnp.maximum(qb - 1, 0) // sub
    key_off = lax.broadcasted_iota(jnp.int32, (TK, TQ), 0)
    qry_off = lax.broadcasted_iota(jnp.int32, (TK, TQ), 1)

    def causal_at(sb):
        return (sb * TK + key_off) <= (qb * TQ + qry_off)

    qi_wide = jnp.concatenate([qiT_ref[h * IDX_DIM:(h + 1) * IDX_DIM, :] for h in range(IDX_HEADS)],
                              axis=1)
    wT = wiT_ref[...]

    def score_keys(sb):
        kib = ki_ref[pl.ds(pl.multiple_of(sb * TK, TK), TK), :]
        d = jnp.dot(kib, qi_wide, preferred_element_type=_F32)
        sc = None
        for h in range(IDX_HEADS):
            term = wT[h:h + 1, :] * jnp.maximum(d[:, h * TQ:(h + 1) * TQ], 0.0)
            sc = term if sc is None else sc + term
        return _order_key(sc)

    def score_body(sb, carry):
        keys_scr[sb] = score_keys(sb)
        return carry

    lax.fori_loop(0, n_sb - 1, score_body, 0)
    keys_scr[n_sb - 1] = jnp.where(causal_at(n_sb - 1), score_keys(n_sb - 1), INT_MIN)

    def count(pred):
        def body(sb, acc):
            hit = jnp.where(pred(keys_scr[sb]), 1.0, 0.0)
            return acc + jnp.sum(hit.reshape(TK // COUNT_ROWS, COUNT_ROWS, TQ), axis=0)
        acc = lax.fori_loop(0, n_sb, body, jnp.zeros((COUNT_ROWS, TQ), _F32))
        return jnp.sum(_fold_rows(acc, jnp.add), axis=0, keepdims=True)

    kf = float(topk)
    thr, n_ge = _bisect_threshold(lambda c: count(lambda kk: kk >= c), (1, TQ), kf,
                                  (n_sb * TK).astype(_F32))
    has_ties = jnp.max(jnp.where(n_ge != kf, 1.0, 0.0)) > 0.0
    n_tie_keep = kf - lax.cond(has_ties, lambda: count(lambda kk: kk > thr),
                               lambda: jnp.zeros((1, TQ), _F32))

    zeros_half = jnp.zeros((HEAD_DIM, TQ), _BF)
    for pair in range(N_HEADS // 2):
        q0 = qT_ref[(2 * pair) * HEAD_DIM:(2 * pair + 1) * HEAD_DIM, :]
        q1 = qT_ref[(2 * pair + 1) * HEAD_DIM:(2 * pair + 2) * HEAD_DIM, :]
        qm_scr[pair] = jnp.concatenate([jnp.concatenate([q0, zeros_half], axis=1),
                                        jnp.concatenate([zeros_half, q1], axis=1)], axis=0)
    m_scr[...] = jnp.full(m_scr.shape, NEG_BIG, _F32)
    acc_scr[...] = jnp.zeros(acc_scr.shape, _F32)
    ones_rows = jnp.ones((SUM_ROWS, TK), _BF)

    def attend_block(sb, tie_seen, near, ties):
        row0 = pl.multiple_of(sb * TK, TK)

        def logits(pair):
            kp = k_ref[pl.ds(row0, TK), pair * HEAD_PAIR:(pair + 1) * HEAD_PAIR]
            return jnp.dot(kp, qm_scr[pair], preferred_element_type=_F32)

        n_pairs = N_HEADS // 2
        s_next = logits(0)
        keyb = keys_scr[sb]
        if ties:
            eq = keyb == thr
            eqf = jnp.where(eq, 1.0, 0.0)
            rank = tie_seen + jnp.dot(ltri_ref[...], eqf.astype(_BF), preferred_element_type=_F32)
            keep = (keyb > thr) | (eq & (rank < n_tie_keep))
            madd = jnp.where(keep, 0.0, NEG_BIG)
            tie_seen = tie_seen + jnp.sum(eqf, axis=0, keepdims=True)
        else:
            madd = jnp.where(keyb >= thr, 0.0, NEG_BIG)
        if near:
            madd = jnp.where(causal_at(sb), madd, NEG_BIG)
        mask_scr[...] = madd
        for pair in range(n_pairs):
            s2 = s_next
            if pair + 1 < n_pairs:
                s_next = logits(pair + 1)
            vp = vT_ref[sb, pair * HEAD_PAIR:(pair + 1) * HEAD_PAIR, :]
            halves = []
            for odd in range(2):
                h = 2 * pair + odd
                s = s2[:, odd * TQ:(odd + 1) * TQ] + mask_scr[...]
                if near:
                    parts = []
                    for j in range(sub):
                        back = qb - (sb * sub + j)
                        w0 = (back == 0).astype(_F32)
                        w1 = (back == 1).astype(_F32)
                        parts.append(s[j * TQ:(j + 1) * TQ] + w0 * bias_ref[0, h] + w1 * bias_ref[1, h])
                    s = jnp.concatenate(parts, axis=0)
                halves.append(s)
            m_old = m_scr[pair]
            m_blk = jnp.concatenate(
                [jnp.max(_fold_rows(s, jnp.maximum), axis=0, keepdims=True) for s in halves], axis=1)
            m_new = jnp.maximum(m_old, m_blk)
            p2 = jnp.concatenate([jnp.exp(s - m_new[:, odd * TQ:(odd + 1) * TQ]).astype(_BF)
                                  for odd, s in enumerate(halves)], axis=1)
            alpha = jnp.exp(m_old - m_new)
            va = jnp.concatenate([vp, ones_rows], axis=0)
            acc_scr[pair] = alpha * acc_scr[pair] + jnp.dot(va, p2, preferred_element_type=_F32)
            m_scr[pair] = m_new
        return tie_seen

    def attend_all(ties):
        def run():
            seen = lax.fori_loop(0, n_far, lambda sb, c: attend_block(sb, c, False, ties),
                                 jnp.zeros((1, TQ), _F32))
            lax.fori_loop(n_far, n_sb, lambda sb, c: attend_block(sb, c, True, ties), seen)
        return run

    lax.cond(has_ties, attend_all(True), attend_all(False))

    for pair in range(N_HEADS // 2):
        a = acc_scr[pair]
        res = jnp.concatenate(
            [a[:HEAD_DIM, :TQ] / a[HEAD_PAIR:HEAD_PAIR + 1, :TQ],
             a[HEAD_DIM:HEAD_PAIR, TQ:] / a[HEAD_PAIR:HEAD_PAIR + 1, TQ:]], axis=0)
        out_ref[:, pair * HEAD_PAIR:(pair + 1) * HEAD_PAIR] = res.T.astype(out_ref.dtype)


def _rel_bias_by_distance(rel_bias, n):
    dist = jnp.arange(n, dtype=jnp.int32)
    nf = jnp.maximum(dist, 1).astype(_F32)
    large = REL_MAX_EXACT + (jnp.log(nf / REL_MAX_EXACT) / math.log(REL_MAX_DIST / REL_MAX_EXACT)
                             * (REL_BUCKETS - REL_MAX_EXACT)).astype(jnp.int32)
    large = jnp.minimum(large, REL_BUCKETS - 1)
    bucket = jnp.where(dist < REL_MAX_EXACT, dist, large)
    return (rel_bias[bucket] - rel_bias[REL_BUCKETS - 1][None, :]).astype(_F32)


def _prompt_attention(proj, rel_bias, nb, s):
    TQ, TK = Q_TILE, K_TILE
    assert s % TK == 0
    nq, nsb = s // TQ, s // TK
    topk = min(TOPK_MAX, s // 4)
    bd = _rel_bias_by_distance(rel_bias, 2 * TQ)
    key_off = jnp.arange(TQ)[:, None]
    qry_off = jnp.arange(TQ)[None, :]
    tiles = [jnp.transpose(bd[jnp.clip(back * TQ + qry_off - key_off, 0, 2 * TQ - 1)], (2, 0, 1))
             for back in range(2)]
    bias_tiles = jnp.stack(tiles)
    ltri = (jnp.arange(TK)[:, None] > jnp.arange(TK)[None, :]).astype(_BF)

    kern = functools.partial(_prompt_attn_kernel, topk=topk)
    return pl.pallas_call(
        kern,
        grid=(nb, nq),
        in_specs=[
            pl.BlockSpec((ATTN_W, TQ), lambda i, j: (0, i * nq + j)),
            pl.BlockSpec((IDX_W, TQ), lambda i, j: (0, i * nq + j)),
            pl.BlockSpec((SUBLANES, TQ), lambda i, j: (0, i * nq + j)),
            _resident((s, ATTN_W), lambda i, j: (i, 0)),
            _resident((nsb, ATTN_W, TK), lambda i, j: (i, 0, 0)),
            _resident((s, IDX_DIM), lambda i, j: (i, 0)),
            _resident((2, N_HEADS, TQ, TQ), lambda i, j: (0, 0, 0, 0)),
            _resident((TK, TK), lambda i, j: (0, 0)),
        ],
        out_specs=pl.BlockSpec((TQ, ATTN_W), lambda i, j: (i * nq + j, 0)),
        out_shape=jax.ShapeDtypeStruct((nb * s, ATTN_W), _BF),
        scratch_shapes=[
            pltpu.VMEM((nsb, TK, TQ), jnp.int32),
            pltpu.VMEM((TK, TQ), _F32),
            pltpu.VMEM((N_HEADS // 2, HEAD_PAIR, 2 * TQ), _BF),
            pltpu.VMEM((N_HEADS // 2, 1, 2 * TQ), _F32),
            pltpu.VMEM((N_HEADS // 2, HEAD_PAIR + SUM_ROWS, 2 * TQ), _F32),
        ],
        compiler_params=pltpu.CompilerParams(
            dimension_semantics=("arbitrary", "arbitrary"), vmem_limit_bytes=VMEM_LIMIT),
        name="prompt_sparse_attention",
    )(proj["qT"], proj["qiT"], proj["wiT"], proj["k16"], proj["vT"], proj["ki16"], bias_tiles, ltri)


def _sample_attn_kernel(pt_ref, qis_ref, wis_ref, kinew_ref, qbd_ref, knew_ref, vnew_ref,
                        *rest, n_pages, t_new, topk):
    del pt_ref
    G = PAGES_PER_STEP
    kidx_refs, k_refs, v_refs = rest[:G], rest[G:2 * G], rest[2 * G:3 * G]
    (biasl_ref, biasn_ref, utri_ref, out_ref, keys_scr, thr_scr, keep_scr,
     tie_scr, flag_scr, m_scr, l_scr, acc_scr) = rest[3 * G:]
    NP = n_pages
    n_steps = NP // G
    phase = pl.program_id(1)
    p = pl.program_id(2)

    def score_keys(kit):
        d = jnp.dot(qis_ref[0], kit, preferred_element_type=_F32)
        w = wis_ref[0]
        sc = None
        for h in range(IDX_HEADS):
            term = w[:, h:h + 1] * jnp.maximum(d[h * t_new:(h + 1) * t_new], 0.0)
            sc = term if sc is None else sc + term
        return _order_key(sc)

    @pl.when(phase == 0)
    def _():
        for j in range(G):
            keys_scr[p * G + j] = score_keys(kidx_refs[j][0].astype(_BF))

    @pl.when((phase == 0) & (p == n_steps - 1))
    def _():
        lane = lax.broadcasted_iota(jnp.int32, (t_new, PAGE_SIZE), 1)
        qrow = lax.broadcasted_iota(jnp.int32, (t_new, PAGE_SIZE), 0)
        keys_scr[NP] = jnp.where(lane <= qrow, score_keys(kinew_ref[0]), INT_MIN)

        def count(pred):
            past_hit = jnp.where(pred(keys_scr[0:NP]), 1.0, 0.0).reshape(NP * t_new, PAGE_SIZE)
            hit = _fold_rows(past_hit, jnp.add, rows=t_new)
            hit = hit + jnp.where(pred(keys_scr[NP]), 1.0, 0.0)
            return jnp.sum(hit, axis=1, keepdims=True)

        kf = float(topk)
        thr, n_ge = _bisect_threshold(lambda c: count(lambda kk: kk >= c), (t_new, 1), kf,
                                      float((NP + 1) * PAGE_SIZE))
        n_gt = count(lambda kk: kk > thr)
        thr_scr[...] = thr
        keep_scr[...] = kf - n_gt
        flag_scr[0] = (jnp.max(jnp.where(n_ge != kf, 1.0, 0.0)) > 0.0).astype(jnp.int32)
        tie_scr[...] = jnp.zeros(tie_scr.shape, _F32)
        m_scr[...] = jnp.full(m_scr.shape, NEG_BIG, _F32)
        l_scr[...] = jnp.zeros(l_scr.shape, _F32)
        acc_scr[...] = jnp.zeros(acc_scr.shape, _F32)

    def attend(keybs, kmats, vmats, bias):
        thr = thr_scr[...]

        def tie_sel():
            masks = []
            for keyb in keybs:
                eq = keyb == thr
                eqf = jnp.where(eq, 1.0, 0.0)
                rank = tie_scr[...] + jnp.dot(eqf.astype(_BF), utri_ref[...],
                                              preferred_element_type=_F32)
                keep = (keyb > thr) | (eq & (rank < keep_scr[...]))
                tie_scr[...] = tie_scr[...] + jnp.sum(eqf, axis=1, keepdims=True)
                masks.append(jnp.where(keep, 0.0, NEG_BIG))
            return jnp.concatenate(masks, axis=1)

        madd8 = lax.cond(flag_scr[0] != 0, tie_sel, lambda: jnp.concatenate(
            [jnp.where(keyb >= thr, 0.0, NEG_BIG) for keyb in keybs], axis=1))
        madd = jnp.concatenate([madd8] * N_HEADS, axis=0)
        s = jnp.concatenate([jnp.dot(qbd_ref[0], km, preferred_element_type=_F32) for km in kmats],
                            axis=1) + madd
        if bias is not None:
            s = s + bias
        m_old = m_scr[...]
        m_new = jnp.maximum(m_old, jnp.max(s, axis=1, keepdims=True))
        pr = jnp.exp(s - m_new)
        alpha = jnp.exp(m_old - m_new)
        l_scr[...] = alpha * l_scr[...] + jnp.sum(pr, axis=1, keepdims=True)
        prb = pr.astype(_BF)
        pv = None
        for j, vm in enumerate(vmats):
            t = lax.dot_general(prb[:, j * PAGE_SIZE:(j + 1) * PAGE_SIZE], vm, _NT,
                                preferred_element_type=_F32)
            pv = t if pv is None else pv + t
        acc_scr[...] = alpha * acc_scr[...] + pv
        m_scr[...] = m_new

    @pl.when(phase == 1)
    def _():
        bias = (p == n_steps - 1).astype(_F32) * biasl_ref[...]
        attend([keys_scr[p * G + j] for j in range(G)],
               [k_refs[j][0].astype(_BF) for j in range(G)],
               [v_refs[j][0].astype(_BF) for j in range(G)], bias)

    @pl.when((phase == 1) & (p == n_steps - 1))
    def _():
        attend([keys_scr[NP]], [knew_ref[0]], [vnew_ref[0]], biasn_ref[...])
        o = acc_scr[...] / l_scr[...]
        head_of_lane = lax.broadcasted_iota(jnp.int32, (t_new, ATTN_W), 1) // HEAD_DIM
        res = jnp.zeros((t_new, ATTN_W), _F32)
        for h in range(N_HEADS):
            res = res + jnp.where(head_of_lane == h, o[h * t_new:(h + 1) * t_new], 0.0)
        out_ref[0] = res.astype(out_ref.dtype)


def _sample_attention(q, k_new, v_new, qi, ki_new, wi, cache_k, cache_v, cache_kidx, page_table,
                      rel_bias):
    nb, t_new = q.shape[:2]
    n_pages = page_table.shape[1]
    n_phys = cache_k.shape[0]
    past = n_pages * PAGE_SIZE
    topk = min(TOPK_MAX, (past + t_new) // 4)
    rows = N_HEADS * t_new
    qis = jnp.swapaxes(qi.reshape(nb, t_new, IDX_HEADS, IDX_DIM), 1, 2)
    qis = qis.reshape(nb, IDX_HEADS * t_new, IDX_DIM)
    G = PAGES_PER_STEP
    assert n_pages % G == 0 and n_pages & (n_pages - 1) == 0
    n_steps = n_pages // G
    q4 = jnp.swapaxes(q.reshape(nb, t_new, N_HEADS, HEAD_DIM), 1, 2)
    eye = jnp.eye(N_HEADS, dtype=_BF)
    qbd = (q4[:, :, :, None, :] * eye[None, :, None, :, None]).reshape(nb, rows, ATTN_W)

    def keys_minor(a):
        return jnp.pad(jnp.swapaxes(a, 1, 2), ((0, 0), (0, 0), (0, PAGE_SIZE - t_new)))

    bd = _rel_bias_by_distance(rel_bias, 2 * PAGE_SIZE)
    qo = jnp.arange(t_new)[:, None]
    co = jnp.arange(PAGE_SIZE)[None, :]
    d_last = jnp.clip(PAGE_SIZE + qo - co, 0, 2 * PAGE_SIZE - 1)
    d_new = jnp.clip(qo - co, 0, 2 * PAGE_SIZE - 1)
    bias_last = jnp.transpose(bd[d_last], (2, 0, 1)).reshape(rows, PAGE_SIZE)
    bias_last = jnp.pad(bias_last, ((0, 0), ((G - 1) * PAGE_SIZE, 0)))
    bias_new = jnp.transpose(bd[d_new], (2, 0, 1)).reshape(rows, PAGE_SIZE)
    utri = (jnp.arange(PAGE_SIZE)[:, None] < jnp.arange(PAGE_SIZE)[None, :]).astype(_BF)

    kern = functools.partial(_sample_attn_kernel, n_pages=n_pages, t_new=t_new, topk=topk)
    seq = lambda i, ph, p, pt: (i, 0, 0)
    const2 = lambda i, ph, p, pt: (0, 0)

    def kidx_page(j):
        return lambda i, ph, p, pt: (
            jnp.where(ph == 0, pt[i * n_pages + p * G + j], pt[i * n_pages + n_pages - G + j]), 0, 0)

    def kv_page(j):
        return lambda i, ph, p, pt: (
            jnp.where(ph == 1, pt[i * n_pages + p * G + j], pt[i * n_pages + j]), 0, 0)

    grid_spec = pltpu.PrefetchScalarGridSpec(
        num_scalar_prefetch=1,
        grid=(nb, 2, n_steps),
        in_specs=[
            pl.BlockSpec((1, IDX_HEADS * t_new, IDX_DIM), seq),
            pl.BlockSpec((1, t_new, IDX_HEADS), seq),
            pl.BlockSpec((1, IDX_DIM, PAGE_SIZE), seq),
            pl.BlockSpec((1, rows, ATTN_W), seq),
            pl.BlockSpec((1, ATTN_W, PAGE_SIZE), seq),
            pl.BlockSpec((1, ATTN_W, PAGE_SIZE), seq),
        ] + [pl.BlockSpec((1, IDX_DIM, PAGE_SIZE), kidx_page(j)) for j in range(G)]
          + [pl.BlockSpec((1, ATTN_W, PAGE_SIZE), kv_page(j)) for j in range(G)]
          + [pl.BlockSpec((1, ATTN_W, PAGE_SIZE), kv_page(j)) for j in range(G)]
          + [
            pl.BlockSpec((rows, G * PAGE_SIZE), const2),
            pl.BlockSpec((rows, PAGE_SIZE), const2),
            pl.BlockSpec((PAGE_SIZE, PAGE_SIZE), const2),
        ],
        out_specs=pl.BlockSpec((1, t_new, ATTN_W), seq),
        scratch_shapes=[
            pltpu.VMEM((n_pages + 1, t_new, PAGE_SIZE), jnp.int32),
            pltpu.VMEM((t_new, 1), jnp.int32),
            pltpu.VMEM((t_new, 1), _F32),
            pltpu.VMEM((t_new, 1), _F32),
            pltpu.SMEM((1,), jnp.int32),
            pltpu.VMEM((rows, 1), _F32),
            pltpu.VMEM((rows, 1), _F32),
            pltpu.VMEM((rows, ATTN_W), _F32),
        ],
    )
    kidx_pages = jnp.swapaxes(cache_kidx, 1, 2)
    k_pages = jnp.transpose(cache_k, (0, 2, 3, 1)).reshape(n_phys, ATTN_W, PAGE_SIZE)
    v_pages = jnp.transpose(cache_v, (0, 2, 3, 1)).reshape(n_phys, ATTN_W, PAGE_SIZE)
    return pl.pallas_call(
        kern,
        grid_spec=grid_spec,
        out_shape=jax.ShapeDtypeStruct((nb, t_new, ATTN_W), _BF),
        compiler_params=pltpu.CompilerParams(
            dimension_semantics=("arbitrary", "arbitrary", "arbitrary"),
            vmem_limit_bytes=VMEM_LIMIT),
        name="sample_sparse_attention",
    )(page_table.reshape(-1).astype(jnp.int32), qis, wi.astype(_F32), keys_minor(ki_new), qbd,
      keys_minor(k_new), keys_minor(v_new),
      *([kidx_pages] * G), *([k_pages] * G), *([v_pages] * G), bias_last, bias_new, utri)


def _pool_diff_kernel(u_ref, halo_ref, d_ref, *, truncate_start):
    tm = u_ref.shape[0]
    i = pl.program_id(1)
    main = u_ref[...]
    halo = jnp.where(i == 0, 0.0, halo_ref[...])
    ext = jnp.concatenate([halo, main], axis=0)
    pos = i * tm + lax.broadcasted_iota(jnp.int32, (tm, POOL_GC), 0)
    for g, w in enumerate(POOL_WINDOWS):
        sl = slice(g * POOL_GC, (g + 1) * POOL_GC)
        acc = ext[:, sl]
        shift = 1
        while shift < w:
            acc = acc + pltpu.roll(acc, shift, 0)
            shift *= 2
        wsum = acc[POOL_HALO:]
        cnt = jnp.minimum(pos + 1, w).astype(_F32) if truncate_start else float(w)
        d_ref[:, sl] = (wsum / cnt - main[:, sl]).astype(d_ref.dtype)


def _pool_diff(u, n_seq, seq_rows, truncate_start):
    tm = min(ROW_TILE, seq_rows)
    assert seq_rows % tm == 0 and tm % POOL_HALO == 0
    nt = seq_rows // tm
    hb = tm // POOL_HALO
    kern = functools.partial(_pool_diff_kernel, truncate_start=truncate_start)
    return pl.pallas_call(
        kern,
        grid=(n_seq, nt),
        in_specs=[pl.BlockSpec((tm, POOL_W), lambda b, i: (b * nt + i, 0)),
                  pl.BlockSpec((POOL_HALO, POOL_W),
                               lambda b, i: (jnp.maximum((b * nt + i) * hb - 1, 0), 0))],
        out_specs=pl.BlockSpec((tm, POOL_W), lambda b, i: (b * nt + i, 0)),
        out_shape=jax.ShapeDtypeStruct(u.shape, _BF),
        compiler_params=pltpu.CompilerParams(dimension_semantics=("arbitrary", "arbitrary")),
        name="pool_window_diff",
    )(u, u)


def _merge_route_kernel(x_ref, attn_ref, d_ref, ga_ref, gp_ref, wmap_ref, pscale_ref, wba_ref,
                        wbp_ref, wout_ref, g2_ref, wr_ref, br_ref, ltri_ref,
                        x1_ref, xn_ref, eid_ref, gate_ref, rank_ref, cnt_ref):
    tm = x_ref.shape[0]
    d = d_ref[...]
    pooled = [jnp.dot(d[:, g * POOL_GC:(g + 1) * POOL_GC], wmap_ref[g], preferred_element_type=_F32)
              for g in range(POOL_GROUPS)]
    pool = (jnp.concatenate(pooled, axis=1) * pscale_ref[...]).astype(_BF)
    a = jnp.dot(attn_ref[...], wba_ref[...], preferred_element_type=_F32)
    pp = jnp.dot(pool, wbp_ref[...], preferred_element_type=_F32)
    m = jax.nn.sigmoid(ga_ref[...]) * a + jax.nn.sigmoid(gp_ref[...]) * pp
    x1 = x_ref[...] + jnp.dot(m.astype(_BF), wout_ref[...], preferred_element_type=_F32)
    x1_ref[...] = x1
    xn = _rms_scale(x1, g2_ref[...]).astype(_BF)
    xn_ref[...] = xn

    logit = jnp.dot(xn, wr_ref[...], preferred_element_type=_F32) + br_ref[...]
    lane = lax.broadcasted_iota(jnp.int32, (tm, LANES), 1).astype(_F32)
    far = float(LANES)

    def first_lane_of(hit):
        return jnp.min(jnp.where(hit, lane, far), axis=1, keepdims=True)

    lc = jnp.where(lane < N_GROUPS, logit, NEG_BIG)
    mc = jnp.max(lc, axis=1, keepdims=True)
    p_grp = 1.0 / jnp.sum(jnp.exp(lc - mc), axis=1, keepdims=True)
    grp = first_lane_of(lc == mc)
    lo = FINE_LANE0 + EXPERTS_PER_GROUP * grp
    in_grp = (lane >= lo) & (lane < lo + EXPERTS_PER_GROUP)
    lf = jnp.where(in_grp, logit, NEG_BIG)
    ef = jnp.exp(lf - jnp.max(lf, axis=1, keepdims=True))
    pf = jnp.where(in_grp, ef / jnp.sum(ef, axis=1, keepdims=True), -1.0)
    p1 = jnp.max(pf, axis=1, keepdims=True)
    l1 = first_lane_of(pf == p1)
    pf2 = jnp.where(lane == l1, -1.0, pf)
    p2 = jnp.max(pf2, axis=1, keepdims=True)
    l2 = first_lane_of(pf2 == p2)
    e1 = l1 - FINE_LANE0
    e2 = l2 - FINE_LANE0
    g1 = p_grp * p1 / (p1 + p2)
    g2 = p_grp * p2 / (p1 + p2)
    eid_ref[...] = jnp.where(lane == 0, e1, jnp.where(lane == 1, e2, 0.0)).astype(jnp.int32)
    gate_ref[...] = jnp.where(lane == 0, g1, jnp.where(lane == 1, g2, 0.0))

    onehot = jnp.where((lane == e1) | (lane == e2), 1.0, 0.0)
    before = jnp.dot(ltri_ref[...], onehot.astype(_BF), preferred_element_type=_F32)
    r1 = jnp.sum(jnp.where(lane == e1, before, 0.0), axis=1, keepdims=True)
    r2 = jnp.sum(jnp.where(lane == e2, before, 0.0), axis=1, keepdims=True)
    rank_ref[...] = jnp.where(lane == 0, r1, jnp.where(lane == 1, r2, 0.0)).astype(jnp.int32)
    cnt_ref[0] = jnp.broadcast_to(jnp.sum(onehot, axis=0, keepdims=True), (SUBLANES, LANES))


def _merge_route(x, attn, d, ga, gp, w):
    n = x.shape[0]
    tm = ROW_TILE
    assert n % tm == 0
    nt = n // tm
    row = lambda width: pl.BlockSpec((tm, width), lambda i: (i, 0))
    full = lambda a: _resident(a.shape, lambda i: (0,) * a.ndim)
    weights = (w["wmap"], w["pscale"], w["wba"], w["wbp"], w["wout"], w["g2"], w["wr"], w["br"],
               w["ltri"])
    return pl.pallas_call(
        _merge_route_kernel,
        grid=(nt,),
        in_specs=[row(D_MODEL), row(ATTN_W), row(POOL_W), row(D_MODEL), row(D_MODEL)]
                 + [full(a) for a in weights],
        out_specs=[row(D_MODEL), row(D_MODEL), row(LANES), row(LANES), row(LANES),
                   pl.BlockSpec((1, SUBLANES, LANES), lambda i: (i, 0, 0))],
        out_shape=[jax.ShapeDtypeStruct((n, D_MODEL), _F32), jax.ShapeDtypeStruct((n, D_MODEL), _BF),
                   jax.ShapeDtypeStruct((n, LANES), jnp.int32), jax.ShapeDtypeStruct((n, LANES), _F32),
                   jax.ShapeDtypeStruct((n, LANES), jnp.int32),
                   jax.ShapeDtypeStruct((nt, SUBLANES, LANES), _F32)],
        compiler_params=pltpu.CompilerParams(dimension_semantics=("arbitrary",),
                                             vmem_limit_bytes=VMEM_LIMIT),
        name="merge_norm_route",
    )(x, attn, d, ga, gp, *weights)


def _experts_kernel(be_ref, nu_ref, x_ref, wg_ref, wu_ref, wd_ref, y_ref):
    del be_ref
    i = pl.program_id(0)

    @pl.when(i < nu_ref[0])
    def _():
        x = x_ref[...]
        hg = jnp.dot(x, wg_ref[0], preferred_element_type=_F32)
        hu = jnp.dot(x, wu_ref[0], preferred_element_type=_F32)
        hdn = (hg * jax.nn.sigmoid(hg) * hu).astype(_BF)
        y_ref[...] = jnp.dot(hdn, wd_ref[0], preferred_element_type=_F32).astype(y_ref.dtype)

    @pl.when(i >= nu_ref[0])
    def _():
        y_ref[...] = jnp.zeros(y_ref.shape, y_ref.dtype)


def _experts(x_pad, block_expert, n_used, wg, wu, wd):
    nblk = block_expert.shape[0]
    bm = MOE_ROWS
    used = lambda i, be, nu: (jnp.minimum(i, jnp.maximum(nu[0] - 1, 0)), 0)
    grid_spec = pltpu.PrefetchScalarGridSpec(
        num_scalar_prefetch=2,
        grid=(nblk,),
        in_specs=[pl.BlockSpec((bm, D_MODEL), used),
                  pl.BlockSpec((1, D_MODEL, D_EXPERT), lambda i, be, nu: (be[i], 0, 0)),
                  pl.BlockSpec((1, D_MODEL, D_EXPERT), lambda i, be, nu: (be[i], 0, 0)),
                  pl.BlockSpec((1, D_EXPERT, D_MODEL), lambda i, be, nu: (be[i], 0, 0))],
        out_specs=pl.BlockSpec((bm, D_MODEL), lambda i, be, nu: (i, 0)),
    )
    return pl.pallas_call(
        _experts_kernel,
        grid_spec=grid_spec,
        out_shape=jax.ShapeDtypeStruct((nblk * bm, D_MODEL), _BF),
        compiler_params=pltpu.CompilerParams(dimension_semantics=("arbitrary",),
                                             vmem_limit_bytes=VMEM_LIMIT),
        name="grouped_swiglu_experts",
    )(block_expert, n_used, x_pad, wg, wu, wd)


def _route_layout(eid, rank, tile_cnt):
    n = eid.shape[0]
    bm = MOE_ROWS
    cnt = tile_cnt[:, 0, :N_EXPERTS].astype(jnp.int32)
    tile_start = jnp.cumsum(cnt, axis=0) - cnt
    counts = jnp.sum(cnt, axis=0)
    padded = ((counts + bm - 1) // bm) * bm
    pad_ends = jnp.cumsum(padded)
    pad_starts = pad_ends - padded
    tile_of = jnp.arange(n, dtype=jnp.int32) // ROW_TILE
    dest = pad_starts[eid] + tile_start[tile_of[:, None], eid] + rank
    nblk = (n * TOP_K_FINE) // bm + N_EXPERTS
    block_expert = jnp.minimum(jnp.searchsorted(pad_ends, jnp.arange(nblk) * bm, side="right"),
                               N_EXPERTS - 1).astype(jnp.int32)
    n_used = (pad_ends[-1] // bm).astype(jnp.int32).reshape(1)
    return dest, block_expert, n_used, nblk


def _combine_norm_kernel(x1_ref, y1_ref, y2_ref, gate_ref, g_ref, out_ref):
    gate = gate_ref[...]
    x2 = (x1_ref[...] + gate[:, 0:1] * y1_ref[...].astype(_F32)
          + gate[:, 1:2] * y2_ref[...].astype(_F32))
    out_ref[...] = _rms_scale(x2, g_ref[...])


def _combine_norm(x1, y1, y2, gate, g):
    n = x1.shape[0]
    tm = ROW_TILE
    row = lambda width: pl.BlockSpec((tm, width), lambda i: (i, 0))
    return pl.pallas_call(
        _combine_norm_kernel,
        grid=(n // tm,),
        in_specs=[row(D_MODEL), row(D_MODEL), row(D_MODEL), row(LANES),
                  _resident((1, D_MODEL), lambda i: (0, 0))],
        out_specs=row(D_MODEL),
        out_shape=jax.ShapeDtypeStruct((n, D_MODEL), _F32),
        compiler_params=pltpu.CompilerParams(dimension_semantics=("arbitrary",)),
        name="combine_final_norm",
    )(x1, y1, y2, gate, g.reshape(1, D_MODEL).astype(_F32))


def kernel(x_prompt, x_sample, cache_k, cache_v, cache_kidx, state_pool, page_table, rel_bias,
           ln1_g, w_in, w_pool_map, pool_scale, w_br_attn, w_br_pool, w_out, ln2_g,
           w_coarse, b_coarse, w_fine, b_fine, w_gate, w_up, w_down, lnf_g):
    layer = 0
    nb, s = x_prompt.shape[:2]
    db, tn = x_sample.shape[:2]
    n_p, n_s = nb * s, db * tn
    rel_bias = rel_bias.astype(_F32)

    wr = jnp.zeros((D_MODEL, LANES), _F32)
    wr = wr.at[:, :N_GROUPS].set(w_coarse[layer])
    wr = wr.at[:, FINE_LANE0:FINE_LANE0 + N_EXPERTS].set(
        jnp.transpose(w_fine[layer], (1, 0, 2)).reshape(D_MODEL, N_EXPERTS))
    br = jnp.zeros((1, LANES), _F32)
    br = br.at[0, :N_GROUPS].set(b_coarse[layer])
    br = br.at[0, FINE_LANE0:FINE_LANE0 + N_EXPERTS].set(b_fine[layer].reshape(-1))
    mw = dict(
        wmap=w_pool_map[layer].astype(_BF), pscale=pool_scale[layer].reshape(1, POOL_W).astype(_F32),
        wba=w_br_attn[layer].astype(_BF), wbp=w_br_pool[layer].astype(_BF),
        wout=w_out[layer].astype(_BF), g2=ln2_g[layer].reshape(1, D_MODEL).astype(_F32),
        wr=wr.astype(_BF), br=br,
        ltri=(jnp.arange(ROW_TILE)[:, None] > jnp.arange(ROW_TILE)[None, :]).astype(_BF))

    pp = _project(x_prompt.reshape(n_p, D_MODEL), ln1_g[layer], w_in[layer])
    attn_p = _prompt_attention(pp, rel_bias, nb, s)
    d_p = _pool_diff(pp["u"], nb, s, True)
    x1_p, xn_p, eid_p, gate_p, rank_p, cnt_p = _merge_route(
        x_prompt.reshape(n_p, D_MODEL), attn_p, d_p, pp["ga"], pp["gp"], mw)

    ps = _project(x_sample.reshape(n_s, D_MODEL), ln1_g[layer], w_in[layer])
    q_s = ps["qT"].T.reshape(db, tn, ATTN_W)
    qi_s = ps["qiT"].T.reshape(db, tn, IDX_W)
    wi_s = ps["wiT"][:IDX_HEADS].T.reshape(db, tn, IDX_HEADS)
    attn_s = _sample_attention(q_s, ps["k16"].reshape(db, tn, ATTN_W),
                               ps["v"].astype(_BF).reshape(db, tn, ATTN_W), qi_s,
                               ps["ki16"].reshape(db, tn, IDX_DIM), wi_s,
                               cache_k[layer], cache_v[layer], cache_kidx[layer], page_table, rel_bias)
    u_s = ps["u"].reshape(db, tn, POOL_W)
    buf = jnp.concatenate([jnp.zeros((db, 1, POOL_W), _F32), state_pool[layer].astype(_F32), u_s],
                          axis=1)
    grp_rows = 1 + POOL_CTX + tn
    d_s = _pool_diff(buf.reshape(db * grp_rows, POOL_W), 1, db * grp_rows, False)
    d_s = d_s.reshape(db, grp_rows, POOL_W)[:, 1 + POOL_CTX:].reshape(n_s, POOL_W)
    x1_s, xn_s, eid_s, gate_s, rank_s, cnt_s = _merge_route(
        x_sample.reshape(n_s, D_MODEL), attn_s.reshape(n_s, ATTN_W), d_s, ps["ga"], ps["gp"], mw)

    n_all = n_p + n_s
    eid = jnp.concatenate([eid_p[:, :TOP_K_FINE], eid_s[:, :TOP_K_FINE]], axis=0)
    rank = jnp.concatenate([rank_p[:, :TOP_K_FINE], rank_s[:, :TOP_K_FINE]], axis=0)
    gate = jnp.concatenate([gate_p, gate_s], axis=0)
    xn = jnp.concatenate([xn_p, xn_s], axis=0)
    x1 = jnp.concatenate([x1_p, x1_s], axis=0)
    dest, block_expert, n_used, nblk = _route_layout(eid, rank, jnp.concatenate([cnt_p, cnt_s], axis=0))
    tok = jnp.broadcast_to(jnp.arange(n_all, dtype=jnp.int32)[:, None], dest.shape)
    slot_tok = jnp.full((nblk * MOE_ROWS,), n_all, jnp.int32).at[dest.reshape(-1)].set(tok.reshape(-1))
    x_pad = jnp.concatenate([xn, jnp.zeros((1, D_MODEL), _BF)], axis=0)[slot_tok]
    yb = _experts(x_pad, block_expert, n_used, w_gate[layer].astype(_BF), w_up[layer].astype(_BF),
                  w_down[layer].astype(_BF))
    y = _combine_norm(x1, yb[dest[:, 0]], yb[dest[:, 1]], gate, lnf_g)

    y_prompt = y[:n_p].reshape(nb, s, D_MODEL)
    y_sample = y[n_p:].reshape(db, tn, D_MODEL)
    head = lambda a, n, t: a.reshape(1, n, t, N_HEADS, HEAD_DIM)
    return (y_prompt, y_sample,
            head(pp["k"], nb, s), head(pp["v"], nb, s), pp["ki"].reshape(1, nb, s, IDX_DIM),
            pp["u"].reshape(nb, s, POOL_W)[None, :, -POOL_CTX:],
            head(ps["k"], db, tn), head(ps["v"], db, tn), ps["ki"].reshape(1, db, tn, IDX_DIM),
            buf[None, :, -POOL_CTX:])
```

```python
import functools
import math

import jax
import jax.numpy as jnp
from jax import lax
from jax.experimental import pallas as pl
from jax.experimental.pallas import tpu as pltpu

D_MODEL = 1024
N_HEADS = 8
HEAD_DIM = 64
ATTN_W = N_HEADS * HEAD_DIM
IDX_HEADS = 4
IDX_DIM = 64
IDX_W = IDX_HEADS * IDX_DIM
TOPK_MAX = 256
PAGE_SIZE = 128
REL_BUCKETS = 32
REL_MAX_EXACT = 16
REL_MAX_DIST = 128
POOL_GROUPS = 4
POOL_GC = 128
POOL_W = POOL_GROUPS * POOL_GC
POOL_WINDOWS = (2, 4, 8, 16)
POOL_CTX = 15
N_GROUPS = 4
EXPERTS_PER_GROUP = 8
N_EXPERTS = N_GROUPS * EXPERTS_PER_GROUP
TOP_K_FINE = 2
D_EXPERT = 512
RMS_EPS = 1e-6

LANES = 128
SUBLANES = 8
HEAD_PAIR = 2 * HEAD_DIM
INT_MIN = -(2 ** 31)
BF16_MIN_NORMAL_BITS = 0x0080
NEG_BIG = -1e30
VMEM_LIMIT = 56 * 1024 * 1024
ROW_TILE = 512
Q_TILE = 128
K_TILE = 512
COUNT_ROWS = 64
SUM_ROWS = 16
PAGES_PER_STEP = 8
MOE_ROWS = 256
POOL_HALO = 16
FINE_LANE0 = 8

_NT = (((1,), (1,)), ((), ()))
_BF = jnp.bfloat16
_F32 = jnp.float32


def _resident(shape, index_map):
    return pl.BlockSpec(shape, index_map, pipeline_mode=pl.Buffered(1))


def _order_key(score):
    bits = pltpu.bitcast(score, jnp.int32)
    key = bits ^ ((bits >> 31) & 0x7FFFFFFF)
    return jnp.where(key == -1, 0, key)


def _bisect_threshold(count_ge, shape, topk, n_keys, count_ge_hi=None):
    def step(count, bit, carry):
        lo, n_lo = carry
        cand = lo + jnp.left_shift(jnp.int32(1), bit)
        n_cand = count(cand)
        ok = n_cand >= topk
        return jnp.where(ok, cand, lo), jnp.where(ok, n_cand, n_lo)

    n0 = jnp.full(shape, n_keys, _F32)
    if count_ge_hi is None:
        return lax.fori_loop(0, 32, lambda i, c: step(count_ge, 31 - i, c),
                             (jnp.full(shape, INT_MIN, jnp.int32), n0))
    hi, n_hi = lax.fori_loop(0, 16, lambda i, c: step(count_ge_hi, 15 - i, c),
                             (jnp.full(shape, -(2 ** 15), jnp.int32), n0))
    return lax.fori_loop(0, 16, lambda i, c: step(count_ge, 15 - i, c),
                         (jnp.left_shift(hi, 16), n_hi))


def _fold_rows(x, op, rows=SUBLANES):
    while x.shape[0] > rows:
        half = x.shape[0] // 2
        x = op(x[:half], x[half:])
    return x


def _rms_scale(x, g):
    ms = jnp.mean(x * x, axis=-1, keepdims=True)
    return x * lax.rsqrt(ms + RMS_EPS) * g


_ROW_SECTIONS = (("k", ATTN_W), ("u", POOL_W), ("ga", D_MODEL), ("gp", D_MODEL),
                 ("ki", LANES))
_COL_SECTIONS = (("q", ATTN_W), ("qi", IDX_W), ("v", ATTN_W), ("k", ATTN_W), ("ki", IDX_DIM),
                 ("wi", SUBLANES))


def _proj_kernel(x_ref, g_ref, wa_ref, wbt_ref,
                 u_ref, ga_ref, gp_ref, k16_ref, ki16_ref,
                 qt_ref, qit_ref, vt_ref, wit_ref, kt32_ref, vt32_ref, kit32_ref):
    h = _rms_scale(x_ref[...], g_ref[...]).astype(_BF)
    lo = 0
    for name, width in _ROW_SECTIONS:
        t = jnp.dot(h, wa_ref[:, lo:lo + width], preferred_element_type=_F32)
        lo += width
        if name == "k":
            k16_ref[...] = t.astype(_BF)
        elif name == "ki":
            ki16_ref[...] = t[:, :IDX_DIM].astype(_BF)
        else:
            {"u": u_ref, "ga": ga_ref, "gp": gp_ref}[name][...] = t
    lo = 0
    for name, width in _COL_SECTIONS:
        t = lax.dot_general(wbt_ref[lo:lo + width, :], h, _NT, preferred_element_type=_F32)
        lo += width
        if name == "q":
            qt_ref[...] = t.astype(_BF)
        elif name == "qi":
            qit_ref[...] = t.astype(_BF)
        elif name == "v":
            vt_ref[0] = t.astype(_BF)
            vt32_ref[0] = t
        elif name == "k":
            kt32_ref[0] = t
        elif name == "ki":
            kit32_ref[0] = t
        else:
            wit_ref[...] = t


def _project(x, ln_g, w_in, n_seq):
    n = x.shape[0]
    tm = ROW_TILE
    assert n % (tm * n_seq) == 0 and tm == K_TILE
    seq_tiles = n // (tm * n_seq)
    widths = (ATTN_W, ATTN_W, ATTN_W, IDX_W, IDX_DIM, IDX_HEADS, POOL_W, D_MODEL, D_MODEL)
    names = ("q", "k", "v", "qi", "ki", "wi", "u", "ga", "gp")
    cols, lo = {}, 0
    for name, width in zip(names, widths):
        cols[name] = w_in[:, lo:lo + width]
        lo += width
    ki_pad = jnp.pad(cols["ki"], ((0, 0), (0, LANES - IDX_DIM)))
    wa = jnp.concatenate([cols["k"], cols["u"], cols["ga"], cols["gp"], ki_pad], axis=1).astype(_BF)
    wi_pad = jnp.pad(cols["wi"], ((0, 0), (0, SUBLANES - IDX_HEADS)))
    wbt = jnp.concatenate([cols["q"] * HEAD_DIM ** -0.5, cols["qi"] * IDX_DIM ** -0.5, cols["v"],
                           cols["k"], cols["ki"], wi_pad], axis=1).T.astype(_BF)
    na, nb = wa.shape[1], wbt.shape[0]
    row = lambda w: pl.BlockSpec((tm, w), lambda i: (i, 0))
    colb = lambda h: pl.BlockSpec((h, tm), lambda i: (0, i))
    seqb = lambda h: pl.BlockSpec((1, h, tm), lambda i: (i // seq_tiles, 0, i % seq_tiles))
    seq_shape = lambda h: (n_seq, h, n // n_seq)
    out_shapes = dict(
        u=((n, POOL_W), _F32, row(POOL_W)), ga=((n, D_MODEL), _F32, row(D_MODEL)),
        gp=((n, D_MODEL), _F32, row(D_MODEL)),
        k16=((n, ATTN_W), _BF, row(ATTN_W)), ki16=((n, IDX_DIM), _BF, row(IDX_DIM)),
        qT=((ATTN_W, n), _BF, colb(ATTN_W)), qiT=((IDX_W, n), _BF, colb(IDX_W)),
        vT=((n // tm, ATTN_W, tm), _BF, pl.BlockSpec((1, ATTN_W, tm), lambda i: (i, 0, 0))),
        wiT=((SUBLANES, n), _F32, colb(SUBLANES)),
        kT32=(seq_shape(ATTN_W), _F32, seqb(ATTN_W)), vT32=(seq_shape(ATTN_W), _F32, seqb(ATTN_W)),
        kiT32=(seq_shape(IDX_DIM), _F32, seqb(IDX_DIM)),
    )
    keys = list(out_shapes)
    res = pl.pallas_call(
        _proj_kernel,
        grid=(n // tm,),
        in_specs=[row(D_MODEL), _resident((1, D_MODEL), lambda i: (0, 0)),
                  _resident((D_MODEL, na), lambda i: (0, 0)),
                  _resident((nb, D_MODEL), lambda i: (0, 0))],
        out_specs=[out_shapes[k][2] for k in keys],
        out_shape=[jax.ShapeDtypeStruct(out_shapes[k][0], out_shapes[k][1]) for k in keys],
        compiler_params=pltpu.CompilerParams(dimension_semantics=("arbitrary",),
                                             vmem_limit_bytes=VMEM_LIMIT),
        name="input_projection",
    )(x, ln_g.reshape(1, D_MODEL).astype(_F32), wa, wbt)
    return dict(zip(keys, res))


def _prompt_attn_kernel(qT_ref, qiT_ref, wiT_ref, k_ref, vT_ref, ki_ref, bias_ref, ltri_ref,
                        out_ref, keys_scr, top_scr, mask_scr, qm_scr, m_scr, acc_scr, *, topk):
    TQ, TK = Q_TILE, K_TILE
    sub = TK // TQ
    qb = pl.program_id(1)
    n_sb = qb // sub + 1
    n_far = jnp.maximum(qb - 1, 0) // sub
    key_off = lax.broadcasted_iota(jnp.int32, (TK, TQ), 0)
    qry_off = lax.broadcasted_iota(jnp.int32, (TK, TQ), 1)

    def causal_at(sb):
        return (sb * TK + key_off) <= (qb * TQ + qry_off)

    qi_wide = jnp.concatenate([qiT_ref[h * IDX_DIM:(h + 1) * IDX_DIM, :] for h in range(IDX_HEADS)],
                              axis=1)
    wT = wiT_ref[...]

    def score_keys(sb, causal=None):
        kib = ki_ref[pl.ds(pl.multiple_of(sb * TK, TK), TK), :]
        d = jnp.dot(kib, qi_wide, preferred_element_type=_F32)
        sc = None
        for h in range(IDX_HEADS):
            term = wT[h:h + 1, :] * jnp.maximum(d[:, h * TQ:(h + 1) * TQ], 0.0)
            sc = term if sc is None else sc + term
        key = _order_key(sc)
        top = pltpu.bitcast(pltpu.bitcast(sc, jnp.int32) & -65536, _F32)
        if causal is not None:
            key = jnp.where(causal, key, INT_MIN)
            top = jnp.where(causal, top, -jnp.inf)
        keys_scr[sb] = key
        top_scr[sb] = top.astype(_BF)

    def score_body(sb, carry):
        score_keys(sb)
        return carry

    lax.fori_loop(0, n_sb - 1, score_body, 0)
    score_keys(n_sb - 1, causal_at(n_sb - 1))

    def count(pred):
        def body(sb, acc):
            hit = jnp.where(pred(keys_scr[sb]), 1.0, 0.0)
            return acc + jnp.sum(hit.reshape(TK // COUNT_ROWS, COUNT_ROWS, TQ), axis=0)
        acc = lax.fori_loop(0, n_sb, body, jnp.zeros((COUNT_ROWS, TQ), _F32))
        return jnp.sum(_fold_rows(acc, jnp.add), axis=0, keepdims=True)

    def count_ge_hi(cand_hi):
        bits_hi = (cand_hi ^ ((cand_hi >> 31) & 0x7FFF)) & 0xFFFF
        bits_hi = jnp.where((bits_hi > 0) & (bits_hi < BF16_MIN_NORMAL_BITS), BF16_MIN_NORMAL_BITS,
                            bits_hi)
        cand = pltpu.bitcast(jnp.left_shift(bits_hi, 16), _F32).astype(_BF)
        one, zero = jnp.ones((), _BF), jnp.zeros((), _BF)

        def body(sb, acc):
            hit = jnp.where(top_scr[sb] >= cand, one, zero)
            return acc + _fold_rows(hit, jnp.add, rows=COUNT_ROWS)
        acc = lax.fori_loop(0, n_sb, body, jnp.zeros((COUNT_ROWS, TQ), _BF))
        return jnp.sum(_fold_rows(acc.astype(_F32), jnp.add), axis=0, keepdims=True)

    kf = float(topk)
    thr, n_ge = _bisect_threshold(lambda c: count(lambda kk: kk >= c), (1, TQ), kf,
                                  (n_sb * TK).astype(_F32), count_ge_hi)
    has_ties = jnp.max(jnp.where(n_ge != kf, 1.0, 0.0)) > 0.0
    n_tie_keep = kf - lax.cond(has_ties, lambda: count(lambda kk: kk > thr),
                               lambda: jnp.zeros((1, TQ), _F32))

    zeros_half = jnp.zeros((HEAD_DIM, TQ), _BF)
    for pair in range(N_HEADS // 2):
        q0 = qT_ref[(2 * pair) * HEAD_DIM:(2 * pair + 1) * HEAD_DIM, :]
        q1 = qT_ref[(2 * pair + 1) * HEAD_DIM:(2 * pair + 2) * HEAD_DIM, :]
        qm_scr[pair] = jnp.concatenate([jnp.concatenate([q0, zeros_half], axis=1),
                                        jnp.concatenate([zeros_half, q1], axis=1)], axis=0)
    m_scr[...] = jnp.full(m_scr.shape, NEG_BIG, _F32)
    acc_scr[...] = jnp.zeros(acc_scr.shape, _F32)
    ones_rows = jnp.ones((SUM_ROWS, TK), _BF)
    n_pairs = N_HEADS // 2

    def logits(sb, pair):
        kp = k_ref[pl.ds(pl.multiple_of(sb * TK, TK), TK),
                   pair * HEAD_PAIR:(pair + 1) * HEAD_PAIR]
        return jnp.dot(kp, qm_scr[pair], preferred_element_type=_F32)

    def attend_block(sb, tie_seen, near, ties):
        s_next = logits(sb, 0)
        keyb = keys_scr[sb]
        if ties:
            eq = keyb == thr
            eqf = jnp.where(eq, 1.0, 0.0)
            rank = tie_seen + jnp.dot(ltri_ref[...], eqf.astype(_BF), preferred_element_type=_F32)
            keep = (keyb > thr) | (eq & (rank < n_tie_keep))
            madd = jnp.where(keep, 0.0, NEG_BIG)
            tie_seen = tie_seen + jnp.sum(eqf, axis=0, keepdims=True)
        else:
            madd = jnp.where(keyb >= thr, 0.0, NEG_BIG)
        if near:
            madd = jnp.where(causal_at(sb), madd, NEG_BIG)
        mask_scr[...] = madd
        for pair in range(n_pairs):
            s2 = s_next
            if pair + 1 < n_pairs:
                s_next = logits(sb, pair + 1)
            vp = vT_ref[sb, pair * HEAD_PAIR:(pair + 1) * HEAD_PAIR, :]
            halves = []
            for odd in range(2):
                h = 2 * pair + odd
                s = s2[:, odd * TQ:(odd + 1) * TQ] + mask_scr[...]
                if near:
                    parts = []
                    for j in range(sub):
                        back = qb - (sb * sub + j)
                        w0 = (back == 0).astype(_F32)
                        w1 = (back == 1).astype(_F32)
                        parts.append(s[j * TQ:(j + 1) * TQ] + w0 * bias_ref[0, h] + w1 * bias_ref[1, h])
                    s = jnp.concatenate(parts, axis=0)
                halves.append(s)
            m_old = m_scr[pair]
            m_blk = jnp.concatenate(
                [jnp.max(_fold_rows(s, jnp.maximum), axis=0, keepdims=True) for s in halves], axis=1)
            m_new = jnp.maximum(m_old, m_blk)
            p2 = jnp.concatenate([jnp.exp(s - m_new[:, odd * TQ:(odd + 1) * TQ]).astype(_BF)
                                  for odd, s in enumerate(halves)], axis=1)
            alpha = jnp.exp(m_old - m_new)
            va = jnp.concatenate([vp, ones_rows], axis=0)
            acc_scr[pair] = alpha * acc_scr[pair] + jnp.dot(va, p2, preferred_element_type=_F32)
            m_scr[pair] = m_new
        return tie_seen

    def attend_all(ties):
        def run():
            seen = lax.fori_loop(0, n_far, lambda sb, c: attend_block(sb, c, False, ties),
                                 jnp.zeros((1, TQ), _F32))
            lax.fori_loop(n_far, n_sb, lambda sb, c: attend_block(sb, c, True, ties), seen)
        return run

    lax.cond(has_ties, attend_all(True), attend_all(False))

    for pair in range(N_HEADS // 2):
        a = acc_scr[pair]
        res = jnp.concatenate(
            [a[:HEAD_DIM, :TQ] / a[HEAD_PAIR:HEAD_PAIR + 1, :TQ],
             a[HEAD_DIM:HEAD_PAIR, TQ:] / a[HEAD_PAIR:HEAD_PAIR + 1, TQ:]], axis=0)
        out_ref[:, pair * HEAD_PAIR:(pair + 1) * HEAD_PAIR] = res.T.astype(out_ref.dtype)


def _rel_bias_by_distance(rel_bias, n):
    dist = jnp.arange(n, dtype=jnp.int32)
    nf = jnp.maximum(dist, 1).astype(_F32)
    large = REL_MAX_EXACT + (jnp.log(nf / REL_MAX_EXACT) / math.log(REL_MAX_DIST / REL_MAX_EXACT)
                             * (REL_BUCKETS - REL_MAX_EXACT)).astype(jnp.int32)
    large = jnp.minimum(large, REL_BUCKETS - 1)
    bucket = jnp.where(dist < REL_MAX_EXACT, dist, large)
    return (rel_bias[bucket] - rel_bias[REL_BUCKETS - 1][None, :]).astype(_F32)


def _prompt_attention(proj, rel_bias, nb, s):
    TQ, TK = Q_TILE, K_TILE
    assert s % TK == 0
    nq, nsb = s // TQ, s // TK
    assert (TK // COUNT_ROWS) * nsb <= 256
    topk = min(TOPK_MAX, s // 4)
    bd = _rel_bias_by_distance(rel_bias, 2 * TQ)
    key_off = jnp.arange(TQ)[:, None]
    qry_off = jnp.arange(TQ)[None, :]
    tiles = [jnp.transpose(bd[jnp.clip(back * TQ + qry_off - key_off, 0, 2 * TQ - 1)], (2, 0, 1))
             for back in range(2)]
    bias_tiles = jnp.stack(tiles)
    ltri = (jnp.arange(TK)[:, None] > jnp.arange(TK)[None, :]).astype(_BF)

    kern = functools.partial(_prompt_attn_kernel, topk=topk)
    return pl.pallas_call(
        kern,
        grid=(nb, nq),
        in_specs=[
            pl.BlockSpec((ATTN_W, TQ), lambda i, j: (0, i * nq + j)),
            pl.BlockSpec((IDX_W, TQ), lambda i, j: (0, i * nq + j)),
            pl.BlockSpec((SUBLANES, TQ), lambda i, j: (0, i * nq + j)),
            _resident((s, ATTN_W), lambda i, j: (i, 0)),
            _resident((nsb, ATTN_W, TK), lambda i, j: (i, 0, 0)),
            _resident((s, IDX_DIM), lambda i, j: (i, 0)),
            _resident((2, N_HEADS, TQ, TQ), lambda i, j: (0, 0, 0, 0)),
            _resident((TK, TK), lambda i, j: (0, 0)),
        ],
        out_specs=pl.BlockSpec((TQ, ATTN_W), lambda i, j: (i * nq + j, 0)),
        out_shape=jax.ShapeDtypeStruct((nb * s, ATTN_W), _BF),
        scratch_shapes=[
            pltpu.VMEM((nsb, TK, TQ), jnp.int32),
            pltpu.VMEM((nsb, TK, TQ), _BF),
            pltpu.VMEM((TK, TQ), _F32),
            pltpu.VMEM((N_HEADS // 2, HEAD_PAIR, 2 * TQ), _BF),
            pltpu.VMEM((N_HEADS // 2, 1, 2 * TQ), _F32),
            pltpu.VMEM((N_HEADS // 2, HEAD_PAIR + SUM_ROWS, 2 * TQ), _F32),
        ],
        compiler_params=pltpu.CompilerParams(
            dimension_semantics=("arbitrary", "arbitrary"), vmem_limit_bytes=VMEM_LIMIT),
        name="prompt_sparse_attention",
    )(proj["qT"], proj["qiT"], proj["wiT"], proj["k16"], proj["vT"], proj["ki16"], bias_tiles, ltri)


def _sample_attn_kernel(pt_ref, qis_ref, wis_ref, kinew_ref, qbd_ref, knew_ref, vnew_ref,
                        *rest, n_pages, t_new, topk):
    del pt_ref
    G = PAGES_PER_STEP
    kidx_refs, k_refs, v_refs = rest[:G], rest[G:2 * G], rest[2 * G:3 * G]
    (biasl_ref, biasn_ref, utri_ref, out_ref, keys_scr, thr_scr, keep_scr,
     tie_scr, flag_scr, m_scr, l_scr, acc_scr) = rest[3 * G:]
    NP = n_pages
    n_steps = NP // G
    phase = pl.program_id(1)
    p = pl.program_id(2)

    def score_keys(kit):
        d = jnp.dot(qis_ref[0], kit, preferred_element_type=_F32)
        w = wis_ref[0]
        sc = None
        for h in range(IDX_HEADS):
            term = w[:, h:h + 1] * jnp.maximum(d[h * t_new:(h + 1) * t_new], 0.0)
            sc = term if sc is None else sc + term
        return _order_key(sc)

    @pl.when(phase == 0)
    def _():
        for j in range(G):
            keys_scr[p * G + j] = score_keys(kidx_refs[j][0].astype(_BF))

    @pl.when((phase == 0) & (p == n_steps - 1))
    def _():
        lane = lax.broadcasted_iota(jnp.int32, (t_new, PAGE_SIZE), 1)
        qrow = lax.broadcasted_iota(jnp.int32, (t_new, PAGE_SIZE), 0)
        keys_scr[NP] = jnp.where(lane <= qrow, score_keys(kinew_ref[0]), INT_MIN)

        def count(pred):
            past_hit = jnp.where(pred(keys_scr[0:NP]), 1.0, 0.0).reshape(NP * t_new, PAGE_SIZE)
            hit = _fold_rows(past_hit, jnp.add, rows=t_new)
            hit = hit + jnp.where(pred(keys_scr[NP]), 1.0, 0.0)
            return jnp.sum(hit, axis=1, keepdims=True)

        kf = float(topk)
        thr, n_ge = _bisect_threshold(lambda c: count(lambda kk: kk >= c), (t_new, 1), kf,
                                      float((NP + 1) * PAGE_SIZE))
        n_gt = count(lambda kk: kk > thr)
        thr_scr[...] = thr
        keep_scr[...] = kf - n_gt
        flag_scr[0] = (jnp.max(jnp.where(n_ge != kf, 1.0, 0.0)) > 0.0).astype(jnp.int32)
        tie_scr[...] = jnp.zeros(tie_scr.shape, _F32)
        m_scr[...] = jnp.full(m_scr.shape, NEG_BIG, _F32)
        l_scr[...] = jnp.zeros(l_scr.shape, _F32)
        acc_scr[...] = jnp.zeros(acc_scr.shape, _F32)

    def attend(keybs, kmats, vmats, bias):
        thr = thr_scr[...]

        def tie_sel():
            masks = []
            for keyb in keybs:
                eq = keyb == thr
                eqf = jnp.where(eq, 1.0, 0.0)
                rank = tie_scr[...] + jnp.dot(eqf.astype(_BF), utri_ref[...],
                                              preferred_element_type=_F32)
                keep = (keyb > thr) | (eq & (rank < keep_scr[...]))
                tie_scr[...] = tie_scr[...] + jnp.sum(eqf, axis=1, keepdims=True)
                masks.append(jnp.where(keep, 0.0, NEG_BIG))
            return jnp.concatenate(masks, axis=1)

        madd8 = lax.cond(flag_scr[0] != 0, tie_sel, lambda: jnp.concatenate(
            [jnp.where(keyb >= thr, 0.0, NEG_BIG) for keyb in keybs], axis=1))
        madd = jnp.concatenate([madd8] * N_HEADS, axis=0)
        s = jnp.concatenate([jnp.dot(qbd_ref[0], km, preferred_element_type=_F32) for km in kmats],
                            axis=1) + madd
        if bias is not None:
            s = s + bias
        m_old = m_scr[...]
        m_new = jnp.maximum(m_old, jnp.max(s, axis=1, keepdims=True))
        pr = jnp.exp(s - m_new)
        alpha = jnp.exp(m_old - m_new)
        l_scr[...] = alpha * l_scr[...] + jnp.sum(pr, axis=1, keepdims=True)
        prb = pr.astype(_BF)
        pv = None
        for j, vm in enumerate(vmats):
            t = lax.dot_general(prb[:, j * PAGE_SIZE:(j + 1) * PAGE_SIZE], vm, _NT,
                                preferred_element_type=_F32)
            pv = t if pv is None else pv + t
        acc_scr[...] = alpha * acc_scr[...] + pv
        m_scr[...] = m_new

    @pl.when(phase == 1)
    def _():
        bias = (p == n_steps - 1).astype(_F32) * biasl_ref[...]
        attend([keys_scr[p * G + j] for j in range(G)],
               [k_refs[j][0].astype(_BF) for j in range(G)],
               [v_refs[j][0].astype(_BF) for j in range(G)], bias)

    @pl.when((phase == 1) & (p == n_steps - 1))
    def _():
        attend([keys_scr[NP]], [knew_ref[0]], [vnew_ref[0]], biasn_ref[...])
        o = acc_scr[...] / l_scr[...]
        head_of_lane = lax.broadcasted_iota(jnp.int32, (t_new, ATTN_W), 1) // HEAD_DIM
        res = jnp.zeros((t_new, ATTN_W), _F32)
        for h in range(N_HEADS):
            res = res + jnp.where(head_of_lane == h, o[h * t_new:(h + 1) * t_new], 0.0)
        out_ref[0] = res.astype(out_ref.dtype)


def _sample_attention(q, k_new, v_new, qi, ki_new, wi, cache_k, cache_v, cache_kidx, page_table,
                      rel_bias):
    nb, t_new = q.shape[:2]
    n_pages = page_table.shape[1]
    n_phys = cache_k.shape[0]
    past = n_pages * PAGE_SIZE
    topk = min(TOPK_MAX, (past + t_new) // 4)
    rows = N_HEADS * t_new
    qis = jnp.swapaxes(qi.reshape(nb, t_new, IDX_HEADS, IDX_DIM), 1, 2)
    qis = qis.reshape(nb, IDX_HEADS * t_new, IDX_DIM)
    G = PAGES_PER_STEP
    assert n_pages % G == 0 and n_pages & (n_pages - 1) == 0
    n_steps = n_pages // G
    q4 = jnp.swapaxes(q.reshape(nb, t_new, N_HEADS, HEAD_DIM), 1, 2)
    eye = jnp.eye(N_HEADS, dtype=_BF)
    qbd = (q4[:, :, :, None, :] * eye[None, :, None, :, None]).reshape(nb, rows, ATTN_W)

    def keys_minor(a):
        return jnp.pad(jnp.swapaxes(a, 1, 2), ((0, 0), (0, 0), (0, PAGE_SIZE - t_new)))

    bd = _rel_bias_by_distance(rel_bias, 2 * PAGE_SIZE)
    qo = jnp.arange(t_new)[:, None]
    co = jnp.arange(PAGE_SIZE)[None, :]
    d_last = jnp.clip(PAGE_SIZE + qo - co, 0, 2 * PAGE_SIZE - 1)
    d_new = jnp.clip(qo - co, 0, 2 * PAGE_SIZE - 1)
    bias_last = jnp.transpose(bd[d_last], (2, 0, 1)).reshape(rows, PAGE_SIZE)
    bias_last = jnp.pad(bias_last, ((0, 0), ((G - 1) * PAGE_SIZE, 0)))
    bias_new = jnp.transpose(bd[d_new], (2, 0, 1)).reshape(rows, PAGE_SIZE)
    utri = (jnp.arange(PAGE_SIZE)[:, None] < jnp.arange(PAGE_SIZE)[None, :]).astype(_BF)

    kern = functools.partial(_sample_attn_kernel, n_pages=n_pages, t_new=t_new, topk=topk)
    seq = lambda i, ph, p, pt: (i, 0, 0)
    const2 = lambda i, ph, p, pt: (0, 0)

    def kidx_page(j):
        return lambda i, ph, p, pt: (
            jnp.where(ph == 0, pt[i * n_pages + p * G + j], pt[i * n_pages + n_pages - G + j]), 0, 0)

    def kv_page(j):
        return lambda i, ph, p, pt: (
            jnp.where(ph == 1, pt[i * n_pages + p * G + j], pt[i * n_pages + j]), 0, 0)

    grid_spec = pltpu.PrefetchScalarGridSpec(
        num_scalar_prefetch=1,
        grid=(nb, 2, n_steps),
        in_specs=[
            pl.BlockSpec((1, IDX_HEADS * t_new, IDX_DIM), seq),
            pl.BlockSpec((1, t_new, IDX_HEADS), seq),
            pl.BlockSpec((1, IDX_DIM, PAGE_SIZE), seq),
            pl.BlockSpec((1, rows, ATTN_W), seq),
            pl.BlockSpec((1, ATTN_W, PAGE_SIZE), seq),
            pl.BlockSpec((1, ATTN_W, PAGE_SIZE), seq),
        ] + [pl.BlockSpec((1, IDX_DIM, PAGE_SIZE), kidx_page(j)) for j in range(G)]
          + [pl.BlockSpec((1, ATTN_W, PAGE_SIZE), kv_page(j)) for j in range(G)]
          + [pl.BlockSpec((1, ATTN_W, PAGE_SIZE), kv_page(j)) for j in range(G)]
          + [
            pl.BlockSpec((rows, G * PAGE_SIZE), const2),
            pl.BlockSpec((rows, PAGE_SIZE), const2),
            pl.BlockSpec((PAGE_SIZE, PAGE_SIZE), const2),
        ],
        out_specs=pl.BlockSpec((1, t_new, ATTN_W), seq),
        scratch_shapes=[
            pltpu.VMEM((n_pages + 1, t_new, PAGE_SIZE), jnp.int32),
            pltpu.VMEM((t_new, 1), jnp.int32),
            pltpu.VMEM((t_new, 1), _F32),
            pltpu.VMEM((t_new, 1), _F32),
            pltpu.SMEM((1,), jnp.int32),
            pltpu.VMEM((rows, 1), _F32),
            pltpu.VMEM((rows, 1), _F32),
            pltpu.VMEM((rows, ATTN_W), _F32),
        ],
    )
    kidx_pages = jnp.swapaxes(cache_kidx, 1, 2)
    k_pages = jnp.transpose(cache_k, (0, 2, 3, 1)).reshape(n_phys, ATTN_W, PAGE_SIZE)
    v_pages = jnp.transpose(cache_v, (0, 2, 3, 1)).reshape(n_phys, ATTN_W, PAGE_SIZE)
    return pl.pallas_call(
        kern,
        grid_spec=grid_spec,
        out_shape=jax.ShapeDtypeStruct((nb, t_new, ATTN_W), _BF),
        compiler_params=pltpu.CompilerParams(
            dimension_semantics=("arbitrary", "arbitrary", "arbitrary"),
            vmem_limit_bytes=VMEM_LIMIT),
        name="sample_sparse_attention",
    )(page_table.reshape(-1).astype(jnp.int32), qis, wi.astype(_F32), keys_minor(ki_new), qbd,
      keys_minor(k_new), keys_minor(v_new),
      *([kidx_pages] * G), *([k_pages] * G), *([v_pages] * G), bias_last, bias_new, utri)


def _pool_diff_kernel(u_ref, halo_ref, d_ref, *, truncate_start):
    tm = u_ref.shape[0]
    i = pl.program_id(1)
    main = u_ref[...]
    halo = jnp.where(i == 0, 0.0, halo_ref[...])
    ext = jnp.concatenate([halo, main], axis=0)
    pos = i * tm + lax.broadcasted_iota(jnp.int32, (tm, POOL_GC), 0)
    for g, w in enumerate(POOL_WINDOWS):
        sl = slice(g * POOL_GC, (g + 1) * POOL_GC)
        acc = ext[:, sl]
        shift = 1
        while shift < w:
            acc = acc + pltpu.roll(acc, shift, 0)
            shift *= 2
        wsum = acc[POOL_HALO:]
        cnt = jnp.minimum(pos + 1, w).astype(_F32) if truncate_start else float(w)
        d_ref[:, sl] = (wsum / cnt - main[:, sl]).astype(d_ref.dtype)


def _pool_diff(u, n_seq, seq_rows, truncate_start):
    tm = min(ROW_TILE, seq_rows)
    assert seq_rows % tm == 0 and tm % POOL_HALO == 0
    nt = seq_rows // tm
    hb = tm // POOL_HALO
    kern = functools.partial(_pool_diff_kernel, truncate_start=truncate_start)
    return pl.pallas_call(
        kern,
        grid=(n_seq, nt),
        in_specs=[pl.BlockSpec((tm, POOL_W), lambda b, i: (b * nt + i, 0)),
                  pl.BlockSpec((POOL_HALO, POOL_W),
                               lambda b, i: (jnp.maximum((b * nt + i) * hb - 1, 0), 0))],
        out_specs=pl.BlockSpec((tm, POOL_W), lambda b, i: (b * nt + i, 0)),
        out_shape=jax.ShapeDtypeStruct(u.shape, _BF),
        compiler_params=pltpu.CompilerParams(dimension_semantics=("arbitrary", "arbitrary")),
        name="pool_window_diff",
    )(u, u)


def _merge_route_kernel(x_ref, attn_ref, d_ref, ga_ref, gp_ref, wmap_ref, pscale_ref, wba_ref,
                        wbp_ref, wout_ref, g2_ref, wr_ref, br_ref, ltri_ref,
                        x1_ref, xn_ref, eid_ref, gate_ref, rank_ref, cnt_ref):
    tm = x_ref.shape[0]
    d = d_ref[...]
    pooled = [jnp.dot(d[:, g * POOL_GC:(g + 1) * POOL_GC], wmap_ref[g], preferred_element_type=_F32)
              for g in range(POOL_GROUPS)]
    pool = (jnp.concatenate(pooled, axis=1) * pscale_ref[...]).astype(_BF)
    a = jnp.dot(attn_ref[...], wba_ref[...], preferred_element_type=_F32)
    pp = jnp.dot(pool, wbp_ref[...], preferred_element_type=_F32)
    m = jax.nn.sigmoid(ga_ref[...]) * a + jax.nn.sigmoid(gp_ref[...]) * pp
    x1 = x_ref[...] + jnp.dot(m.astype(_BF), wout_ref[...], preferred_element_type=_F32)
    x1_ref[...] = x1
    xn = _rms_scale(x1, g2_ref[...]).astype(_BF)
    xn_ref[...] = xn

    logit = jnp.dot(xn, wr_ref[...], preferred_element_type=_F32) + br_ref[...]
    lane = lax.broadcasted_iota(jnp.int32, (tm, LANES), 1).astype(_F32)
    far = float(LANES)

    def first_lane_of(hit):
        return jnp.min(jnp.where(hit, lane, far), axis=1, keepdims=True)

    lc = jnp.where(lane < N_GROUPS, logit, NEG_BIG)
    mc = jnp.max(lc, axis=1, keepdims=True)
    p_grp = 1.0 / jnp.sum(jnp.exp(lc - mc), axis=1, keepdims=True)
    grp = first_lane_of(lc == mc)
    lo = FINE_LANE0 + EXPERTS_PER_GROUP * grp
    in_grp = (lane >= lo) & (lane < lo + EXPERTS_PER_GROUP)
    lf = jnp.where(in_grp, logit, NEG_BIG)
    ef = jnp.exp(lf - jnp.max(lf, axis=1, keepdims=True))
    pf = jnp.where(in_grp, ef / jnp.sum(ef, axis=1, keepdims=True), -1.0)
    p1 = jnp.max(pf, axis=1, keepdims=True)
    l1 = first_lane_of(pf == p1)
    pf2 = jnp.where(lane == l1, -1.0, pf)
    p2 = jnp.max(pf2, axis=1, keepdims=True)
    l2 = first_lane_of(pf2 == p2)
    e1 = l1 - FINE_LANE0
    e2 = l2 - FINE_LANE0
    g1 = p_grp * p1 / (p1 + p2)
    g2 = p_grp * p2 / (p1 + p2)
    eid_ref[...] = jnp.where(lane == 0, e1, jnp.where(lane == 1, e2, 0.0)).astype(jnp.int32)
    gate_ref[...] = jnp.where(lane == 0, g1, jnp.where(lane == 1, g2, 0.0))

    onehot = jnp.where((lane == e1) | (lane == e2), 1.0, 0.0)
    before = jnp.dot(ltri_ref[...], onehot.astype(_BF), preferred_element_type=_F32)
    r1 = jnp.sum(jnp.where(lane == e1, before, 0.0), axis=1, keepdims=True)
    r2 = jnp.sum(jnp.where(lane == e2, before, 0.0), axis=1, keepdims=True)
    rank_ref[...] = jnp.where(lane == 0, r1, jnp.where(lane == 1, r2, 0.0)).astype(jnp.int32)
    cnt_ref[0] = jnp.broadcast_to(jnp.sum(onehot, axis=0, keepdims=True), (SUBLANES, LANES))


def _merge_route(x, attn, d, ga, gp, w):
    n = x.shape[0]
    tm = ROW_TILE
    assert n % tm == 0
    nt = n // tm
    row = lambda width: pl.BlockSpec((tm, width), lambda i: (i, 0))
    full = lambda a: _resident(a.shape, lambda i: (0,) * a.ndim)
    weights = (w["wmap"], w["pscale"], w["wba"], w["wbp"], w["wout"], w["g2"], w["wr"], w["br"],
               w["ltri"])
    return pl.pallas_call(
        _merge_route_kernel,
        grid=(nt,),
        in_specs=[row(D_MODEL), row(ATTN_W), row(POOL_W), row(D_MODEL), row(D_MODEL)]
                 + [full(a) for a in weights],
        out_specs=[row(D_MODEL), row(D_MODEL), row(LANES), row(LANES), row(LANES),
                   pl.BlockSpec((1, SUBLANES, LANES), lambda i: (i, 0, 0))],
        out_shape=[jax.ShapeDtypeStruct((n, D_MODEL), _F32), jax.ShapeDtypeStruct((n, D_MODEL), _BF),
                   jax.ShapeDtypeStruct((n, LANES), jnp.int32), jax.ShapeDtypeStruct((n, LANES), _F32),
                   jax.ShapeDtypeStruct((n, LANES), jnp.int32),
                   jax.ShapeDtypeStruct((nt, SUBLANES, LANES), _F32)],
        compiler_params=pltpu.CompilerParams(dimension_semantics=("arbitrary",),
                                             vmem_limit_bytes=VMEM_LIMIT),
        name="merge_norm_route",
    )(x, attn, d, ga, gp, *weights)


def _experts_kernel(be_ref, nu_ref, x_ref, wg_ref, wu_ref, wd_ref, y_ref):
    del be_ref
    i = pl.program_id(0)

    @pl.when(i < nu_ref[0])
    def _():
        x = x_ref[...]
        hg = jnp.dot(x, wg_ref[0], preferred_element_type=_F32)
        hu = jnp.dot(x, wu_ref[0], preferred_element_type=_F32)
        hdn = (hg * jax.nn.sigmoid(hg) * hu).astype(_BF)
        y_ref[...] = jnp.dot(hdn, wd_ref[0], preferred_element_type=_F32).astype(y_ref.dtype)

    @pl.when(i >= nu_ref[0])
    def _():
        y_ref[...] = jnp.zeros(y_ref.shape, y_ref.dtype)


def _experts(x_pad, block_expert, n_used, wg, wu, wd):
    nblk = block_expert.shape[0]
    bm = MOE_ROWS
    used = lambda i, be, nu: (jnp.minimum(i, jnp.maximum(nu[0] - 1, 0)), 0)
    grid_spec = pltpu.PrefetchScalarGridSpec(
        num_scalar_prefetch=2,
        grid=(nblk,),
        in_specs=[pl.BlockSpec((bm, D_MODEL), used),
                  pl.BlockSpec((1, D_MODEL, D_EXPERT), lambda i, be, nu: (be[i], 0, 0)),
                  pl.BlockSpec((1, D_MODEL, D_EXPERT), lambda i, be, nu: (be[i], 0, 0)),
                  pl.BlockSpec((1, D_EXPERT, D_MODEL), lambda i, be, nu: (be[i], 0, 0))],
        out_specs=pl.BlockSpec((bm, D_MODEL), lambda i, be, nu: (i, 0)),
    )
    return pl.pallas_call(
        _experts_kernel,
        grid_spec=grid_spec,
        out_shape=jax.ShapeDtypeStruct((nblk * bm, D_MODEL), _BF),
        compiler_params=pltpu.CompilerParams(dimension_semantics=("arbitrary",),
                                             vmem_limit_bytes=VMEM_LIMIT),
        name="grouped_swiglu_experts",
    )(block_expert, n_used, x_pad, wg, wu, wd)


def _route_layout(eid, rank, tile_cnt):
    n = eid.shape[0]
    bm = MOE_ROWS
    cnt = tile_cnt[:, 0, :N_EXPERTS].astype(jnp.int32)
    tile_start = jnp.cumsum(cnt, axis=0) - cnt
    counts = jnp.sum(cnt, axis=0)
    padded = ((counts + bm - 1) // bm) * bm
    pad_ends = jnp.cumsum(padded)
    pad_starts = pad_ends - padded
    tile_of = jnp.arange(n, dtype=jnp.int32) // ROW_TILE
    dest = pad_starts[eid] + tile_start[tile_of[:, None], eid] + rank
    nblk = (n * TOP_K_FINE) // bm + N_EXPERTS
    block_expert = jnp.minimum(jnp.searchsorted(pad_ends, jnp.arange(nblk) * bm, side="right"),
                               N_EXPERTS - 1).astype(jnp.int32)
    n_used = (pad_ends[-1] // bm).astype(jnp.int32).reshape(1)
    return dest, block_expert, n_used, nblk


def _combine_norm_kernel(x1_ref, y1_ref, y2_ref, gate_ref, g_ref, out_ref):
    gate = gate_ref[...]
    x2 = (x1_ref[...] + gate[:, 0:1] * y1_ref[...].astype(_F32)
          + gate[:, 1:2] * y2_ref[...].astype(_F32))
    out_ref[...] = _rms_scale(x2, g_ref[...])


def _combine_norm(x1, y1, y2, gate, g):
    n = x1.shape[0]
    tm = ROW_TILE
    row = lambda width: pl.BlockSpec((tm, width), lambda i: (i, 0))
    return pl.pallas_call(
        _combine_norm_kernel,
        grid=(n // tm,),
        in_specs=[row(D_MODEL), row(D_MODEL), row(D_MODEL), row(LANES),
                  _resident((1, D_MODEL), lambda i: (0, 0))],
        out_specs=row(D_MODEL),
        out_shape=jax.ShapeDtypeStruct((n, D_MODEL), _F32),
        compiler_params=pltpu.CompilerParams(dimension_semantics=("arbitrary",)),
        name="combine_final_norm",
    )(x1, y1, y2, gate, g.reshape(1, D_MODEL).astype(_F32))


def kernel(x_prompt, x_sample, cache_k, cache_v, cache_kidx, state_pool, page_table, rel_bias,
           ln1_g, w_in, w_pool_map, pool_scale, w_br_attn, w_br_pool, w_out, ln2_g,
           w_coarse, b_coarse, w_fine, b_fine, w_gate, w_up, w_down, lnf_g):
    layer = 0
    nb, s = x_prompt.shape[:2]
    db, tn = x_sample.shape[:2]
    n_p, n_s = nb * s, db * tn
    rel_bias = rel_bias.astype(_F32)

    wr = jnp.zeros((D_MODEL, LANES), _F32)
    wr = wr.at[:, :N_GROUPS].set(w_coarse[layer])
    wr = wr.at[:, FINE_LANE0:FINE_LANE0 + N_EXPERTS].set(
        jnp.transpose(w_fine[layer], (1, 0, 2)).reshape(D_MODEL, N_EXPERTS))
    br = jnp.zeros((1, LANES), _F32)
    br = br.at[0, :N_GROUPS].set(b_coarse[layer])
    br = br.at[0, FINE_LANE0:FINE_LANE0 + N_EXPERTS].set(b_fine[layer].reshape(-1))
    mw = dict(
        wmap=w_pool_map[layer].astype(_BF), pscale=pool_scale[layer].reshape(1, POOL_W).astype(_F32),
        wba=w_br_attn[layer].astype(_BF), wbp=w_br_pool[layer].astype(_BF),
        wout=w_out[layer].astype(_BF), g2=ln2_g[layer].reshape(1, D_MODEL).astype(_F32),
        wr=wr.astype(_BF), br=br,
        ltri=(jnp.arange(ROW_TILE)[:, None] > jnp.arange(ROW_TILE)[None, :]).astype(_BF))

    pp = _project(x_prompt.reshape(n_p, D_MODEL), ln1_g[layer], w_in[layer], nb)
    attn_p = _prompt_attention(pp, rel_bias, nb, s)
    d_p = _pool_diff(pp["u"], nb, s, True)
    x1_p, xn_p, eid_p, gate_p, rank_p, cnt_p = _merge_route(
        x_prompt.reshape(n_p, D_MODEL), attn_p, d_p, pp["ga"], pp["gp"], mw)

    ps = _project(x_sample.reshape(n_s, D_MODEL), ln1_g[layer], w_in[layer], 1)
    q_s = ps["qT"].T.reshape(db, tn, ATTN_W)
    qi_s = ps["qiT"].T.reshape(db, tn, IDX_W)
    wi_s = ps["wiT"][:IDX_HEADS].T.reshape(db, tn, IDX_HEADS)
    k_s, v_s, ki_s = ps["kT32"][0].T, ps["vT32"][0].T, ps["kiT32"][0].T
    attn_s = _sample_attention(q_s, ps["k16"].reshape(db, tn, ATTN_W),
                               v_s.astype(_BF).reshape(db, tn, ATTN_W), qi_s,
                               ps["ki16"].reshape(db, tn, IDX_DIM), wi_s,
                               cache_k[layer], cache_v[layer], cache_kidx[layer], page_table, rel_bias)
    u_s = ps["u"].reshape(db, tn, POOL_W)
    buf = jnp.concatenate([jnp.zeros((db, 1, POOL_W), _F32), state_pool[layer].astype(_F32), u_s],
                          axis=1)
    grp_rows = 1 + POOL_CTX + tn
    d_s = _pool_diff(buf.reshape(db * grp_rows, POOL_W), 1, db * grp_rows, False)
    d_s = d_s.reshape(db, grp_rows, POOL_W)[:, 1 + POOL_CTX:].reshape(n_s, POOL_W)
    x1_s, xn_s, eid_s, gate_s, rank_s, cnt_s = _merge_route(
        x_sample.reshape(n_s, D_MODEL), attn_s.reshape(n_s, ATTN_W), d_s, ps["ga"], ps["gp"], mw)

    n_all = n_p + n_s
    eid = jnp.concatenate([eid_p[:, :TOP_K_FINE], eid_s[:, :TOP_K_FINE]], axis=0)
    rank = jnp.concatenate([rank_p[:, :TOP_K_FINE], rank_s[:, :TOP_K_FINE]], axis=0)
    gate = jnp.concatenate([gate_p, gate_s], axis=0)
    xn = jnp.concatenate([xn_p, xn_s], axis=0)
    x1 = jnp.concatenate([x1_p, x1_s], axis=0)
    dest, block_expert, n_used, nblk = _route_layout(eid, rank, jnp.concatenate([cnt_p, cnt_s], axis=0))
    tok = jnp.broadcast_to(jnp.arange(n_all, dtype=jnp.int32)[:, None], dest.shape)
    slot_tok = jnp.full((nblk * MOE_ROWS,), n_all, jnp.int32).at[dest.reshape(-1)].set(tok.reshape(-1))
    x_pad = jnp.concatenate([xn, jnp.zeros((1, D_MODEL), _BF)], axis=0)[slot_tok]
    yb = _experts(x_pad, block_expert, n_used, w_gate[layer].astype(_BF), w_up[layer].astype(_BF),
                  w_down[layer].astype(_BF))
    y = _combine_norm(x1, yb[dest[:, 0]], yb[dest[:, 1]], gate, lnf_g)

    y_prompt = y[:n_p].reshape(nb, s, D_MODEL)
    y_sample = y[n_p:].reshape(db, tn, D_MODEL)
    head = lambda a, n, t: a.reshape(1, n, t, N_HEADS, HEAD_DIM)
    heads_kminor = lambda a: jnp.transpose(a.reshape(nb, N_HEADS, HEAD_DIM, s), (0, 3, 1, 2))[None]
    return (y_prompt, y_sample,
            heads_kminor(pp["kT32"]), heads_kminor(pp["vT32"]), jnp.swapaxes(pp["kiT32"], 1, 2)[None],
            pp["u"].reshape(nb, s, POOL_W)[None, :, -POOL_CTX:],
            head(k_s, db, tn), head(v_s, db, tn), ki_s.reshape(1, db, tn, IDX_DIM),
            buf[None, :, -POOL_CTX:])
```

```python
import functools
import math

import jax
import jax.numpy as jnp
from jax import lax
from jax.experimental import pallas as pl
from jax.experimental.pallas import tpu as pltpu

D_MODEL = 1024
N_HEADS = 8
HEAD_DIM = 64
ATTN_W = N_HEADS * HEAD_DIM
IDX_HEADS = 4
IDX_DIM = 64
IDX_W = IDX_HEADS * IDX_DIM
TOPK_MAX = 256
PAGE_SIZE = 128
REL_BUCKETS = 32
REL_MAX_EXACT = 16
REL_MAX_DIST = 128
POOL_GROUPS = 4
POOL_GC = 128
POOL_W = POOL_GROUPS * POOL_GC
POOL_WINDOWS = (2, 4, 8, 16)
POOL_CTX = 15
N_GROUPS = 4
EXPERTS_PER_GROUP = 8
N_EXPERTS = N_GROUPS * EXPERTS_PER_GROUP
TOP_K_FINE = 2
D_EXPERT = 512
RMS_EPS = 1e-6

LANES = 128
SUBLANES = 8
HEAD_PAIR = 2 * HEAD_DIM
INT_MIN = -(2 ** 31)
NEG_BIG = -1e30
VMEM_LIMIT = 56 * 1024 * 1024
ROW_TILE = 512
Q_TILE = 128
K_TILE = 512
COUNT_ROWS = 64
SUM_ROWS = 16
PAGES_PER_STEP = 16
MOE_ROWS = 256
POOL_HALO = 16
FINE_LANE0 = 8

_NT = (((1,), (1,)), ((), ()))
_BF = jnp.bfloat16
_F32 = jnp.float32


def _resident(shape, index_map):
    return pl.BlockSpec(shape, index_map, pipeline_mode=pl.Buffered(1))


def _order_key(score):
    bits = pltpu.bitcast(score, jnp.int32)
    key = bits ^ ((bits >> 31) & 0x7FFFFFFF)
    return jnp.where(key == -1, 0, key)


def _bisect_threshold(count_ge, shape, topk, n_keys):
    def body(i, carry):
        lo, n_lo = carry
        cand = lo + jnp.left_shift(jnp.int32(1), 31 - i)
        n_cand = count_ge(cand)
        ok = n_cand >= topk
        return jnp.where(ok, cand, lo), jnp.where(ok, n_cand, n_lo)

    init = (jnp.full(shape, INT_MIN, jnp.int32), jnp.full(shape, n_keys, _F32))
    return lax.fori_loop(0, 32, body, init)


def _fold_rows(x, op, rows=SUBLANES):
    while x.shape[0] > rows:
        half = x.shape[0] // 2
        x = op(x[:half], x[half:])
    return x


def _rms_scale(x, g):
    ms = jnp.mean(x * x, axis=-1, keepdims=True)
    return x * lax.rsqrt(ms + RMS_EPS) * g


_ROW_SECTIONS = (("k", ATTN_W), ("u", POOL_W), ("ga", D_MODEL), ("gp", D_MODEL),
                 ("ki", LANES))
_COL_SECTIONS = (("q", ATTN_W), ("qi", IDX_W), ("v", ATTN_W), ("k", ATTN_W), ("ki", IDX_DIM),
                 ("wi", SUBLANES))


def _proj_kernel(x_ref, g_ref, wa_ref, wbt_ref,
                 u_ref, ga_ref, gp_ref, k16_ref, ki16_ref,
                 qt_ref, qit_ref, vt_ref, wit_ref, kt32_ref, vt32_ref, kit32_ref):
    h = _rms_scale(x_ref[...], g_ref[...]).astype(_BF)
    lo = 0
    for name, width in _ROW_SECTIONS:
        t = jnp.dot(h, wa_ref[:, lo:lo + width], preferred_element_type=_F32)
        lo += width
        if name == "k":
            k16_ref[...] = t.astype(_BF)
        elif name == "ki":
            ki16_ref[...] = t[:, :IDX_DIM].astype(_BF)
        else:
            {"u": u_ref, "ga": ga_ref, "gp": gp_ref}[name][...] = t
    lo = 0
    for name, width in _COL_SECTIONS:
        t = lax.dot_general(wbt_ref[lo:lo + width, :], h, _NT, preferred_element_type=_F32)
        lo += width
        if name == "q":
            qt_ref[...] = t.astype(_BF)
        elif name == "qi":
            qit_ref[...] = t.astype(_BF)
        elif name == "v":
            vt_ref[0] = t.astype(_BF)
            vt32_ref[0] = t
        elif name == "k":
            kt32_ref[0] = t
        elif name == "ki":
            kit32_ref[0] = t
        else:
            wit_ref[...] = t


def _project(x, ln_g, w_in, n_seq):
    n = x.shape[0]
    tm = ROW_TILE
    assert n % (tm * n_seq) == 0 and tm == K_TILE
    seq_tiles = n // (tm * n_seq)
    widths = (ATTN_W, ATTN_W, ATTN_W, IDX_W, IDX_DIM, IDX_HEADS, POOL_W, D_MODEL, D_MODEL)
    names = ("q", "k", "v", "qi", "ki", "wi", "u", "ga", "gp")
    cols, lo = {}, 0
    for name, width in zip(names, widths):
        cols[name] = w_in[:, lo:lo + width]
        lo += width
    ki_pad = jnp.pad(cols["ki"], ((0, 0), (0, LANES - IDX_DIM)))
    wa = jnp.concatenate([cols["k"], cols["u"], cols["ga"], cols["gp"], ki_pad], axis=1).astype(_BF)
    wi_pad = jnp.pad(cols["wi"], ((0, 0), (0, SUBLANES - IDX_HEADS)))
    wbt = jnp.concatenate([cols["q"] * HEAD_DIM ** -0.5, cols["qi"] * IDX_DIM ** -0.5, cols["v"],
                           cols["k"], cols["ki"], wi_pad], axis=1).T.astype(_BF)
    na, nb = wa.shape[1], wbt.shape[0]
    row = lambda w: pl.BlockSpec((tm, w), lambda i: (i, 0))
    colb = lambda h: pl.BlockSpec((h, tm), lambda i: (0, i))
    seqb = lambda h: pl.BlockSpec((1, h, tm), lambda i: (i // seq_tiles, 0, i % seq_tiles))
    seq_shape = lambda h: (n_seq, h, n // n_seq)
    out_shapes = dict(
        u=((n, POOL_W), _F32, row(POOL_W)), ga=((n, D_MODEL), _F32, row(D_MODEL)),
        gp=((n, D_MODEL), _F32, row(D_MODEL)),
        k16=((n, ATTN_W), _BF, row(ATTN_W)), ki16=((n, IDX_DIM), _BF, row(IDX_DIM)),
        qT=((ATTN_W, n), _BF, colb(ATTN_W)), qiT=((IDX_W, n), _BF, colb(IDX_W)),
        vT=((n // tm, ATTN_W, tm), _BF, pl.BlockSpec((1, ATTN_W, tm), lambda i: (i, 0, 0))),
        wiT=((SUBLANES, n), _F32, colb(SUBLANES)),
        kT32=(seq_shape(ATTN_W), _F32, seqb(ATTN_W)), vT32=(seq_shape(ATTN_W), _F32, seqb(ATTN_W)),
        kiT32=(seq_shape(IDX_DIM), _F32, seqb(IDX_DIM)),
    )
    keys = list(out_shapes)
    res = pl.pallas_call(
        _proj_kernel,
        grid=(n // tm,),
        in_specs=[row(D_MODEL), _resident((1, D_MODEL), lambda i: (0, 0)),
                  _resident((D_MODEL, na), lambda i: (0, 0)),
                  _resident((nb, D_MODEL), lambda i: (0, 0))],
        out_specs=[out_shapes[k][2] for k in keys],
        out_shape=[jax.ShapeDtypeStruct(out_shapes[k][0], out_shapes[k][1]) for k in keys],
        compiler_params=pltpu.CompilerParams(dimension_semantics=("arbitrary",),
                                             vmem_limit_bytes=VMEM_LIMIT),
        name="input_projection",
    )(x, ln_g.reshape(1, D_MODEL).astype(_F32), wa, wbt)
    return dict(zip(keys, res))


def _prompt_attn_kernel(qT_ref, qiT_ref, wiT_ref, k_ref, vT_ref, ki_ref, bias_ref, ltri_ref,
                        out_ref, keys_scr, mask_scr, qm_scr, m_scr, acc_scr, *, topk):
    TQ, TK = Q_TILE, K_TILE
    sub = TK // TQ
    qb = pl.program_id(1)
    n_sb = qb // sub + 1
    n_far = jnp.maximum(qb - 1, 0) // sub
    key_off = lax.broadcasted_iota(jnp.int32, (TK, TQ), 0)
    qry_off = lax.broadcasted_iota(jnp.int32, (TK, TQ), 1)

    def causal_at(sb):
        return (sb * TK + key_off) <= (qb * TQ + qry_off)

    qi_wide = jnp.concatenate([qiT_ref[h * IDX_DIM:(h + 1) * IDX_DIM, :] for h in range(IDX_HEADS)],
                              axis=1)
    wT = wiT_ref[...]

    def score_keys(sb, causal=None):
        kib = ki_ref[pl.ds(pl.multiple_of(sb * TK, TK), TK), :]
        d = jnp.dot(kib, qi_wide, preferred_element_type=_F32)
        sc = None
        for h in range(IDX_HEADS):
            term = wT[h:h + 1, :] * jnp.maximum(d[:, h * TQ:(h + 1) * TQ], 0.0)
            sc = term if sc is None else sc + term
        key = _order_key(sc)
        if causal is not None:
            key = jnp.where(causal, key, INT_MIN)
        keys_scr[sb] = key

    def score_body(sb, carry):
        score_keys(sb)
        return carry

    lax.fori_loop(0, n_sb - 1, score_body, 0)
    score_keys(n_sb - 1, causal_at(n_sb - 1))

    def count(pred):
        def body(sb, acc):
            hit = jnp.where(pred(keys_scr[sb]), 1.0, 0.0)
            return acc + jnp.sum(hit.reshape(TK // COUNT_ROWS, COUNT_ROWS, TQ), axis=0)
        acc = lax.fori_loop(0, n_sb, body, jnp.zeros((COUNT_ROWS, TQ), _F32))
        return jnp.sum(_fold_rows(acc, jnp.add), axis=0, keepdims=True)

    kf = float(topk)
    thr, n_ge = _bisect_threshold(lambda c: count(lambda kk: kk >= c), (1, TQ), kf,
                                  (n_sb * TK).astype(_F32))
    has_ties = jnp.max(jnp.where(n_ge != kf, 1.0, 0.0)) > 0.0
    n_tie_keep = kf - lax.cond(has_ties, lambda: count(lambda kk: kk > thr),
                               lambda: jnp.zeros((1, TQ), _F32))

    zeros_half = jnp.zeros((HEAD_DIM, TQ), _BF)
    for pair in range(N_HEADS // 2):
        q0 = qT_ref[(2 * pair) * HEAD_DIM:(2 * pair + 1) * HEAD_DIM, :]
        q1 = qT_ref[(2 * pair + 1) * HEAD_DIM:(2 * pair + 2) * HEAD_DIM, :]
        qm_scr[pair] = jnp.concatenate([jnp.concatenate([q0, zeros_half], axis=1),
                                        jnp.concatenate([zeros_half, q1], axis=1)], axis=0)
    m_scr[...] = jnp.full(m_scr.shape, NEG_BIG, _F32)
    acc_scr[...] = jnp.zeros(acc_scr.shape, _F32)
    ones_rows = jnp.ones((SUM_ROWS, TK), _BF)
    n_pairs = N_HEADS // 2

    def logits(sb, pair):
        kp = k_ref[pl.ds(pl.multiple_of(sb * TK, TK), TK),
                   pair * HEAD_PAIR:(pair + 1) * HEAD_PAIR]
        return jnp.dot(kp, qm_scr[pair], preferred_element_type=_F32)

    def attend_block(sb, tie_seen, near, ties):
        s_next = logits(sb, 0)
        keyb = keys_scr[sb]
        if ties:
            eq = keyb == thr
            eqf = jnp.where(eq, 1.0, 0.0)
            rank = tie_seen + jnp.dot(ltri_ref[...], eqf.astype(_BF), preferred_element_type=_F32)
            keep = (keyb > thr) | (eq & (rank < n_tie_keep))
            madd = jnp.where(keep, 0.0, NEG_BIG)
            tie_seen = tie_seen + jnp.sum(eqf, axis=0, keepdims=True)
        else:
            madd = jnp.where(keyb >= thr, 0.0, NEG_BIG)
        if near:
            madd = jnp.where(causal_at(sb), madd, NEG_BIG)
        mask_scr[...] = madd
        for pair in range(n_pairs):
            s2 = s_next
            if pair + 1 < n_pairs:
                s_next = logits(sb, pair + 1)
            vp = vT_ref[sb, pair * HEAD_PAIR:(pair + 1) * HEAD_PAIR, :]
            halves = []
            for odd in range(2):
                h = 2 * pair + odd
                s = s2[:, odd * TQ:(odd + 1) * TQ] + mask_scr[...]
                if near:
                    parts = []
                    for j in range(sub):
                        back = qb - (sb * sub + j)
                        w0 = (back == 0).astype(_F32)
                        w1 = (back == 1).astype(_F32)
                        parts.append(s[j * TQ:(j + 1) * TQ] + w0 * bias_ref[0, h] + w1 * bias_ref[1, h])
                    s = jnp.concatenate(parts, axis=0)
                halves.append(s)
            m_old = m_scr[pair]
            m_blk = jnp.concatenate(
                [jnp.max(_fold_rows(s, jnp.maximum), axis=0, keepdims=True) for s in halves], axis=1)
            m_new = jnp.maximum(m_old, m_blk)
            p2 = jnp.concatenate([jnp.exp(s - m_new[:, odd * TQ:(odd + 1) * TQ]).astype(_BF)
                                  for odd, s in enumerate(halves)], axis=1)
            alpha = jnp.exp(m_old - m_new)
            va = jnp.concatenate([vp, ones_rows], axis=0)
            acc_scr[pair] = alpha * acc_scr[pair] + jnp.dot(va, p2, preferred_element_type=_F32)
            m_scr[pair] = m_new
        return tie_seen

    def attend_all(ties):
        def run():
            seen = lax.fori_loop(0, n_far, lambda sb, c: attend_block(sb, c, False, ties),
                                 jnp.zeros((1, TQ), _F32))
            lax.fori_loop(n_far, n_sb, lambda sb, c: attend_block(sb, c, True, ties), seen)
        return run

    lax.cond(has_ties, attend_all(True), attend_all(False))

    for pair in range(N_HEADS // 2):
        a = acc_scr[pair]
        res = jnp.concatenate(
            [a[:HEAD_DIM, :TQ] / a[HEAD_PAIR:HEAD_PAIR + 1, :TQ],
             a[HEAD_DIM:HEAD_PAIR, TQ:] / a[HEAD_PAIR:HEAD_PAIR + 1, TQ:]], axis=0)
        out_ref[:, pair * HEAD_PAIR:(pair + 1) * HEAD_PAIR] = res.T.astype(out_ref.dtype)


def _rel_bias_by_distance(rel_bias, n):
    dist = jnp.arange(n, dtype=jnp.int32)
    nf = jnp.maximum(dist, 1).astype(_F32)
    large = REL_MAX_EXACT + (jnp.log(nf / REL_MAX_EXACT) / math.log(REL_MAX_DIST / REL_MAX_EXACT)
                             * (REL_BUCKETS - REL_MAX_EXACT)).astype(jnp.int32)
    large = jnp.minimum(large, REL_BUCKETS - 1)
    bucket = jnp.where(dist < REL_MAX_EXACT, dist, large)
    return (rel_bias[bucket] - rel_bias[REL_BUCKETS - 1][None, :]).astype(_F32)


def _prompt_attention(proj, rel_bias, nb, s):
    TQ, TK = Q_TILE, K_TILE
    assert s % TK == 0
    nq, nsb = s // TQ, s // TK
    topk = min(TOPK_MAX, s // 4)
    bd = _rel_bias_by_distance(rel_bias, 2 * TQ)
    key_off = jnp.arange(TQ)[:, None]
    qry_off = jnp.arange(TQ)[None, :]
    tiles = [jnp.transpose(bd[jnp.clip(back * TQ + qry_off - key_off, 0, 2 * TQ - 1)], (2, 0, 1))
             for back in range(2)]
    bias_tiles = jnp.stack(tiles)
    ltri = (jnp.arange(TK)[:, None] > jnp.arange(TK)[None, :]).astype(_BF)

    kern = functools.partial(_prompt_attn_kernel, topk=topk)
    return pl.pallas_call(
        kern,
        grid=(nb, nq),
        in_specs=[
            pl.BlockSpec((ATTN_W, TQ), lambda i, j: (0, i * nq + j)),
            pl.BlockSpec((IDX_W, TQ), lambda i, j: (0, i * nq + j)),
            pl.BlockSpec((SUBLANES, TQ), lambda i, j: (0, i * nq + j)),
            _resident((s, ATTN_W), lambda i, j: (i, 0)),
            _resident((nsb, ATTN_W, TK), lambda i, j: (i, 0, 0)),
            _resident((s, IDX_DIM), lambda i, j: (i, 0)),
            _resident((2, N_HEADS, TQ, TQ), lambda i, j: (0, 0, 0, 0)),
            _resident((TK, TK), lambda i, j: (0, 0)),
        ],
        out_specs=pl.BlockSpec((TQ, ATTN_W), lambda i, j: (i * nq + j, 0)),
        out_shape=jax.ShapeDtypeStruct((nb * s, ATTN_W), _BF),
        scratch_shapes=[
            pltpu.VMEM((nsb, TK, TQ), jnp.int32),
            pltpu.VMEM((TK, TQ), _F32),
            pltpu.VMEM((N_HEADS // 2, HEAD_PAIR, 2 * TQ), _BF),
            pltpu.VMEM((N_HEADS // 2, 1, 2 * TQ), _F32),
            pltpu.VMEM((N_HEADS // 2, HEAD_PAIR + SUM_ROWS, 2 * TQ), _F32),
        ],
        compiler_params=pltpu.CompilerParams(
            dimension_semantics=("arbitrary", "arbitrary"), vmem_limit_bytes=VMEM_LIMIT),
        name="prompt_sparse_attention",
    )(proj["qT"], proj["qiT"], proj["wiT"], proj["k16"], proj["vT"], proj["ki16"], bias_tiles, ltri)


def _sample_attn_kernel(pt_ref, qis_ref, wis_ref, kinew_ref, qbd_ref, knew_ref, vnew_ref,
                        *rest, n_pages, t_new, topk):
    del pt_ref
    G = PAGES_PER_STEP
    kidx_refs, k_refs, v_refs = rest[:G], rest[G:2 * G], rest[2 * G:3 * G]
    (biasl_ref, biasn_ref, utri_ref, out_ref, keys_scr, thr_scr, keep_scr,
     tie_scr, flag_scr, m_scr, l_scr, acc_scr) = rest[3 * G:]
    NP = n_pages
    n_steps = NP // G
    phase = pl.program_id(1)
    p = pl.program_id(2)

    def score_keys(kit):
        d = jnp.dot(qis_ref[0], kit, preferred_element_type=_F32)
        w = wis_ref[0]
        sc = None
        for h in range(IDX_HEADS):
            term = w[:, h:h + 1] * jnp.maximum(d[h * t_new:(h + 1) * t_new], 0.0)
            sc = term if sc is None else sc + term
        return _order_key(sc)

    @pl.when(phase == 0)
    def _():
        for j in range(G):
            keys_scr[p * G + j] = score_keys(kidx_refs[j][0].astype(_BF))

    @pl.when((phase == 0) & (p == n_steps - 1))
    def _():
        lane = lax.broadcasted_iota(jnp.int32, (t_new, PAGE_SIZE), 1)
        qrow = lax.broadcasted_iota(jnp.int32, (t_new, PAGE_SIZE), 0)
        keys_scr[NP] = jnp.where(lane <= qrow, score_keys(kinew_ref[0]), INT_MIN)

        def count(pred):
            past_hit = jnp.where(pred(keys_scr[0:NP]), 1.0, 0.0).reshape(NP * t_new, PAGE_SIZE)
            hit = _fold_rows(past_hit, jnp.add, rows=t_new)
            hit = hit + jnp.where(pred(keys_scr[NP]), 1.0, 0.0)
            return jnp.sum(hit, axis=1, keepdims=True)

        kf = float(topk)
        thr, n_ge = _bisect_threshold(lambda c: count(lambda kk: kk >= c), (t_new, 1), kf,
                                      float((NP + 1) * PAGE_SIZE))
        n_gt = count(lambda kk: kk > thr)
        thr_scr[...] = thr
        keep_scr[...] = kf - n_gt
        flag_scr[0] = (jnp.max(jnp.where(n_ge != kf, 1.0, 0.0)) > 0.0).astype(jnp.int32)
        tie_scr[...] = jnp.zeros(tie_scr.shape, _F32)
        m_scr[...] = jnp.full(m_scr.shape, NEG_BIG, _F32)
        l_scr[...] = jnp.zeros(l_scr.shape, _F32)
        acc_scr[...] = jnp.zeros(acc_scr.shape, _F32)

    def attend(keybs, kmats, vmats, bias):
        thr = thr_scr[...]

        def tie_sel():
            masks = []
            for keyb in keybs:
                eq = keyb == thr
                eqf = jnp.where(eq, 1.0, 0.0)
                rank = tie_scr[...] + jnp.dot(eqf.astype(_BF), utri_ref[...],
                                              preferred_element_type=_F32)
                keep = (keyb > thr) | (eq & (rank < keep_scr[...]))
                tie_scr[...] = tie_scr[...] + jnp.sum(eqf, axis=1, keepdims=True)
                masks.append(jnp.where(keep, 0.0, NEG_BIG))
            return jnp.concatenate(masks, axis=1)

        madd8 = lax.cond(flag_scr[0] != 0, tie_sel, lambda: jnp.concatenate(
            [jnp.where(keyb >= thr, 0.0, NEG_BIG) for keyb in keybs], axis=1))
        madd = jnp.concatenate([madd8] * N_HEADS, axis=0)
        s = jnp.concatenate([jnp.dot(qbd_ref[0], km, preferred_element_type=_F32) for km in kmats],
                            axis=1) + madd
        if bias is not None:
            s = s + bias
        m_old = m_scr[...]
        m_new = jnp.maximum(m_old, jnp.max(s, axis=1, keepdims=True))
        pr = jnp.exp(s - m_new)
        alpha = jnp.exp(m_old - m_new)
        l_scr[...] = alpha * l_scr[...] + jnp.sum(pr, axis=1, keepdims=True)
        prb = pr.astype(_BF)
        pv = None
        for j, vm in enumerate(vmats):
            t = lax.dot_general(prb[:, j * PAGE_SIZE:(j + 1) * PAGE_SIZE], vm, _NT,
                                preferred_element_type=_F32)
            pv = t if pv is None else pv + t
        acc_scr[...] = alpha * acc_scr[...] + pv
        m_scr[...] = m_new

    @pl.when(phase == 1)
    def _():
        bias = (p == n_steps - 1).astype(_F32) * biasl_ref[...]
        attend([keys_scr[p * G + j] for j in range(G)],
               [k_refs[j][0].astype(_BF) for j in range(G)],
               [v_refs[j][0].astype(_BF) for j in range(G)], bias)

    @pl.when((phase == 1) & (p == n_steps - 1))
    def _():
        attend([keys_scr[NP]], [knew_ref[0]], [vnew_ref[0]], biasn_ref[...])
        o = acc_scr[...] / l_scr[...]
        head_of_lane = lax.broadcasted_iota(jnp.int32, (t_new, ATTN_W), 1) // HEAD_DIM
        res = jnp.zeros((t_new, ATTN_W), _F32)
        for h in range(N_HEADS):
            res = res + jnp.where(head_of_lane == h, o[h * t_new:(h + 1) * t_new], 0.0)
        out_ref[0] = res.astype(out_ref.dtype)


def _sample_attention(q, k_new, v_new, qi, ki_new, wi, cache_k, cache_v, cache_kidx, page_table,
                      rel_bias):
    nb, t_new = q.shape[:2]
    n_pages = page_table.shape[1]
    n_phys = cache_k.shape[0]
    past = n_pages * PAGE_SIZE
    topk = min(TOPK_MAX, (past + t_new) // 4)
    rows = N_HEADS * t_new
    qis = jnp.swapaxes(qi.reshape(nb, t_new, IDX_HEADS, IDX_DIM), 1, 2)
    qis = qis.reshape(nb, IDX_HEADS * t_new, IDX_DIM)
    G = PAGES_PER_STEP
    assert n_pages % G == 0 and n_pages & (n_pages - 1) == 0
    n_steps = n_pages // G
    q4 = jnp.swapaxes(q.reshape(nb, t_new, N_HEADS, HEAD_DIM), 1, 2)
    eye = jnp.eye(N_HEADS, dtype=_BF)
    qbd = (q4[:, :, :, None, :] * eye[None, :, None, :, None]).reshape(nb, rows, ATTN_W)

    def keys_minor(a):
        return jnp.pad(jnp.swapaxes(a, 1, 2), ((0, 0), (0, 0), (0, PAGE_SIZE - t_new)))

    bd = _rel_bias_by_distance(rel_bias, 2 * PAGE_SIZE)
    qo = jnp.arange(t_new)[:, None]
    co = jnp.arange(PAGE_SIZE)[None, :]
    d_last = jnp.clip(PAGE_SIZE + qo - co, 0, 2 * PAGE_SIZE - 1)
    d_new = jnp.clip(qo - co, 0, 2 * PAGE_SIZE - 1)
    bias_last = jnp.transpose(bd[d_last], (2, 0, 1)).reshape(rows, PAGE_SIZE)
    bias_last = jnp.pad(bias_last, ((0, 0), ((G - 1) * PAGE_SIZE, 0)))
    bias_new = jnp.transpose(bd[d_new], (2, 0, 1)).reshape(rows, PAGE_SIZE)
    utri = (jnp.arange(PAGE_SIZE)[:, None] < jnp.arange(PAGE_SIZE)[None, :]).astype(_BF)

    kern = functools.partial(_sample_attn_kernel, n_pages=n_pages, t_new=t_new, topk=topk)
    seq = lambda i, ph, p, pt: (i, 0, 0)
    const2 = lambda i, ph, p, pt: (0, 0)

    def kidx_page(j):
        return lambda i, ph, p, pt: (
            jnp.where(ph == 0, pt[i * n_pages + p * G + j], pt[i * n_pages + n_pages - G + j]), 0, 0)

    def kv_page(j):
        return lambda i, ph, p, pt: (
            jnp.where(ph == 1, pt[i * n_pages + p * G + j], pt[i * n_pages + j]), 0, 0)

    grid_spec = pltpu.PrefetchScalarGridSpec(
        num_scalar_prefetch=1,
        grid=(nb, 2, n_steps),
        in_specs=[
            pl.BlockSpec((1, IDX_HEADS * t_new, IDX_DIM), seq),
            pl.BlockSpec((1, t_new, IDX_HEADS), seq),
            pl.BlockSpec((1, IDX_DIM, PAGE_SIZE), seq),
            pl.BlockSpec((1, rows, ATTN_W), seq),
            pl.BlockSpec((1, ATTN_W, PAGE_SIZE), seq),
            pl.BlockSpec((1, ATTN_W, PAGE_SIZE), seq),
        ] + [pl.BlockSpec((1, IDX_DIM, PAGE_SIZE), kidx_page(j)) for j in range(G)]
          + [pl.BlockSpec((1, ATTN_W, PAGE_SIZE), kv_page(j)) for j in range(G)]
          + [pl.BlockSpec((1, ATTN_W, PAGE_SIZE), kv_page(j)) for j in range(G)]
          + [
            pl.BlockSpec((rows, G * PAGE_SIZE), const2),
            pl.BlockSpec((rows, PAGE_SIZE), const2),
            pl.BlockSpec((PAGE_SIZE, PAGE_SIZE), const2),
        ],
        out_specs=pl.BlockSpec((1, t_new, ATTN_W), seq),
        scratch_shapes=[
            pltpu.VMEM((n_pages + 1, t_new, PAGE_SIZE), jnp.int32),
            pltpu.VMEM((t_new, 1), jnp.int32),
            pltpu.VMEM((t_new, 1), _F32),
            pltpu.VMEM((t_new, 1), _F32),
            pltpu.SMEM((1,), jnp.int32),
            pltpu.VMEM((rows, 1), _F32),
            pltpu.VMEM((rows, 1), _F32),
            pltpu.VMEM((rows, ATTN_W), _F32),
        ],
    )
    kidx_pages = jnp.swapaxes(cache_kidx, 1, 2)
    k_pages = jnp.transpose(cache_k, (0, 2, 3, 1)).reshape(n_phys, ATTN_W, PAGE_SIZE)
    v_pages = jnp.transpose(cache_v, (0, 2, 3, 1)).reshape(n_phys, ATTN_W, PAGE_SIZE)
    return pl.pallas_call(
        kern,
        grid_spec=grid_spec,
        out_shape=jax.ShapeDtypeStruct((nb, t_new, ATTN_W), _BF),
        compiler_params=pltpu.CompilerParams(
            dimension_semantics=("arbitrary", "arbitrary", "arbitrary"),
            vmem_limit_bytes=VMEM_LIMIT),
        name="sample_sparse_attention",
    )(page_table.reshape(-1).astype(jnp.int32), qis, wi.astype(_F32), keys_minor(ki_new), qbd,
      keys_minor(k_new), keys_minor(v_new),
      *([kidx_pages] * G), *([k_pages] * G), *([v_pages] * G), bias_last, bias_new, utri)


def _pool_diff_kernel(u_ref, halo_ref, d_ref, *, truncate_start):
    tm = u_ref.shape[0]
    i = pl.program_id(1)
    main = u_ref[...]
    halo = jnp.where(i == 0, 0.0, halo_ref[...])
    ext = jnp.concatenate([halo, main], axis=0)
    pos = i * tm + lax.broadcasted_iota(jnp.int32, (tm, POOL_GC), 0)
    for g, w in enumerate(POOL_WINDOWS):
        sl = slice(g * POOL_GC, (g + 1) * POOL_GC)
        acc = ext[:, sl]
        shift = 1
        while shift < w:
            acc = acc + pltpu.roll(acc, shift, 0)
            shift *= 2
        wsum = acc[POOL_HALO:]
        cnt = jnp.minimum(pos + 1, w).astype(_F32) if truncate_start else float(w)
        d_ref[:, sl] = (wsum / cnt - main[:, sl]).astype(d_ref.dtype)


def _pool_diff(u, n_seq, seq_rows, truncate_start):
    tm = min(ROW_TILE, seq_rows)
    assert seq_rows % tm == 0 and tm % POOL_HALO == 0
    nt = seq_rows // tm
    hb = tm // POOL_HALO
    kern = functools.partial(_pool_diff_kernel, truncate_start=truncate_start)
    return pl.pallas_call(
        kern,
        grid=(n_seq, nt),
        in_specs=[pl.BlockSpec((tm, POOL_W), lambda b, i: (b * nt + i, 0)),
                  pl.BlockSpec((POOL_HALO, POOL_W),
                               lambda b, i: (jnp.maximum((b * nt + i) * hb - 1, 0), 0))],
        out_specs=pl.BlockSpec((tm, POOL_W), lambda b, i: (b * nt + i, 0)),
        out_shape=jax.ShapeDtypeStruct(u.shape, _BF),
        compiler_params=pltpu.CompilerParams(dimension_semantics=("arbitrary", "arbitrary")),
        name="pool_window_diff",
    )(u, u)


def _merge_route_kernel(x_ref, attn_ref, d_ref, ga_ref, gp_ref, wmap_ref, pscale_ref, wba_ref,
                        wbp_ref, wout_ref, g2_ref, wr_ref, br_ref, ltri_ref,
                        x1_ref, xn_ref, eid_ref, gate_ref, rank_ref, cnt_ref):
    tm = x_ref.shape[0]
    d = d_ref[...]
    pooled = [jnp.dot(d[:, g * POOL_GC:(g + 1) * POOL_GC], wmap_ref[g], preferred_element_type=_F32)
              for g in range(POOL_GROUPS)]
    pool = (jnp.concatenate(pooled, axis=1) * pscale_ref[...]).astype(_BF)
    a = jnp.dot(attn_ref[...], wba_ref[...], preferred_element_type=_F32)
    pp = jnp.dot(pool, wbp_ref[...], preferred_element_type=_F32)
    m = jax.nn.sigmoid(ga_ref[...]) * a + jax.nn.sigmoid(gp_ref[...]) * pp
    x1 = x_ref[...] + jnp.dot(m.astype(_BF), wout_ref[...], preferred_element_type=_F32)
    x1_ref[...] = x1
    xn = _rms_scale(x1, g2_ref[...]).astype(_BF)
    xn_ref[...] = xn

    logit = jnp.dot(xn, wr_ref[...], preferred_element_type=_F32) + br_ref[...]
    lane = lax.broadcasted_iota(jnp.int32, (tm, LANES), 1).astype(_F32)
    far = float(LANES)

    def first_lane_of(hit):
        return jnp.min(jnp.where(hit, lane, far), axis=1, keepdims=True)

    lc = jnp.where(lane < N_GROUPS, logit, NEG_BIG)
    mc = jnp.max(lc, axis=1, keepdims=True)
    p_grp = 1.0 / jnp.sum(jnp.exp(lc - mc), axis=1, keepdims=True)
    grp = first_lane_of(lc == mc)
    lo = FINE_LANE0 + EXPERTS_PER_GROUP * grp
    in_grp = (lane >= lo) & (lane < lo + EXPERTS_PER_GROUP)
    lf = jnp.where(in_grp, logit, NEG_BIG)
    ef = jnp.exp(lf - jnp.max(lf, axis=1, keepdims=True))
    pf = jnp.where(in_grp, ef / jnp.sum(ef, axis=1, keepdims=True), -1.0)
    p1 = jnp.max(pf, axis=1, keepdims=True)
    l1 = first_lane_of(pf == p1)
    pf2 = jnp.where(lane == l1, -1.0, pf)
    p2 = jnp.max(pf2, axis=1, keepdims=True)
    l2 = first_lane_of(pf2 == p2)
    e1 = l1 - FINE_LANE0
    e2 = l2 - FINE_LANE0
    g1 = p_grp * p1 / (p1 + p2)
    g2 = p_grp * p2 / (p1 + p2)
    eid_ref[...] = jnp.where(lane == 0, e1, jnp.where(lane == 1, e2, 0.0)).astype(jnp.int32)
    gate_ref[...] = jnp.where(lane == 0, g1, jnp.where(lane == 1, g2, 0.0))

    onehot = jnp.where((lane == e1) | (lane == e2), 1.0, 0.0)
    before = jnp.dot(ltri_ref[...], onehot.astype(_BF), preferred_element_type=_F32)
    r1 = jnp.sum(jnp.where(lane == e1, before, 0.0), axis=1, keepdims=True)
    r2 = jnp.sum(jnp.where(lane == e2, before, 0.0), axis=1, keepdims=True)
    rank_ref[...] = jnp.where(lane == 0, r1, jnp.where(lane == 1, r2, 0.0)).astype(jnp.int32)
    cnt_ref[0] = jnp.broadcast_to(jnp.sum(onehot, axis=0, keepdims=True), (SUBLANES, LANES))


def _merge_route(x, attn, d, ga, gp, w):
    n = x.shape[0]
    tm = ROW_TILE
    assert n % tm == 0
    nt = n // tm
    row = lambda width: pl.BlockSpec((tm, width), lambda i: (i, 0))
    full = lambda a: _resident(a.shape, lambda i: (0,) * a.ndim)
    weights = (w["wmap"], w["pscale"], w["wba"], w["wbp"], w["wout"], w["g2"], w["wr"], w["br"],
               w["ltri"])
    return pl.pallas_call(
        _merge_route_kernel,
        grid=(nt,),
        in_specs=[row(D_MODEL), row(ATTN_W), row(POOL_W), row(D_MODEL), row(D_MODEL)]
                 + [full(a) for a in weights],
        out_specs=[row(D_MODEL), row(D_MODEL), row(LANES), row(LANES), row(LANES),
                   pl.BlockSpec((1, SUBLANES, LANES), lambda i: (i, 0, 0))],
        out_shape=[jax.ShapeDtypeStruct((n, D_MODEL), _F32), jax.ShapeDtypeStruct((n, D_MODEL), _BF),
                   jax.ShapeDtypeStruct((n, LANES), jnp.int32), jax.ShapeDtypeStruct((n, LANES), _F32),
                   jax.ShapeDtypeStruct((n, LANES), jnp.int32),
                   jax.ShapeDtypeStruct((nt, SUBLANES, LANES), _F32)],
        compiler_params=pltpu.CompilerParams(dimension_semantics=("arbitrary",),
                                             vmem_limit_bytes=VMEM_LIMIT),
        name="merge_norm_route",
    )(x, attn, d, ga, gp, *weights)


def _experts_kernel(be_ref, nu_ref, x_ref, wg_ref, wu_ref, wd_ref, y_ref):
    del be_ref
    i = pl.program_id(0)

    @pl.when(i < nu_ref[0])
    def _():
        x = x_ref[...]
        hg = jnp.dot(x, wg_ref[0], preferred_element_type=_F32)
        hu = jnp.dot(x, wu_ref[0], preferred_element_type=_F32)
        hdn = (hg * jax.nn.sigmoid(hg) * hu).astype(_BF)
        y_ref[...] = jnp.dot(hdn, wd_ref[0], preferred_element_type=_F32).astype(y_ref.dtype)

    @pl.when(i >= nu_ref[0])
    def _():
        y_ref[...] = jnp.zeros(y_ref.shape, y_ref.dtype)


def _experts(x_pad, block_expert, n_used, wg, wu, wd):
    nblk = block_expert.shape[0]
    bm = MOE_ROWS
    used = lambda i, be, nu: (jnp.minimum(i, jnp.maximum(nu[0] - 1, 0)), 0)
    grid_spec = pltpu.PrefetchScalarGridSpec(
        num_scalar_prefetch=2,
        grid=(nblk,),
        in_specs=[pl.BlockSpec((bm, D_MODEL), used),
                  pl.BlockSpec((1, D_MODEL, D_EXPERT), lambda i, be, nu: (be[i], 0, 0)),
                  pl.BlockSpec((1, D_MODEL, D_EXPERT), lambda i, be, nu: (be[i], 0, 0)),
                  pl.BlockSpec((1, D_EXPERT, D_MODEL), lambda i, be, nu: (be[i], 0, 0))],
        out_specs=pl.BlockSpec((bm, D_MODEL), lambda i, be, nu: (i, 0)),
    )
    return pl.pallas_call(
        _experts_kernel,
        grid_spec=grid_spec,
        out_shape=jax.ShapeDtypeStruct((nblk * bm, D_MODEL), _BF),
        compiler_params=pltpu.CompilerParams(dimension_semantics=("arbitrary",),
                                             vmem_limit_bytes=VMEM_LIMIT),
        name="grouped_swiglu_experts",
    )(block_expert, n_used, x_pad, wg, wu, wd)


def _route_layout(eid, rank, tile_cnt):
    n = eid.shape[0]
    bm = MOE_ROWS
    cnt = tile_cnt[:, 0, :N_EXPERTS].astype(jnp.int32)
    tile_start = jnp.cumsum(cnt, axis=0) - cnt
    counts = jnp.sum(cnt, axis=0)
    padded = ((counts + bm - 1) // bm) * bm
    pad_ends = jnp.cumsum(padded)
    pad_starts = pad_ends - padded
    base = jnp.repeat(tile_start + pad_starts[None, :], ROW_TILE, axis=0)
    onehot = eid[:, :, None] == jnp.arange(N_EXPERTS, dtype=jnp.int32)[None, None, :]
    dest = jnp.sum(jnp.where(onehot, base[:, None, :], 0), axis=-1) + rank
    nblk = (n * TOP_K_FINE) // bm + N_EXPERTS
    block_row0 = jnp.arange(nblk, dtype=jnp.int32) * bm
    block_expert = jnp.minimum(jnp.sum(pad_ends[None, :] <= block_row0[:, None], axis=1),
                               N_EXPERTS - 1).astype(jnp.int32)
    n_used = (pad_ends[-1] // bm).astype(jnp.int32).reshape(1)
    return dest, block_expert, n_used, nblk


def _combine_norm_kernel(x1_ref, y1_ref, y2_ref, gate_ref, g_ref, out_ref):
    gate = gate_ref[...]
    x2 = (x1_ref[...] + gate[:, 0:1] * y1_ref[...].astype(_F32)
          + gate[:, 1:2] * y2_ref[...].astype(_F32))
    out_ref[...] = _rms_scale(x2, g_ref[...])


def _combine_norm(x1, y1, y2, gate, g):
    n = x1.shape[0]
    tm = ROW_TILE
    row = lambda width: pl.BlockSpec((tm, width), lambda i: (i, 0))
    return pl.pallas_call(
        _combine_norm_kernel,
        grid=(n // tm,),
        in_specs=[row(D_MODEL), row(D_MODEL), row(D_MODEL), row(LANES),
                  _resident((1, D_MODEL), lambda i: (0, 0))],
        out_specs=row(D_MODEL),
        out_shape=jax.ShapeDtypeStruct((n, D_MODEL), _F32),
        compiler_params=pltpu.CompilerParams(dimension_semantics=("arbitrary",)),
        name="combine_final_norm",
    )(x1, y1, y2, gate, g.reshape(1, D_MODEL).astype(_F32))


def kernel(x_prompt, x_sample, cache_k, cache_v, cache_kidx, state_pool, page_table, rel_bias,
           ln1_g, w_in, w_pool_map, pool_scale, w_br_attn, w_br_pool, w_out, ln2_g,
           w_coarse, b_coarse, w_fine, b_fine, w_gate, w_up, w_down, lnf_g):
    layer = 0
    nb, s = x_prompt.shape[:2]
    db, tn = x_sample.shape[:2]
    n_p, n_s = nb * s, db * tn
    rel_bias = rel_bias.astype(_F32)

    wr = jnp.zeros((D_MODEL, LANES), _F32)
    wr = wr.at[:, :N_GROUPS].set(w_coarse[layer])
    wr = wr.at[:, FINE_LANE0:FINE_LANE0 + N_EXPERTS].set(
        jnp.transpose(w_fine[layer], (1, 0, 2)).reshape(D_MODEL, N_EXPERTS))
    br = jnp.zeros((1, LANES), _F32)
    br = br.at[0, :N_GROUPS].set(b_coarse[layer])
    br = br.at[0, FINE_LANE0:FINE_LANE0 + N_EXPERTS].set(b_fine[layer].reshape(-1))
    mw = dict(
        wmap=w_pool_map[layer].astype(_BF), pscale=pool_scale[layer].reshape(1, POOL_W).astype(_F32),
        wba=w_br_attn[layer].astype(_BF), wbp=w_br_pool[layer].astype(_BF),
        wout=w_out[layer].astype(_BF), g2=ln2_g[layer].reshape(1, D_MODEL).astype(_F32),
        wr=wr.astype(_BF), br=br,
        ltri=(jnp.arange(ROW_TILE)[:, None] > jnp.arange(ROW_TILE)[None, :]).astype(_BF))

    pp = _project(x_prompt.reshape(n_p, D_MODEL), ln1_g[layer], w_in[layer], nb)
    attn_p = _prompt_attention(pp, rel_bias, nb, s)
    d_p = _pool_diff(pp["u"], nb, s, True)
    x1_p, xn_p, eid_p, gate_p, rank_p, cnt_p = _merge_route(
        x_prompt.reshape(n_p, D_MODEL), attn_p, d_p, pp["ga"], pp["gp"], mw)

    ps = _project(x_sample.reshape(n_s, D_MODEL), ln1_g[layer], w_in[layer], 1)
    q_s = ps["qT"].T.reshape(db, tn, ATTN_W)
    qi_s = ps["qiT"].T.reshape(db, tn, IDX_W)
    wi_s = ps["wiT"][:IDX_HEADS].T.reshape(db, tn, IDX_HEADS)
    k_s, v_s, ki_s = ps["kT32"][0].T, ps["vT32"][0].T, ps["kiT32"][0].T
    attn_s = _sample_attention(q_s, ps["k16"].reshape(db, tn, ATTN_W),
                               v_s.astype(_BF).reshape(db, tn, ATTN_W), qi_s,
                               ps["ki16"].reshape(db, tn, IDX_DIM), wi_s,
                               cache_k[layer], cache_v[layer], cache_kidx[layer], page_table, rel_bias)
    u_s = ps["u"].reshape(db, tn, POOL_W)
    buf = jnp.concatenate([jnp.zeros((db, 1, POOL_W), _F32), state_pool[layer].astype(_F32), u_s],
                          axis=1)
    grp_rows = 1 + POOL_CTX + tn
    d_s = _pool_diff(buf.reshape(db * grp_rows, POOL_W), 1, db * grp_rows, False)
    d_s = d_s.reshape(db, grp_rows, POOL_W)[:, 1 + POOL_CTX:].reshape(n_s, POOL_W)
    x1_s, xn_s, eid_s, gate_s, rank_s, cnt_s = _merge_route(
        x_sample.reshape(n_s, D_MODEL), attn_s.reshape(n_s, ATTN_W), d_s, ps["ga"], ps["gp"], mw)

    n_all = n_p + n_s
    eid = jnp.concatenate([eid_p[:, :TOP_K_FINE], eid_s[:, :TOP_K_FINE]], axis=0)
    rank = jnp.concatenate([rank_p[:, :TOP_K_FINE], rank_s[:, :TOP_K_FINE]], axis=0)
    dest, block_expert, n_used, nblk = _route_layout(eid, rank, jnp.concatenate([cnt_p, cnt_s], axis=0))
    tok = jnp.broadcast_to(jnp.arange(n_all, dtype=jnp.int32)[:, None], dest.shape)
    slot_tok = jnp.full((nblk * MOE_ROWS,), n_all, jnp.int32).at[dest.reshape(-1)].set(tok.reshape(-1))
    x_pad = jnp.concatenate([xn_p, xn_s, jnp.zeros((1, D_MODEL), _BF)], axis=0)[slot_tok]
    yb = _experts(x_pad, block_expert, n_used, w_gate[layer].astype(_BF), w_up[layer].astype(_BF),
                  w_down[layer].astype(_BF))
    dest_p, dest_s = dest[:n_p], dest[n_p:]
    y_prompt = _combine_norm(x1_p, yb[dest_p[:, 0]], yb[dest_p[:, 1]], gate_p, lnf_g)
    y_sample = _combine_norm(x1_s, yb[dest_s[:, 0]], yb[dest_s[:, 1]], gate_s, lnf_g)
    y_prompt = y_prompt.reshape(nb, s, D_MODEL)
    y_sample = y_sample.reshape(db, tn, D_MODEL)
    head = lambda a, n, t: a.reshape(1, n, t, N_HEADS, HEAD_DIM)
    heads_kminor = lambda a: jnp.transpose(a.reshape(nb, N_HEADS, HEAD_DIM, s), (0, 3, 1, 2))[None]
    return (y_prompt, y_sample,
            heads_kminor(pp["kT32"]), heads_kminor(pp["vT32"]), jnp.swapaxes(pp["kiT32"], 1, 2)[None],
            pp["u"].reshape(nb, s, POOL_W)[None, :, -POOL_CTX:],
            head(k_s, db, tn), head(v_s, db, tn), ki_s.reshape(1, db, tn, IDX_DIM),
            buf[None, :, -POOL_CTX:])
```

```python
import functools
import math

import jax
import jax.numpy as jnp
from jax import lax
from jax.experimental import pallas as pl
from jax.experimental.pallas import tpu as pltpu

D_MODEL = 1024
N_HEADS = 8
HEAD_DIM = 64
ATTN_W = N_HEADS * HEAD_DIM
IDX_HEADS = 4
IDX_DIM = 64
IDX_W = IDX_HEADS * IDX_DIM
TOPK_MAX = 256
PAGE_SIZE = 128
REL_BUCKETS = 32
REL_MAX_EXACT = 16
REL_MAX_DIST = 128
POOL_GROUPS = 4
POOL_GC = 128
POOL_W = POOL_GROUPS * POOL_GC
POOL_WINDOWS = (2, 4, 8, 16)
POOL_CTX = 15
N_GROUPS = 4
EXPERTS_PER_GROUP = 8
N_EXPERTS = N_GROUPS * EXPERTS_PER_GROUP
TOP_K_FINE = 2
D_EXPERT = 512
RMS_EPS = 1e-6

LANES = 128
SUBLANES = 8
HEAD_PAIR = 2 * HEAD_DIM
INT_MIN = -(2 ** 31)
NEG_BIG = -1e30
VMEM_LIMIT = 56 * 1024 * 1024
ROW_TILE = 512
Q_TILE = 128
K_TILE = 512
FAR_UNROLL = 2
COUNT_ROWS = 64
SUM_ROWS = 16
PAGES_PER_STEP = 32
MOE_ROWS = 512
POOL_HALO = 16
FINE_LANE0 = 8

_NT = (((1,), (1,)), ((), ()))
_BF = jnp.bfloat16
_F32 = jnp.float32


def _resident(shape, index_map):
    return pl.BlockSpec(shape, index_map, pipeline_mode=pl.Buffered(1))


def _order_key(score):
    bits = pltpu.bitcast(score, jnp.int32)
    key = bits ^ ((bits >> 31) & 0x7FFFFFFF)
    return jnp.where(key == -1, 0, key)


def _bisect_threshold(count_ge, shape, topk, n_keys):
    def body(i, carry):
        lo, n_lo = carry
        cand = lo + jnp.left_shift(jnp.int32(1), 31 - i)
        n_cand = count_ge(cand)
        ok = n_cand >= topk
        return jnp.where(ok, cand, lo), jnp.where(ok, n_cand, n_lo)

    init = (jnp.full(shape, INT_MIN, jnp.int32), jnp.full(shape, n_keys, _F32))
    return lax.fori_loop(0, 32, body, init)


def _fold_rows(x, op, rows=SUBLANES):
    while x.shape[0] > rows:
        half = x.shape[0] // 2
        x = op(x[:half], x[half:])
    return x


def _rms_scale(x, g):
    ms = jnp.mean(x * x, axis=-1, keepdims=True)
    return x * lax.rsqrt(ms + RMS_EPS) * g


_ROW_SECTIONS = (("k", ATTN_W), ("u", POOL_W), ("ga", D_MODEL), ("gp", D_MODEL),
                 ("ki", LANES))
_COL_SECTIONS = (("q", ATTN_W), ("qi", IDX_W), ("v", ATTN_W), ("k", ATTN_W), ("ki", IDX_DIM),
                 ("wi", SUBLANES))


def _proj_kernel(x_ref, g_ref, wa_ref, wbt_ref,
                 u_ref, ga_ref, gp_ref, k16_ref, ki16_ref,
                 qt_ref, qit_ref, vt_ref, wit_ref, kt32_ref, vt32_ref, kit32_ref):
    h = _rms_scale(x_ref[...], g_ref[...]).astype(_BF)
    lo = 0
    for name, width in _ROW_SECTIONS:
        t = jnp.dot(h, wa_ref[:, lo:lo + width], preferred_element_type=_F32)
        lo += width
        if name == "k":
            k16_ref[...] = t.astype(_BF)
        elif name == "ki":
            ki16_ref[...] = t[:, :IDX_DIM].astype(_BF)
        else:
            {"u": u_ref, "ga": ga_ref, "gp": gp_ref}[name][...] = t
    lo = 0
    for name, width in _COL_SECTIONS:
        t = lax.dot_general(wbt_ref[lo:lo + width, :], h, _NT, preferred_element_type=_F32)
        lo += width
        if name == "q":
            qt_ref[...] = t.astype(_BF)
        elif name == "qi":
            qit_ref[...] = t.astype(_BF)
        elif name == "v":
            vt_ref[0] = t.astype(_BF)
            vt32_ref[0] = t
        elif name == "k":
            kt32_ref[0] = t
        elif name == "ki":
            kit32_ref[0] = t
        else:
            wit_ref[...] = t


def _project(x, ln_g, w_in, n_seq):
    n = x.shape[0]
    tm = ROW_TILE
    assert n % (tm * n_seq) == 0 and tm == K_TILE
    seq_tiles = n // (tm * n_seq)
    widths = (ATTN_W, ATTN_W, ATTN_W, IDX_W, IDX_DIM, IDX_HEADS, POOL_W, D_MODEL, D_MODEL)
    names = ("q", "k", "v", "qi", "ki", "wi", "u", "ga", "gp")
    cols, lo = {}, 0
    for name, width in zip(names, widths):
        cols[name] = w_in[:, lo:lo + width]
        lo += width
    ki_pad = jnp.pad(cols["ki"], ((0, 0), (0, LANES - IDX_DIM)))
    wa = jnp.concatenate([cols["k"], cols["u"], cols["ga"], cols["gp"], ki_pad], axis=1).astype(_BF)
    wi_pad = jnp.pad(cols["wi"], ((0, 0), (0, SUBLANES - IDX_HEADS)))
    wbt = jnp.concatenate([cols["q"] * HEAD_DIM ** -0.5, cols["qi"] * IDX_DIM ** -0.5, cols["v"],
                           cols["k"], cols["ki"], wi_pad], axis=1).T.astype(_BF)
    na, nb = wa.shape[1], wbt.shape[0]
    row = lambda w: pl.BlockSpec((tm, w), lambda i: (i, 0))
    colb = lambda h: pl.BlockSpec((h, tm), lambda i: (0, i))
    seqb = lambda h: pl.BlockSpec((1, h, tm), lambda i: (i // seq_tiles, 0, i % seq_tiles))
    seq_shape = lambda h: (n_seq, h, n // n_seq)
    out_shapes = dict(
        u=((n, POOL_W), _F32, row(POOL_W)), ga=((n, D_MODEL), _F32, row(D_MODEL)),
        gp=((n, D_MODEL), _F32, row(D_MODEL)),
        k16=((n, ATTN_W), _BF, row(ATTN_W)), ki16=((n, IDX_DIM), _BF, row(IDX_DIM)),
        qT=((ATTN_W, n), _BF, colb(ATTN_W)), qiT=((IDX_W, n), _BF, colb(IDX_W)),
        vT=((n // tm, ATTN_W, tm), _BF, pl.BlockSpec((1, ATTN_W, tm), lambda i: (i, 0, 0))),
        wiT=((SUBLANES, n), _F32, colb(SUBLANES)),
        kT32=(seq_shape(ATTN_W), _F32, seqb(ATTN_W)), vT32=(seq_shape(ATTN_W), _F32, seqb(ATTN_W)),
        kiT32=(seq_shape(IDX_DIM), _F32, seqb(IDX_DIM)),
    )
    keys = list(out_shapes)
    res = pl.pallas_call(
        _proj_kernel,
        grid=(n // tm,),
        in_specs=[row(D_MODEL), _resident((1, D_MODEL), lambda i: (0, 0)),
                  _resident((D_MODEL, na), lambda i: (0, 0)),
                  _resident((nb, D_MODEL), lambda i: (0, 0))],
        out_specs=[out_shapes[k][2] for k in keys],
        out_shape=[jax.ShapeDtypeStruct(out_shapes[k][0], out_shapes[k][1]) for k in keys],
        compiler_params=pltpu.CompilerParams(dimension_semantics=("arbitrary",),
                                             vmem_limit_bytes=VMEM_LIMIT),
        name="input_projection",
    )(x, ln_g.reshape(1, D_MODEL).astype(_F32), wa, wbt)
    return dict(zip(keys, res))


def _prompt_attn_kernel(qT_ref, qiT_ref, wiT_ref, k_ref, vT_ref, ki_ref, bias_ref, ltri_ref,
                        out_ref, keys_scr, mask_scr, qm_scr, m_scr, acc_scr, *, topk):
    TQ, TK = Q_TILE, K_TILE
    sub = TK // TQ
    qb = pl.program_id(1)
    n_sb = qb // sub + 1
    n_far = jnp.maximum(qb - 1, 0) // sub
    key_off = lax.broadcasted_iota(jnp.int32, (TK, TQ), 0)
    qry_off = lax.broadcasted_iota(jnp.int32, (TK, TQ), 1)

    def causal_at(sb):
        return (sb * TK + key_off) <= (qb * TQ + qry_off)

    qi_wide = jnp.concatenate([qiT_ref[h * IDX_DIM:(h + 1) * IDX_DIM, :] for h in range(IDX_HEADS)],
                              axis=1)
    wT = wiT_ref[...]

    def score_keys(sb, causal=None):
        kib = ki_ref[pl.ds(pl.multiple_of(sb * TK, TK), TK), :]
        d = jnp.dot(kib, qi_wide, preferred_element_type=_F32)
        sc = None
        for h in range(IDX_HEADS):
            term = wT[h:h + 1, :] * jnp.maximum(d[:, h * TQ:(h + 1) * TQ], 0.0)
            sc = term if sc is None else sc + term
        key = _order_key(sc)
        if causal is not None:
            key = jnp.where(causal, key, INT_MIN)
        keys_scr[sb] = key

    def score_body(sb, carry):
        score_keys(sb)
        return carry

    lax.fori_loop(0, n_sb - 1, score_body, 0)
    score_keys(n_sb - 1, causal_at(n_sb - 1))

    def count(pred):
        def body(sb, acc):
            hit = jnp.where(pred(keys_scr[sb]), 1.0, 0.0)
            return acc + jnp.sum(hit.reshape(TK // COUNT_ROWS, COUNT_ROWS, TQ), axis=0)
        acc = lax.fori_loop(0, n_sb, body, jnp.zeros((COUNT_ROWS, TQ), _F32))
        return jnp.sum(_fold_rows(acc, jnp.add), axis=0, keepdims=True)

    kf = float(topk)
    thr, n_ge = _bisect_threshold(lambda c: count(lambda kk: kk >= c), (1, TQ), kf,
                                  (n_sb * TK).astype(_F32))
    has_ties = jnp.max(jnp.where(n_ge != kf, 1.0, 0.0)) > 0.0
    n_tie_keep = kf - lax.cond(has_ties, lambda: count(lambda kk: kk > thr),
                               lambda: jnp.zeros((1, TQ), _F32))

    zeros_half = jnp.zeros((HEAD_DIM, TQ), _BF)
    for pair in range(N_HEADS // 2):
        q0 = qT_ref[(2 * pair) * HEAD_DIM:(2 * pair + 1) * HEAD_DIM, :]
        q1 = qT_ref[(2 * pair + 1) * HEAD_DIM:(2 * pair + 2) * HEAD_DIM, :]
        qm_scr[pair] = jnp.concatenate([jnp.concatenate([q0, zeros_half], axis=1),
                                        jnp.concatenate([zeros_half, q1], axis=1)], axis=0)
    m_scr[...] = jnp.full(m_scr.shape, NEG_BIG, _F32)
    acc_scr[...] = jnp.zeros(acc_scr.shape, _F32)
    ones_rows = jnp.ones((SUM_ROWS, TK), _BF)
    n_pairs = N_HEADS // 2

    def logits(sb, pair):
        kp = k_ref[pl.ds(pl.multiple_of(sb * TK, TK), TK),
                   pair * HEAD_PAIR:(pair + 1) * HEAD_PAIR]
        return jnp.dot(kp, qm_scr[pair], preferred_element_type=_F32)

    def attend_blocks(sbs, tie_seen, near, ties):
        stages = [(slot, sb, pair) for slot, sb in enumerate(sbs) for pair in range(n_pairs)]
        s_next = logits(sbs[0], 0)
        for slot, sb in enumerate(sbs):
            keyb = keys_scr[sb]
            if ties:
                eq = keyb == thr
                eqf = jnp.where(eq, 1.0, 0.0)
                rank = tie_seen + jnp.dot(ltri_ref[...], eqf.astype(_BF), preferred_element_type=_F32)
                keep = (keyb > thr) | (eq & (rank < n_tie_keep))
                madd = jnp.where(keep, 0.0, NEG_BIG)
                tie_seen = tie_seen + jnp.sum(eqf, axis=0, keepdims=True)
            else:
                madd = jnp.where(keyb >= thr, 0.0, NEG_BIG)
            if near:
                madd = jnp.where(causal_at(sb), madd, NEG_BIG)
            mask_scr[slot] = madd
        for k, (slot, sb, pair) in enumerate(stages):
            s2 = s_next
            if k + 1 < len(stages):
                s_next = logits(stages[k + 1][1], stages[k + 1][2])
            vp = vT_ref[sb, pair * HEAD_PAIR:(pair + 1) * HEAD_PAIR, :]
            halves = []
            for odd in range(2):
                h = 2 * pair + odd
                s = s2[:, odd * TQ:(odd + 1) * TQ] + mask_scr[slot]
                if near:
                    parts = []
                    for j in range(sub):
                        back = qb - (sb * sub + j)
                        w0 = (back == 0).astype(_F32)
                        w1 = (back == 1).astype(_F32)
                        parts.append(s[j * TQ:(j + 1) * TQ] + w0 * bias_ref[0, h] + w1 * bias_ref[1, h])
                    s = jnp.concatenate(parts, axis=0)
                halves.append(s)
            m_old = m_scr[pair]
            m_blk = jnp.concatenate(
                [jnp.max(_fold_rows(s, jnp.maximum), axis=0, keepdims=True) for s in halves], axis=1)
            m_new = jnp.maximum(m_old, m_blk)
            p2 = jnp.concatenate([jnp.exp(s - m_new[:, odd * TQ:(odd + 1) * TQ]).astype(_BF)
                                  for odd, s in enumerate(halves)], axis=1)
            alpha = jnp.exp(m_old - m_new)
            va = jnp.concatenate([vp, ones_rows], axis=0)
            acc_scr[pair] = alpha * acc_scr[pair] + jnp.dot(va, p2, preferred_element_type=_F32)
            m_scr[pair] = m_new
        return tie_seen

    def attend_all(ties):
        def run():
            seen = lax.fori_loop(
                0, n_far // FAR_UNROLL,
                lambda i, c: attend_blocks([FAR_UNROLL * i + u for u in range(FAR_UNROLL)], c, False, ties),
                jnp.zeros((1, TQ), _F32))
            seen = lax.fori_loop(n_far - n_far % FAR_UNROLL, n_far,
                                 lambda sb, c: attend_blocks([sb], c, False, ties), seen)
            lax.fori_loop(n_far, n_sb, lambda sb, c: attend_blocks([sb], c, True, ties), seen)
        return run

    lax.cond(has_ties, attend_all(True), attend_all(False))

    for pair in range(N_HEADS // 2):
        a = acc_scr[pair]
        res = jnp.concatenate(
            [a[:HEAD_DIM, :TQ] / a[HEAD_PAIR:HEAD_PAIR + 1, :TQ],
             a[HEAD_DIM:HEAD_PAIR, TQ:] / a[HEAD_PAIR:HEAD_PAIR + 1, TQ:]], axis=0)
        out_ref[:, pair * HEAD_PAIR:(pair + 1) * HEAD_PAIR] = res.T.astype(out_ref.dtype)


def _rel_bias_by_distance(rel_bias, n):
    dist = jnp.arange(n, dtype=jnp.int32)
    nf = jnp.maximum(dist, 1).astype(_F32)
    large = REL_MAX_EXACT + (jnp.log(nf / REL_MAX_EXACT) / math.log(REL_MAX_DIST / REL_MAX_EXACT)
                             * (REL_BUCKETS - REL_MAX_EXACT)).astype(jnp.int32)
    large = jnp.minimum(large, REL_BUCKETS - 1)
    bucket = jnp.where(dist < REL_MAX_EXACT, dist, large)
    return (rel_bias[bucket] - rel_bias[REL_BUCKETS - 1][None, :]).astype(_F32)


def _prompt_attention(proj, rel_bias, nb, s):
    TQ, TK = Q_TILE, K_TILE
    assert s % TK == 0
    nq, nsb = s // TQ, s // TK
    topk = min(TOPK_MAX, s // 4)
    bd = _rel_bias_by_distance(rel_bias, 2 * TQ)
    key_off = jnp.arange(TQ)[:, None]
    qry_off = jnp.arange(TQ)[None, :]
    tiles = [jnp.transpose(bd[jnp.clip(back * TQ + qry_off - key_off, 0, 2 * TQ - 1)], (2, 0, 1))
             for back in range(2)]
    bias_tiles = jnp.stack(tiles)
    ltri = (jnp.arange(TK)[:, None] > jnp.arange(TK)[None, :]).astype(_BF)

    kern = functools.partial(_prompt_attn_kernel, topk=topk)
    return pl.pallas_call(
        kern,
        grid=(nb, nq),
        in_specs=[
            pl.BlockSpec((ATTN_W, TQ), lambda i, j: (0, i * nq + j)),
            pl.BlockSpec((IDX_W, TQ), lambda i, j: (0, i * nq + j)),
            pl.BlockSpec((SUBLANES, TQ), lambda i, j: (0, i * nq + j)),
            _resident((s, ATTN_W), lambda i, j: (i, 0)),
            _resident((nsb, ATTN_W, TK), lambda i, j: (i, 0, 0)),
            _resident((s, IDX_DIM), lambda i, j: (i, 0)),
            _resident((2, N_HEADS, TQ, TQ), lambda i, j: (0, 0, 0, 0)),
            _resident((TK, TK), lambda i, j: (0, 0)),
        ],
        out_specs=pl.BlockSpec((TQ, ATTN_W), lambda i, j: (i * nq + j, 0)),
        out_shape=jax.ShapeDtypeStruct((nb * s, ATTN_W), _BF),
        scratch_shapes=[
            pltpu.VMEM((nsb, TK, TQ), jnp.int32),
            pltpu.VMEM((FAR_UNROLL, TK, TQ), _F32),
            pltpu.VMEM((N_HEADS // 2, HEAD_PAIR, 2 * TQ), _BF),
            pltpu.VMEM((N_HEADS // 2, 1, 2 * TQ), _F32),
            pltpu.VMEM((N_HEADS // 2, HEAD_PAIR + SUM_ROWS, 2 * TQ), _F32),
        ],
        compiler_params=pltpu.CompilerParams(
            dimension_semantics=("arbitrary", "arbitrary"), vmem_limit_bytes=VMEM_LIMIT),
        name="prompt_sparse_attention",
    )(proj["qT"], proj["qiT"], proj["wiT"], proj["k16"], proj["vT"], proj["ki16"], bias_tiles, ltri)


def _sample_attn_kernel(pt_ref, qis_ref, wis_ref, kinew_ref, qbd_ref, knew_ref, vnew_ref,
                        *rest, n_pages, t_new, topk):
    del pt_ref
    G = PAGES_PER_STEP
    kidx_refs, k_refs, v_refs = rest[:G], rest[G:2 * G], rest[2 * G:3 * G]
    (biasl_ref, biasn_ref, utri_ref, out_ref, keys_scr, thr_scr, keep_scr,
     tie_scr, flag_scr, m_scr, l_scr, acc_scr) = rest[3 * G:]
    NP = n_pages
    n_steps = NP // G
    phase = pl.program_id(1)
    p = pl.program_id(2)

    def score_keys(kit):
        d = jnp.dot(qis_ref[0], kit, preferred_element_type=_F32)
        w = wis_ref[0]
        sc = None
        for h in range(IDX_HEADS):
            term = w[:, h:h + 1] * jnp.maximum(d[h * t_new:(h + 1) * t_new], 0.0)
            sc = term if sc is None else sc + term
        return _order_key(sc)

    @pl.when(phase == 0)
    def _():
        for j in range(G):
            keys_scr[p * G + j] = score_keys(kidx_refs[j][0].astype(_BF))

    @pl.when((phase == 0) & (p == n_steps - 1))
    def _():
        lane = lax.broadcasted_iota(jnp.int32, (t_new, PAGE_SIZE), 1)
        qrow = lax.broadcasted_iota(jnp.int32, (t_new, PAGE_SIZE), 0)
        keys_scr[NP] = jnp.where(lane <= qrow, score_keys(kinew_ref[0]), INT_MIN)

        def count(pred):
            past_hit = jnp.where(pred(keys_scr[0:NP]), 1.0, 0.0).reshape(NP * t_new, PAGE_SIZE)
            hit = _fold_rows(past_hit, jnp.add, rows=t_new)
            hit = hit + jnp.where(pred(keys_scr[NP]), 1.0, 0.0)
            return jnp.sum(hit, axis=1, keepdims=True)

        kf = float(topk)
        thr, n_ge = _bisect_threshold(lambda c: count(lambda kk: kk >= c), (t_new, 1), kf,
                                      float((NP + 1) * PAGE_SIZE))
        n_gt = count(lambda kk: kk > thr)
        thr_scr[...] = thr
        keep_scr[...] = kf - n_gt
        flag_scr[0] = (jnp.max(jnp.where(n_ge != kf, 1.0, 0.0)) > 0.0).astype(jnp.int32)
        tie_scr[...] = jnp.zeros(tie_scr.shape, _F32)
        m_scr[...] = jnp.full(m_scr.shape, NEG_BIG, _F32)
        l_scr[...] = jnp.zeros(l_scr.shape, _F32)
        acc_scr[...] = jnp.zeros(acc_scr.shape, _F32)

    def attend(keybs, kmats, vmats, bias):
        thr = thr_scr[...]

        def tie_sel():
            masks = []
            for keyb in keybs:
                eq = keyb == thr
                eqf = jnp.where(eq, 1.0, 0.0)
                rank = tie_scr[...] + jnp.dot(eqf.astype(_BF), utri_ref[...],
                                              preferred_element_type=_F32)
                keep = (keyb > thr) | (eq & (rank < keep_scr[...]))
                tie_scr[...] = tie_scr[...] + jnp.sum(eqf, axis=1, keepdims=True)
                masks.append(jnp.where(keep, 0.0, NEG_BIG))
            return jnp.concatenate(masks, axis=1)

        madd8 = lax.cond(flag_scr[0] != 0, tie_sel, lambda: jnp.concatenate(
            [jnp.where(keyb >= thr, 0.0, NEG_BIG) for keyb in keybs], axis=1))
        madd = jnp.concatenate([madd8] * N_HEADS, axis=0)
        s = jnp.concatenate([jnp.dot(qbd_ref[0], km, preferred_element_type=_F32) for km in kmats],
                            axis=1) + madd
        if bias is not None:
            s = s + bias
        m_old = m_scr[...]
        m_new = jnp.maximum(m_old, jnp.max(s, axis=1, keepdims=True))
        pr = jnp.exp(s - m_new)
        alpha = jnp.exp(m_old - m_new)
        l_scr[...] = alpha * l_scr[...] + jnp.sum(pr, axis=1, keepdims=True)
        prb = pr.astype(_BF)
        pv = None
        for j, vm in enumerate(vmats):
            t = lax.dot_general(prb[:, j * PAGE_SIZE:(j + 1) * PAGE_SIZE], vm, _NT,
                                preferred_element_type=_F32)
            pv = t if pv is None else pv + t
        acc_scr[...] = alpha * acc_scr[...] + pv
        m_scr[...] = m_new

    @pl.when(phase == 1)
    def _():
        bias = (p == n_steps - 1).astype(_F32) * biasl_ref[...]
        attend([keys_scr[p * G + j] for j in range(G)],
               [k_refs[j][0].astype(_BF) for j in range(G)],
               [v_refs[j][0].astype(_BF) for j in range(G)], bias)

    @pl.when((phase == 1) & (p == n_steps - 1))
    def _():
        attend([keys_scr[NP]], [knew_ref[0]], [vnew_ref[0]], biasn_ref[...])
        o = acc_scr[...] / l_scr[...]
        head_of_lane = lax.broadcasted_iota(jnp.int32, (t_new, ATTN_W), 1) // HEAD_DIM
        res = jnp.zeros((t_new, ATTN_W), _F32)
        for h in range(N_HEADS):
            res = res + jnp.where(head_of_lane == h, o[h * t_new:(h + 1) * t_new], 0.0)
        out_ref[0] = res.astype(out_ref.dtype)


def _sample_attention(q, k_new, v_new, qi, ki_new, wi, cache_k, cache_v, cache_kidx, page_table,
                      rel_bias):
    nb, t_new = q.shape[:2]
    n_pages = page_table.shape[1]
    n_phys = cache_k.shape[0]
    past = n_pages * PAGE_SIZE
    topk = min(TOPK_MAX, (past + t_new) // 4)
    rows = N_HEADS * t_new
    qis = jnp.swapaxes(qi.reshape(nb, t_new, IDX_HEADS, IDX_DIM), 1, 2)
    qis = qis.reshape(nb, IDX_HEADS * t_new, IDX_DIM)
    G = PAGES_PER_STEP
    assert n_pages % G == 0 and n_pages & (n_pages - 1) == 0
    n_steps = n_pages // G
    q4 = jnp.swapaxes(q.reshape(nb, t_new, N_HEADS, HEAD_DIM), 1, 2)
    eye = jnp.eye(N_HEADS, dtype=_BF)
    qbd = (q4[:, :, :, None, :] * eye[None, :, None, :, None]).reshape(nb, rows, ATTN_W)

    def keys_minor(a):
        return jnp.pad(jnp.swapaxes(a, 1, 2), ((0, 0), (0, 0), (0, PAGE_SIZE - t_new)))

    bd = _rel_bias_by_distance(rel_bias, 2 * PAGE_SIZE)
    qo = jnp.arange(t_new)[:, None]
    co = jnp.arange(PAGE_SIZE)[None, :]
    d_last = jnp.clip(PAGE_SIZE + qo - co, 0, 2 * PAGE_SIZE - 1)
    d_new = jnp.clip(qo - co, 0, 2 * PAGE_SIZE - 1)
    bias_last = jnp.transpose(bd[d_last], (2, 0, 1)).reshape(rows, PAGE_SIZE)
    bias_last = jnp.pad(bias_last, ((0, 0), ((G - 1) * PAGE_SIZE, 0)))
    bias_new = jnp.transpose(bd[d_new], (2, 0, 1)).reshape(rows, PAGE_SIZE)
    utri = (jnp.arange(PAGE_SIZE)[:, None] < jnp.arange(PAGE_SIZE)[None, :]).astype(_BF)

    kern = functools.partial(_sample_attn_kernel, n_pages=n_pages, t_new=t_new, topk=topk)
    seq = lambda i, ph, p, pt: (i, 0, 0)
    const2 = lambda i, ph, p, pt: (0, 0)

    def kidx_page(j):
        return lambda i, ph, p, pt: (
            jnp.where(ph == 0, pt[i * n_pages + p * G + j], pt[i * n_pages + n_pages - G + j]), 0, 0)

    def kv_page(j):
        return lambda i, ph, p, pt: (
            jnp.where(ph == 1, pt[i * n_pages + p * G + j], pt[i * n_pages + j]), 0, 0)

    grid_spec = pltpu.PrefetchScalarGridSpec(
        num_scalar_prefetch=1,
        grid=(nb, 2, n_steps),
        in_specs=[
            pl.BlockSpec((1, IDX_HEADS * t_new, IDX_DIM), seq),
            pl.BlockSpec((1, t_new, IDX_HEADS), seq),
            pl.BlockSpec((1, IDX_DIM, PAGE_SIZE), seq),
            pl.BlockSpec((1, rows, ATTN_W), seq),
            pl.BlockSpec((1, ATTN_W, PAGE_SIZE), seq),
            pl.BlockSpec((1, ATTN_W, PAGE_SIZE), seq),
        ] + [pl.BlockSpec((1, IDX_DIM, PAGE_SIZE), kidx_page(j)) for j in range(G)]
          + [pl.BlockSpec((1, ATTN_W, PAGE_SIZE), kv_page(j)) for j in range(G)]
          + [pl.BlockSpec((1, ATTN_W, PAGE_SIZE), kv_page(j)) for j in range(G)]
          + [
            pl.BlockSpec((rows, G * PAGE_SIZE), const2),
            pl.BlockSpec((rows, PAGE_SIZE), const2),
            pl.BlockSpec((PAGE_SIZE, PAGE_SIZE), const2),
        ],
        out_specs=pl.BlockSpec((1, t_new, ATTN_W), seq),
        scratch_shapes=[
            pltpu.VMEM((n_pages + 1, t_new, PAGE_SIZE), jnp.int32),
            pltpu.VMEM((t_new, 1), jnp.int32),
            pltpu.VMEM((t_new, 1), _F32),
            pltpu.VMEM((t_new, 1), _F32),
            pltpu.SMEM((1,), jnp.int32),
            pltpu.VMEM((rows, 1), _F32),
            pltpu.VMEM((rows, 1), _F32),
            pltpu.VMEM((rows, ATTN_W), _F32),
        ],
    )
    kidx_pages = jnp.swapaxes(cache_kidx, 1, 2)
    k_pages = jnp.transpose(cache_k, (0, 2, 3, 1)).reshape(n_phys, ATTN_W, PAGE_SIZE)
    v_pages = jnp.transpose(cache_v, (0, 2, 3, 1)).reshape(n_phys, ATTN_W, PAGE_SIZE)
    return pl.pallas_call(
        kern,
        grid_spec=grid_spec,
        out_shape=jax.ShapeDtypeStruct((nb, t_new, ATTN_W), _BF),
        compiler_params=pltpu.CompilerParams(
            dimension_semantics=("arbitrary", "arbitrary", "arbitrary"),
            vmem_limit_bytes=VMEM_LIMIT),
        name="sample_sparse_attention",
    )(page_table.reshape(-1).astype(jnp.int32), qis, wi.astype(_F32), keys_minor(ki_new), qbd,
      keys_minor(k_new), keys_minor(v_new),
      *([kidx_pages] * G), *([k_pages] * G), *([v_pages] * G), bias_last, bias_new, utri)


def _pool_diff_kernel(u_ref, halo_ref, d_ref, *, truncate_start):
    tm = u_ref.shape[0]
    i = pl.program_id(1)
    main = u_ref[...]
    halo = jnp.where(i == 0, 0.0, halo_ref[...])
    ext = jnp.concatenate([halo, main], axis=0)
    pos = i * tm + lax.broadcasted_iota(jnp.int32, (tm, POOL_GC), 0)
    for g, w in enumerate(POOL_WINDOWS):
        sl = slice(g * POOL_GC, (g + 1) * POOL_GC)
        acc = ext[:, sl]
        shift = 1
        while shift < w:
            acc = acc + pltpu.roll(acc, shift, 0)
            shift *= 2
        wsum = acc[POOL_HALO:]
        cnt = jnp.minimum(pos + 1, w).astype(_F32) if truncate_start else float(w)
        d_ref[:, sl] = (wsum / cnt - main[:, sl]).astype(d_ref.dtype)


def _pool_diff(u, n_seq, seq_rows, truncate_start):
    tm = min(ROW_TILE, seq_rows)
    assert seq_rows % tm == 0 and tm % POOL_HALO == 0
    nt = seq_rows // tm
    hb = tm // POOL_HALO
    kern = functools.partial(_pool_diff_kernel, truncate_start=truncate_start)
    return pl.pallas_call(
        kern,
        grid=(n_seq, nt),
        in_specs=[pl.BlockSpec((tm, POOL_W), lambda b, i: (b * nt + i, 0)),
                  pl.BlockSpec((POOL_HALO, POOL_W),
                               lambda b, i: (jnp.maximum((b * nt + i) * hb - 1, 0), 0))],
        out_specs=pl.BlockSpec((tm, POOL_W), lambda b, i: (b * nt + i, 0)),
        out_shape=jax.ShapeDtypeStruct(u.shape, _BF),
        compiler_params=pltpu.CompilerParams(dimension_semantics=("arbitrary", "arbitrary")),
        name="pool_window_diff",
    )(u, u)


def _merge_route_kernel(x_ref, attn_ref, d_ref, ga_ref, gp_ref, wmap_ref, pscale_ref, wba_ref,
                        wbp_ref, wout_ref, g2_ref, wr_ref, br_ref, ltri_ref,
                        x1_ref, xn_ref, eid_ref, gate_ref, rank_ref, cnt_ref):
    tm = x_ref.shape[0]
    d = d_ref[...]
    pooled = [jnp.dot(d[:, g * POOL_GC:(g + 1) * POOL_GC], wmap_ref[g], preferred_element_type=_F32)
              for g in range(POOL_GROUPS)]
    pool = (jnp.concatenate(pooled, axis=1) * pscale_ref[...]).astype(_BF)
    a = jnp.dot(attn_ref[...], wba_ref[...], preferred_element_type=_F32)
    pp = jnp.dot(pool, wbp_ref[...], preferred_element_type=_F32)
    m = jax.nn.sigmoid(ga_ref[...]) * a + jax.nn.sigmoid(gp_ref[...]) * pp
    x1 = x_ref[...] + jnp.dot(m.astype(_BF), wout_ref[...], preferred_element_type=_F32)
    x1_ref[...] = x1
    xn = _rms_scale(x1, g2_ref[...]).astype(_BF)
    xn_ref[...] = xn

    logit = jnp.dot(xn, wr_ref[...], preferred_element_type=_F32) + br_ref[...]
    lane = lax.broadcasted_iota(jnp.int32, (tm, LANES), 1).astype(_F32)
    far = float(LANES)

    def first_lane_of(hit):
        return jnp.min(jnp.where(hit, lane, far), axis=1, keepdims=True)

    lc = jnp.where(lane < N_GROUPS, logit, NEG_BIG)
    mc = jnp.max(lc, axis=1, keepdims=True)
    p_grp = 1.0 / jnp.sum(jnp.exp(lc - mc), axis=1, keepdims=True)
    grp = first_lane_of(lc == mc)
    lo = FINE_LANE0 + EXPERTS_PER_GROUP * grp
    in_grp = (lane >= lo) & (lane < lo + EXPERTS_PER_GROUP)
    lf = jnp.where(in_grp, logit, NEG_BIG)
    ef = jnp.exp(lf - jnp.max(lf, axis=1, keepdims=True))
    pf = jnp.where(in_grp, ef / jnp.sum(ef, axis=1, keepdims=True), -1.0)
    p1 = jnp.max(pf, axis=1, keepdims=True)
    l1 = first_lane_of(pf == p1)
    pf2 = jnp.where(lane == l1, -1.0, pf)
    p2 = jnp.max(pf2, axis=1, keepdims=True)
    l2 = first_lane_of(pf2 == p2)
    e1 = l1 - FINE_LANE0
    e2 = l2 - FINE_LANE0
    g1 = p_grp * p1 / (p1 + p2)
    g2 = p_grp * p2 / (p1 + p2)
    eid_ref[...] = jnp.where(lane == 0, e1, jnp.where(lane == 1, e2, 0.0)).astype(jnp.int32)
    gate_ref[...] = jnp.where(lane == 0, g1, jnp.where(lane == 1, g2, 0.0))

    onehot = jnp.where((lane == e1) | (lane == e2), 1.0, 0.0)
    before = jnp.dot(ltri_ref[...], onehot.astype(_BF), preferred_element_type=_F32)
    r1 = jnp.sum(jnp.where(lane == e1, before, 0.0), axis=1, keepdims=True)
    r2 = jnp.sum(jnp.where(lane == e2, before, 0.0), axis=1, keepdims=True)
    rank_ref[...] = jnp.where(lane == 0, r1, jnp.where(lane == 1, r2, 0.0)).astype(jnp.int32)
    cnt_ref[0] = jnp.broadcast_to(jnp.sum(onehot, axis=0, keepdims=True), (SUBLANES, LANES))


def _merge_route(x, attn, d, ga, gp, w):
    n = x.shape[0]
    tm = ROW_TILE
    assert n % tm == 0
    nt = n // tm
    row = lambda width: pl.BlockSpec((tm, width), lambda i: (i, 0))
    full = lambda a: _resident(a.shape, lambda i: (0,) * a.ndim)
    weights = (w["wmap"], w["pscale"], w["wba"], w["wbp"], w["wout"], w["g2"], w["wr"], w["br"],
               w["ltri"])
    return pl.pallas_call(
        _merge_route_kernel,
        grid=(nt,),
        in_specs=[row(D_MODEL), row(ATTN_W), row(POOL_W), row(D_MODEL), row(D_MODEL)]
                 + [full(a) for a in weights],
        out_specs=[row(D_MODEL), row(D_MODEL), row(LANES), row(LANES), row(LANES),
                   pl.BlockSpec((1, SUBLANES, LANES), lambda i: (i, 0, 0))],
        out_shape=[jax.ShapeDtypeStruct((n, D_MODEL), _F32), jax.ShapeDtypeStruct((n, D_MODEL), _BF),
                   jax.ShapeDtypeStruct((n, LANES), jnp.int32), jax.ShapeDtypeStruct((n, LANES), _F32),
                   jax.ShapeDtypeStruct((n, LANES), jnp.int32),
                   jax.ShapeDtypeStruct((nt, SUBLANES, LANES), _F32)],
        compiler_params=pltpu.CompilerParams(dimension_semantics=("arbitrary",),
                                             vmem_limit_bytes=VMEM_LIMIT),
        name="merge_norm_route",
    )(x, attn, d, ga, gp, *weights)


def _experts_kernel(be_ref, nu_ref, x_ref, wg_ref, wu_ref, wd_ref, y_ref):
    del be_ref
    i = pl.program_id(0)

    @pl.when(i < nu_ref[0])
    def _():
        x = x_ref[...]
        hg = jnp.dot(x, wg_ref[0], preferred_element_type=_F32)
        hu = jnp.dot(x, wu_ref[0], preferred_element_type=_F32)
        hdn = (hg * jax.nn.sigmoid(hg) * hu).astype(_BF)
        y_ref[...] = jnp.dot(hdn, wd_ref[0], preferred_element_type=_F32).astype(y_ref.dtype)

    @pl.when(i >= nu_ref[0])
    def _():
        y_ref[...] = jnp.zeros(y_ref.shape, y_ref.dtype)


def _experts(x_pad, block_expert, n_used, wg, wu, wd):
    nblk = block_expert.shape[0]
    bm = MOE_ROWS
    used = lambda i, be, nu: (jnp.minimum(i, jnp.maximum(nu[0] - 1, 0)), 0)
    grid_spec = pltpu.PrefetchScalarGridSpec(
        num_scalar_prefetch=2,
        grid=(nblk,),
        in_specs=[pl.BlockSpec((bm, D_MODEL), used),
                  pl.BlockSpec((1, D_MODEL, D_EXPERT), lambda i, be, nu: (be[i], 0, 0)),
                  pl.BlockSpec((1, D_MODEL, D_EXPERT), lambda i, be, nu: (be[i], 0, 0)),
                  pl.BlockSpec((1, D_EXPERT, D_MODEL), lambda i, be, nu: (be[i], 0, 0))],
        out_specs=pl.BlockSpec((bm, D_MODEL), lambda i, be, nu: (i, 0)),
    )
    return pl.pallas_call(
        _experts_kernel,
        grid_spec=grid_spec,
        out_shape=jax.ShapeDtypeStruct((nblk * bm, D_MODEL), _BF),
        compiler_params=pltpu.CompilerParams(dimension_semantics=("arbitrary",),
                                             vmem_limit_bytes=VMEM_LIMIT),
        name="grouped_swiglu_experts",
    )(block_expert, n_used, x_pad, wg, wu, wd)


def _route_layout(eid, rank, tile_cnt):
    n = eid.shape[0]
    bm = MOE_ROWS
    cnt = tile_cnt[:, 0, :N_EXPERTS].astype(jnp.int32)
    tile_start = jnp.cumsum(cnt, axis=0) - cnt
    counts = jnp.sum(cnt, axis=0)
    padded = ((counts + bm - 1) // bm) * bm
    pad_ends = jnp.cumsum(padded)
    pad_starts = pad_ends - padded
    base = jnp.repeat(tile_start + pad_starts[None, :], ROW_TILE, axis=0)
    onehot = eid[:, :, None] == jnp.arange(N_EXPERTS, dtype=jnp.int32)[None, None, :]
    dest = jnp.sum(jnp.where(onehot, base[:, None, :], 0), axis=-1) + rank
    nblk = (n * TOP_K_FINE) // bm + N_EXPERTS
    block_row0 = jnp.arange(nblk, dtype=jnp.int32) * bm
    block_expert = jnp.minimum(jnp.sum(pad_ends[None, :] <= block_row0[:, None], axis=1),
                               N_EXPERTS - 1).astype(jnp.int32)
    n_used = (pad_ends[-1] // bm).astype(jnp.int32).reshape(1)
    return dest, block_expert, n_used, nblk


def _combine_norm_kernel(x1_ref, y1_ref, y2_ref, gate_ref, g_ref, out_ref):
    gate = gate_ref[...]
    x2 = (x1_ref[...] + gate[:, 0:1] * y1_ref[...].astype(_F32)
          + gate[:, 1:2] * y2_ref[...].astype(_F32))
    out_ref[...] = _rms_scale(x2, g_ref[...])


def _combine_norm(x1, y1, y2, gate, g):
    n = x1.shape[0]
    tm = ROW_TILE
    row = lambda width: pl.BlockSpec((tm, width), lambda i: (i, 0))
    return pl.pallas_call(
        _combine_norm_kernel,
        grid=(n // tm,),
        in_specs=[row(D_MODEL), row(D_MODEL), row(D_MODEL), row(LANES),
                  _resident((1, D_MODEL), lambda i: (0, 0))],
        out_specs=row(D_MODEL),
        out_shape=jax.ShapeDtypeStruct((n, D_MODEL), _F32),
        compiler_params=pltpu.CompilerParams(dimension_semantics=("arbitrary",)),
        name="combine_final_norm",
    )(x1, y1, y2, gate, g.reshape(1, D_MODEL).astype(_F32))


def kernel(x_prompt, x_sample, cache_k, cache_v, cache_kidx, state_pool, page_table, rel_bias,
           ln1_g, w_in, w_pool_map, pool_scale, w_br_attn, w_br_pool, w_out, ln2_g,
           w_coarse, b_coarse, w_fine, b_fine, w_gate, w_up, w_down, lnf_g):
    layer = 0
    nb, s = x_prompt.shape[:2]
    db, tn = x_sample.shape[:2]
    n_p, n_s = nb * s, db * tn
    rel_bias = rel_bias.astype(_F32)

    wr = jnp.zeros((D_MODEL, LANES), _F32)
    wr = wr.at[:, :N_GROUPS].set(w_coarse[layer])
    wr = wr.at[:, FINE_LANE0:FINE_LANE0 + N_EXPERTS].set(
        jnp.transpose(w_fine[layer], (1, 0, 2)).reshape(D_MODEL, N_EXPERTS))
    br = jnp.zeros((1, LANES), _F32)
    br = br.at[0, :N_GROUPS].set(b_coarse[layer])
    br = br.at[0, FINE_LANE0:FINE_LANE0 + N_EXPERTS].set(b_fine[layer].reshape(-1))
    mw = dict(
        wmap=w_pool_map[layer].astype(_BF), pscale=pool_scale[layer].reshape(1, POOL_W).astype(_F32),
        wba=w_br_attn[layer].astype(_BF), wbp=w_br_pool[layer].astype(_BF),
        wout=w_out[layer].astype(_BF), g2=ln2_g[layer].reshape(1, D_MODEL).astype(_F32),
        wr=wr.astype(_BF), br=br,
        ltri=(jnp.arange(ROW_TILE)[:, None] > jnp.arange(ROW_TILE)[None, :]).astype(_BF))

    pp = _project(x_prompt.reshape(n_p, D_MODEL), ln1_g[layer], w_in[layer], nb)
    attn_p = _prompt_attention(pp, rel_bias, nb, s)
    d_p = _pool_diff(pp["u"], nb, s, True)
    x1_p, xn_p, eid_p, gate_p, rank_p, cnt_p = _merge_route(
        x_prompt.reshape(n_p, D_MODEL), attn_p, d_p, pp["ga"], pp["gp"], mw)

    ps = _project(x_sample.reshape(n_s, D_MODEL), ln1_g[layer], w_in[layer], 1)
    q_s = ps["qT"].T.reshape(db, tn, ATTN_W)
    qi_s = ps["qiT"].T.reshape(db, tn, IDX_W)
    wi_s = ps["wiT"][:IDX_HEADS].T.reshape(db, tn, IDX_HEADS)
    k_s, v_s, ki_s = ps["kT32"][0].T, ps["vT32"][0].T, ps["kiT32"][0].T
    attn_s = _sample_attention(q_s, ps["k16"].reshape(db, tn, ATTN_W),
                               v_s.astype(_BF).reshape(db, tn, ATTN_W), qi_s,
                               ps["ki16"].reshape(db, tn, IDX_DIM), wi_s,
                               cache_k[layer], cache_v[layer], cache_kidx[layer], page_table, rel_bias)
    u_s = ps["u"].reshape(db, tn, POOL_W)
    buf = jnp.concatenate([jnp.zeros((db, 1, POOL_W), _F32), state_pool[layer].astype(_F32), u_s],
                          axis=1)
    grp_rows = 1 + POOL_CTX + tn
    d_s = _pool_diff(buf.reshape(db * grp_rows, POOL_W), 1, db * grp_rows, False)
    d_s = d_s.reshape(db, grp_rows, POOL_W)[:, 1 + POOL_CTX:].reshape(n_s, POOL_W)
    x1_s, xn_s, eid_s, gate_s, rank_s, cnt_s = _merge_route(
        x_sample.reshape(n_s, D_MODEL), attn_s.reshape(n_s, ATTN_W), d_s, ps["ga"], ps["gp"], mw)

    n_all = n_p + n_s
    eid = jnp.concatenate([eid_p[:, :TOP_K_FINE], eid_s[:, :TOP_K_FINE]], axis=0)
    rank = jnp.concatenate([rank_p[:, :TOP_K_FINE], rank_s[:, :TOP_K_FINE]], axis=0)
    dest, block_expert, n_used, nblk = _route_layout(eid, rank, jnp.concatenate([cnt_p, cnt_s], axis=0))
    tok = jnp.broadcast_to(jnp.arange(n_all, dtype=jnp.int32)[:, None], dest.shape)
    slot_tok = jnp.full((nblk * MOE_ROWS,), n_all, jnp.int32).at[dest.reshape(-1)].set(tok.reshape(-1))
    x_pad = jnp.concatenate([xn_p, xn_s, jnp.zeros((1, D_MODEL), _BF)], axis=0)[slot_tok]
    yb = _experts(x_pad, block_expert, n_used, w_gate[layer].astype(_BF), w_up[layer].astype(_BF),
                  w_down[layer].astype(_BF))
    dest_p, dest_s = dest[:n_p], dest[n_p:]
    y_prompt = _combine_norm(x1_p, yb[dest_p[:, 0]], yb[dest_p[:, 1]], gate_p, lnf_g)
    y_sample = _combine_norm(x1_s, yb[dest_s[:, 0]], yb[dest_s[:, 1]], gate_s, lnf_g)
    y_prompt = y_prompt.reshape(nb, s, D_MODEL)
    y_sample = y_sample.reshape(db, tn, D_MODEL)
    head = lambda a, n, t: a.reshape(1, n, t, N_HEADS, HEAD_DIM)
    heads_kminor = lambda a: jnp.transpose(a.reshape(nb, N_HEADS, HEAD_DIM, s), (0, 3, 1, 2))[None]
    return (y_prompt, y_sample,
            heads_kminor(pp["kT32"]), heads_kminor(pp["vT32"]), jnp.swapaxes(pp["kiT32"], 1, 2)[None],
            pp["u"].reshape(nb, s, POOL_W)[None, :, -POOL_CTX:],
            head(k_s, db, tn), head(v_s, db, tn), ki_s.reshape(1, db, tn, IDX_DIM),
            buf[None, :, -POOL_CTX:])
```

```python
import functools
import math

import jax
import jax.numpy as jnp
from jax import lax
from jax.experimental import pallas as pl
from jax.experimental.pallas import tpu as pltpu

D_MODEL = 1024
N_HEADS = 8
HEAD_DIM = 64
ATTN_W = N_HEADS * HEAD_DIM
IDX_HEADS = 4
IDX_DIM = 64
IDX_W = IDX_HEADS * IDX_DIM
TOPK_MAX = 256
PAGE_SIZE = 128
REL_BUCKETS = 32
REL_MAX_EXACT = 16
REL_MAX_DIST = 128
POOL_GROUPS = 4
POOL_GC = 128
POOL_W = POOL_GROUPS * POOL_GC
POOL_WINDOWS = (2, 4, 8, 16)
POOL_CTX = 15
N_GROUPS = 4
EXPERTS_PER_GROUP = 8
N_EXPERTS = N_GROUPS * EXPERTS_PER_GROUP
TOP_K_FINE = 2
D_EXPERT = 512
RMS_EPS = 1e-6

LANES = 128
SUBLANES = 8
HEAD_PAIR = 2 * HEAD_DIM
INT_MIN = -(2 ** 31)
NEG_BIG = -1e30
VMEM_LIMIT = 56 * 1024 * 1024
ROW_TILE = 512
Q_TILE = 128
K_TILE = 512
FAR_UNROLL = 2
COUNT_ROWS = 64
SUM_ROWS = 16
PAGES_PER_STEP = 32
MOE_ROWS = 512
POOL_HALO = 16
FINE_LANE0 = 8

_NT = (((1,), (1,)), ((), ()))
_BF = jnp.bfloat16
_F32 = jnp.float32


def _resident(shape, index_map):
    return pl.BlockSpec(shape, index_map, pipeline_mode=pl.Buffered(1))


def _order_key(score):
    bits = pltpu.bitcast(score, jnp.int32)
    key = bits ^ ((bits >> 31) & 0x7FFFFFFF)
    return jnp.where(key == -1, 0, key)


def _bisect_threshold(count_ge, shape, topk, n_keys):
    def body(i, carry):
        lo, n_lo = carry
        cand = lo + jnp.left_shift(jnp.int32(1), 31 - i)
        n_cand = count_ge(cand)
        ok = n_cand >= topk
        return jnp.where(ok, cand, lo), jnp.where(ok, n_cand, n_lo)

    init = (jnp.full(shape, INT_MIN, jnp.int32), jnp.full(shape, n_keys, _F32))
    return lax.fori_loop(0, 32, body, init)


def _fold_rows(x, op, rows=SUBLANES):
    while x.shape[0] > rows:
        half = x.shape[0] // 2
        x = op(x[:half], x[half:])
    return x


def _rms_scale(x, g):
    ms = jnp.mean(x * x, axis=-1, keepdims=True)
    return x * lax.rsqrt(ms + RMS_EPS) * g


_ROW_SECTIONS = (("k", ATTN_W), ("u", POOL_W), ("ga", D_MODEL), ("gp", D_MODEL),
                 ("ki", LANES))
_COL_SECTIONS = (("q", ATTN_W), ("qi", IDX_W), ("v", ATTN_W), ("k", ATTN_W), ("ki", IDX_DIM),
                 ("wi", SUBLANES))


def _proj_kernel(x_ref, g_ref, wa_ref, wbt_ref,
                 u_ref, ga_ref, gp_ref, k16_ref, ki16_ref,
                 qt_ref, qit_ref, vt_ref, wit_ref, kt32_ref, vt32_ref, kit32_ref):
    h = _rms_scale(x_ref[...], g_ref[...]).astype(_BF)
    lo = 0
    for name, width in _ROW_SECTIONS:
        t = jnp.dot(h, wa_ref[:, lo:lo + width], preferred_element_type=_F32)
        lo += width
        if name == "k":
            k16_ref[...] = t.astype(_BF)
        elif name == "ki":
            ki16_ref[...] = t[:, :IDX_DIM].astype(_BF)
        else:
            {"u": u_ref, "ga": ga_ref, "gp": gp_ref}[name][...] = t
    lo = 0
    for name, width in _COL_SECTIONS:
        t = lax.dot_general(wbt_ref[lo:lo + width, :], h, _NT, preferred_element_type=_F32)
        lo += width
        if name == "q":
            qt_ref[...] = t.astype(_BF)
        elif name == "qi":
            qit_ref[...] = t.astype(_BF)
        elif name == "v":
            vt_ref[0] = t.astype(_BF)
            vt32_ref[0] = t
        elif name == "k":
            kt32_ref[0] = t
        elif name == "ki":
            kit32_ref[0] = t
        else:
            wit_ref[...] = t


def _project(x, ln_g, w_in, n_seq):
    n = x.shape[0]
    tm = ROW_TILE
    assert n % (tm * n_seq) == 0 and tm == K_TILE
    seq_tiles = n // (tm * n_seq)
    widths = (ATTN_W, ATTN_W, ATTN_W, IDX_W, IDX_DIM, IDX_HEADS, POOL_W, D_MODEL, D_MODEL)
    names = ("q", "k", "v", "qi", "ki", "wi", "u", "ga", "gp")
    cols, lo = {}, 0
    for name, width in zip(names, widths):
        cols[name] = w_in[:, lo:lo + width]
        lo += width
    ki_pad = jnp.pad(cols["ki"], ((0, 0), (0, LANES - IDX_DIM)))
    wa = jnp.concatenate([cols["k"], cols["u"], cols["ga"], cols["gp"], ki_pad], axis=1).astype(_BF)
    wi_pad = jnp.pad(cols["wi"], ((0, 0), (0, SUBLANES - IDX_HEADS)))
    wbt = jnp.concatenate([cols["q"] * HEAD_DIM ** -0.5, cols["qi"] * IDX_DIM ** -0.5, cols["v"],
                           cols["k"], cols["ki"], wi_pad], axis=1).T.astype(_BF)
    na, nb = wa.shape[1], wbt.shape[0]
    row = lambda w: pl.BlockSpec((tm, w), lambda i: (i, 0))
    colb = lambda h: pl.BlockSpec((h, tm), lambda i: (0, i))
    seqb = lambda h: pl.BlockSpec((1, h, tm), lambda i: (i // seq_tiles, 0, i % seq_tiles))
    seq_shape = lambda h: (n_seq, h, n // n_seq)
    out_shapes = dict(
        u=((n, POOL_W), _F32, row(POOL_W)), ga=((n, D_MODEL), _F32, row(D_MODEL)),
        gp=((n, D_MODEL), _F32, row(D_MODEL)),
        k16=((n, ATTN_W), _BF, row(ATTN_W)), ki16=((n, IDX_DIM), _BF, row(IDX_DIM)),
        qT=((ATTN_W, n), _BF, colb(ATTN_W)), qiT=((IDX_W, n), _BF, colb(IDX_W)),
        vT=((n // tm, ATTN_W, tm), _BF, pl.BlockSpec((1, ATTN_W, tm), lambda i: (i, 0, 0))),
        wiT=((SUBLANES, n), _F32, colb(SUBLANES)),
        kT32=(seq_shape(ATTN_W), _F32, seqb(ATTN_W)), vT32=(seq_shape(ATTN_W), _F32, seqb(ATTN_W)),
        kiT32=(seq_shape(IDX_DIM), _F32, seqb(IDX_DIM)),
    )
    keys = list(out_shapes)
    res = pl.pallas_call(
        _proj_kernel,
        grid=(n // tm,),
        in_specs=[row(D_MODEL), _resident((1, D_MODEL), lambda i: (0, 0)),
                  _resident((D_MODEL, na), lambda i: (0, 0)),
                  _resident((nb, D_MODEL), lambda i: (0, 0))],
        out_specs=[out_shapes[k][2] for k in keys],
        out_shape=[jax.ShapeDtypeStruct(out_shapes[k][0], out_shapes[k][1]) for k in keys],
        compiler_params=pltpu.CompilerParams(dimension_semantics=("arbitrary",),
                                             vmem_limit_bytes=VMEM_LIMIT),
        name="input_projection",
    )(x, ln_g.reshape(1, D_MODEL).astype(_F32), wa, wbt)
    return dict(zip(keys, res))


def _prompt_attn_kernel(qT_ref, qiT_ref, wiT_ref, k_ref, vT_ref, ki_ref, bias_ref, ltri_ref,
                        out_ref, keys_scr, mask_scr, qm_scr, m_scr, acc_scr, *, topk):
    TQ, TK = Q_TILE, K_TILE
    sub = TK // TQ
    qb = pl.program_id(1)
    n_sb = qb // sub + 1
    n_far = jnp.maximum(qb - 1, 0) // sub
    key_off = lax.broadcasted_iota(jnp.int32, (TK, TQ), 0)
    qry_off = lax.broadcasted_iota(jnp.int32, (TK, TQ), 1)

    def causal_at(sb):
        return (sb * TK + key_off) <= (qb * TQ + qry_off)

    qi_wide = jnp.concatenate([qiT_ref[h * IDX_DIM:(h + 1) * IDX_DIM, :] for h in range(IDX_HEADS)],
                              axis=1)
    wT = wiT_ref[...]

    def score_keys(sb, causal=None):
        kib = ki_ref[pl.ds(pl.multiple_of(sb * TK, TK), TK), :]
        d = jnp.dot(kib, qi_wide, preferred_element_type=_F32)
        sc = None
        for h in range(IDX_HEADS):
            term = wT[h:h + 1, :] * jnp.maximum(d[:, h * TQ:(h + 1) * TQ], 0.0)
            sc = term if sc is None else sc + term
        key = _order_key(sc)
        if causal is not None:
            key = jnp.where(causal, key, INT_MIN)
        keys_scr[sb] = key

    def score_body(sb, carry):
        score_keys(sb)
        return carry

    lax.fori_loop(0, n_sb - 1, score_body, 0)
    score_keys(n_sb - 1, causal_at(n_sb - 1))

    def count(pred):
        def body(sb, acc):
            hit = jnp.where(pred(keys_scr[sb]), 1.0, 0.0)
            return acc + jnp.sum(hit.reshape(TK // COUNT_ROWS, COUNT_ROWS, TQ), axis=0)
        acc = lax.fori_loop(0, n_sb, body, jnp.zeros((COUNT_ROWS, TQ), _F32))
        return jnp.sum(_fold_rows(acc, jnp.add), axis=0, keepdims=True)

    kf = float(topk)
    thr, n_ge = _bisect_threshold(lambda c: count(lambda kk: kk >= c), (1, TQ), kf,
                                  (n_sb * TK).astype(_F32))
    has_ties = jnp.max(jnp.where(n_ge != kf, 1.0, 0.0)) > 0.0
    n_tie_keep = kf - lax.cond(has_ties, lambda: count(lambda kk: kk > thr),
                               lambda: jnp.zeros((1, TQ), _F32))

    zeros_half = jnp.zeros((HEAD_DIM, TQ), _BF)
    for pair in range(N_HEADS // 2):
        q0 = qT_ref[(2 * pair) * HEAD_DIM:(2 * pair + 1) * HEAD_DIM, :]
        q1 = qT_ref[(2 * pair + 1) * HEAD_DIM:(2 * pair + 2) * HEAD_DIM, :]
        qm_scr[pair] = jnp.concatenate([jnp.concatenate([q0, zeros_half], axis=1),
                                        jnp.concatenate([zeros_half, q1], axis=1)], axis=0)
    m_scr[...] = jnp.full(m_scr.shape, NEG_BIG, _F32)
    acc_scr[...] = jnp.zeros(acc_scr.shape, _F32)
    ones_rows = jnp.ones((SUM_ROWS, TK), _BF)
    n_pairs = N_HEADS // 2

    def logits(sb, pair):
        kp = k_ref[pl.ds(pl.multiple_of(sb * TK, TK), TK),
                   pair * HEAD_PAIR:(pair + 1) * HEAD_PAIR]
        return jnp.dot(kp, qm_scr[pair], preferred_element_type=_F32)

    def attend_blocks(sbs, tie_seen, near, ties):
        stages = [(slot, sb, pair) for slot, sb in enumerate(sbs) for pair in range(n_pairs)]
        s_next = logits(sbs[0], 0)
        for slot, sb in enumerate(sbs):
            keyb = keys_scr[sb]
            if ties:
                eq = keyb == thr
                eqf = jnp.where(eq, 1.0, 0.0)
                rank = tie_seen + jnp.dot(ltri_ref[...], eqf.astype(_BF), preferred_element_type=_F32)
                keep = (keyb > thr) | (eq & (rank < n_tie_keep))
                madd = jnp.where(keep, 0.0, NEG_BIG)
                tie_seen = tie_seen + jnp.sum(eqf, axis=0, keepdims=True)
            else:
                madd = jnp.where(keyb >= thr, 0.0, NEG_BIG)
            if near:
                madd = jnp.where(causal_at(sb), madd, NEG_BIG)
            mask_scr[slot] = madd
        for k, (slot, sb, pair) in enumerate(stages):
            s2 = s_next
            if k + 1 < len(stages):
                s_next = logits(stages[k + 1][1], stages[k + 1][2])
            vp = vT_ref[sb, pair * HEAD_PAIR:(pair + 1) * HEAD_PAIR, :]
            halves = []
            for odd in range(2):
                h = 2 * pair + odd
                s = s2[:, odd * TQ:(odd + 1) * TQ] + mask_scr[slot]
                if near:
                    parts = []
                    for j in range(sub):
                        back = qb - (sb * sub + j)
                        w0 = (back == 0).astype(_F32)
                        w1 = (back == 1).astype(_F32)
                        parts.append(s[j * TQ:(j + 1) * TQ] + w0 * bias_ref[0, h] + w1 * bias_ref[1, h])
                    s = jnp.concatenate(parts, axis=0)
                halves.append(s)
            m_old = m_scr[pair]
            m_blk = jnp.concatenate(
                [jnp.max(_fold_rows(s, jnp.maximum), axis=0, keepdims=True) for s in halves], axis=1)
            m_new = jnp.maximum(m_old, m_blk)
            p2 = jnp.concatenate([jnp.exp(s - m_new[:, odd * TQ:(odd + 1) * TQ]).astype(_BF)
                                  for odd, s in enumerate(halves)], axis=1)
            alpha = jnp.exp(m_old - m_new)
            va = jnp.concatenate([vp, ones_rows], axis=0)
            acc_scr[pair] = alpha * acc_scr[pair] + jnp.dot(va, p2, preferred_element_type=_F32)
            m_scr[pair] = m_new
        return tie_seen

    def attend_all(ties):
        def run():
            seen = lax.fori_loop(
                0, n_far // FAR_UNROLL,
                lambda i, c: attend_blocks([FAR_UNROLL * i + u for u in range(FAR_UNROLL)], c, False, ties),
                jnp.zeros((1, TQ), _F32))
            seen = lax.fori_loop(n_far - n_far % FAR_UNROLL, n_far,
                                 lambda sb, c: attend_blocks([sb], c, False, ties), seen)
            lax.fori_loop(n_far, n_sb, lambda sb, c: attend_blocks([sb], c, True, ties), seen)
        return run

    lax.cond(has_ties, attend_all(True), attend_all(False))

    for pair in range(N_HEADS // 2):
        a = acc_scr[pair]
        res = jnp.concatenate(
            [a[:HEAD_DIM, :TQ] / a[HEAD_PAIR:HEAD_PAIR + 1, :TQ],
             a[HEAD_DIM:HEAD_PAIR, TQ:] / a[HEAD_PAIR:HEAD_PAIR + 1, TQ:]], axis=0)
        out_ref[:, pair * HEAD_PAIR:(pair + 1) * HEAD_PAIR] = res.T.astype(out_ref.dtype)


def _rel_bias_by_distance(rel_bias, n):
    dist = jnp.arange(n, dtype=jnp.int32)
    nf = jnp.maximum(dist, 1).astype(_F32)
    large = REL_MAX_EXACT + (jnp.log(nf / REL_MAX_EXACT) / math.log(REL_MAX_DIST / REL_MAX_EXACT)
                             * (REL_BUCKETS - REL_MAX_EXACT)).astype(jnp.int32)
    large = jnp.minimum(large, REL_BUCKETS - 1)
    bucket = jnp.where(dist < REL_MAX_EXACT, dist, large)
    return (rel_bias[bucket] - rel_bias[REL_BUCKETS - 1][None, :]).astype(_F32)


def _prompt_attention(proj, rel_bias, nb, s):
    TQ, TK = Q_TILE, K_TILE
    assert s % TK == 0
    nq, nsb = s // TQ, s // TK
    topk = min(TOPK_MAX, s // 4)
    bd = _rel_bias_by_distance(rel_bias, 2 * TQ)
    key_off = jnp.arange(TQ)[:, None]
    qry_off = jnp.arange(TQ)[None, :]
    tiles = [jnp.transpose(bd[jnp.clip(back * TQ + qry_off - key_off, 0, 2 * TQ - 1)], (2, 0, 1))
             for back in range(2)]
    bias_tiles = jnp.stack(tiles)
    ltri = (jnp.arange(TK)[:, None] > jnp.arange(TK)[None, :]).astype(_BF)

    kern = functools.partial(_prompt_attn_kernel, topk=topk)
    return pl.pallas_call(
        kern,
        grid=(nb, nq),
        in_specs=[
            pl.BlockSpec((ATTN_W, TQ), lambda i, j: (0, i * nq + j)),
            pl.BlockSpec((IDX_W, TQ), lambda i, j: (0, i * nq + j)),
            pl.BlockSpec((SUBLANES, TQ), lambda i, j: (0, i * nq + j)),
            _resident((s, ATTN_W), lambda i, j: (i, 0)),
            _resident((nsb, ATTN_W, TK), lambda i, j: (i, 0, 0)),
            _resident((s, IDX_DIM), lambda i, j: (i, 0)),
            _resident((2, N_HEADS, TQ, TQ), lambda i, j: (0, 0, 0, 0)),
            _resident((TK, TK), lambda i, j: (0, 0)),
        ],
        out_specs=pl.BlockSpec((TQ, ATTN_W), lambda i, j: (i * nq + j, 0)),
        out_shape=jax.ShapeDtypeStruct((nb * s, ATTN_W), _BF),
        scratch_shapes=[
            pltpu.VMEM((nsb, TK, TQ), jnp.int32),
            pltpu.VMEM((FAR_UNROLL, TK, TQ), _F32),
            pltpu.VMEM((N_HEADS // 2, HEAD_PAIR, 2 * TQ), _BF),
            pltpu.VMEM((N_HEADS // 2, 1, 2 * TQ), _F32),
            pltpu.VMEM((N_HEADS // 2, HEAD_PAIR + SUM_ROWS, 2 * TQ), _F32),
        ],
        compiler_params=pltpu.CompilerParams(
            dimension_semantics=("arbitrary", "arbitrary"), vmem_limit_bytes=VMEM_LIMIT),
        name="prompt_sparse_attention",
    )(proj["qT"], proj["qiT"], proj["wiT"], proj["k16"], proj["vT"], proj["ki16"], bias_tiles, ltri)


def _sample_attn_kernel(pt_ref, qis_ref, wis_ref, kinew_ref, qbd_ref, knew_ref, vnew_ref,
                        *rest, n_pages, t_new, topk):
    del pt_ref
    G = PAGES_PER_STEP
    kidx_refs, k_refs, v_refs = rest[:G], rest[G:2 * G], rest[2 * G:3 * G]
    (biasl_ref, biasn_ref, utri_ref, out_ref, keys_scr, thr_scr, keep_scr,
     tie_scr, flag_scr, m_scr, l_scr, acc_scr) = rest[3 * G:]
    NP = n_pages
    n_steps = NP // G
    phase = pl.program_id(1)
    p = pl.program_id(2)

    def score_keys(kit):
        d = jnp.dot(qis_ref[0], kit, preferred_element_type=_F32)
        w = wis_ref[0]
        sc = None
        for h in range(IDX_HEADS):
            term = w[:, h:h + 1] * jnp.maximum(d[h * t_new:(h + 1) * t_new], 0.0)
            sc = term if sc is None else sc + term
        return _order_key(sc)

    @pl.when(phase == 0)
    def _():
        for j in range(G):
            keys_scr[p * G + j] = score_keys(kidx_refs[j][0].astype(_BF))

    @pl.when((phase == 0) & (p == n_steps - 1))
    def _():
        lane = lax.broadcasted_iota(jnp.int32, (t_new, PAGE_SIZE), 1)
        qrow = lax.broadcasted_iota(jnp.int32, (t_new, PAGE_SIZE), 0)
        keys_scr[NP] = jnp.where(lane <= qrow, score_keys(kinew_ref[0]), INT_MIN)

        def count(pred):
            past_hit = jnp.where(pred(keys_scr[0:NP]), 1.0, 0.0).reshape(NP * t_new, PAGE_SIZE)
            hit = _fold_rows(past_hit, jnp.add, rows=t_new)
            hit = hit + jnp.where(pred(keys_scr[NP]), 1.0, 0.0)
            return jnp.sum(hit, axis=1, keepdims=True)

        kf = float(topk)
        thr, n_ge = _bisect_threshold(lambda c: count(lambda kk: kk >= c), (t_new, 1), kf,
                                      float((NP + 1) * PAGE_SIZE))
        n_gt = count(lambda kk: kk > thr)
        thr_scr[...] = thr
        keep_scr[...] = kf - n_gt
        flag_scr[0] = (jnp.max(jnp.where(n_ge != kf, 1.0, 0.0)) > 0.0).astype(jnp.int32)
        tie_scr[...] = jnp.zeros(tie_scr.shape, _F32)
        m_scr[...] = jnp.full(m_scr.shape, NEG_BIG, _F32)
        l_scr[...] = jnp.zeros(l_scr.shape, _F32)
        acc_scr[...] = jnp.zeros(acc_scr.shape, _F32)

    def attend(keybs, kmats, vmats, bias):
        thr = thr_scr[...]

        def tie_sel():
            masks = []
            for keyb in keybs:
                eq = keyb == thr
                eqf = jnp.where(eq, 1.0, 0.0)
                rank = tie_scr[...] + jnp.dot(eqf.astype(_BF), utri_ref[...],
                                              preferred_element_type=_F32)
                keep = (keyb > thr) | (eq & (rank < keep_scr[...]))
                tie_scr[...] = tie_scr[...] + jnp.sum(eqf, axis=1, keepdims=True)
                masks.append(jnp.where(keep, 0.0, NEG_BIG))
            return jnp.concatenate(masks, axis=1)

        madd8 = lax.cond(flag_scr[0] != 0, tie_sel, lambda: jnp.concatenate(
            [jnp.where(keyb >= thr, 0.0, NEG_BIG) for keyb in keybs], axis=1))
        madd = jnp.concatenate([madd8] * N_HEADS, axis=0)
        s = jnp.concatenate([jnp.dot(qbd_ref[0], km, preferred_element_type=_F32) for km in kmats],
                            axis=1) + madd
        if bias is not None:
            s = s + bias
        m_old = m_scr[...]
        m_new = jnp.maximum(m_old, jnp.max(s, axis=1, keepdims=True))
        pr = jnp.exp(s - m_new)
        alpha = jnp.exp(m_old - m_new)
        l_scr[...] = alpha * l_scr[...] + jnp.sum(pr, axis=1, keepdims=True)
        prb = pr.astype(_BF)
        pv = None
        for j, vm in enumerate(vmats):
            t = lax.dot_general(prb[:, j * PAGE_SIZE:(j + 1) * PAGE_SIZE], vm, _NT,
                                preferred_element_type=_F32)
            pv = t if pv is None else pv + t
        acc_scr[...] = alpha * acc_scr[...] + pv
        m_scr[...] = m_new

    @pl.when(phase == 1)
    def _():
        bias = (p == n_steps - 1).astype(_F32) * biasl_ref[...]
        attend([keys_scr[p * G + j] for j in range(G)],
               [k_refs[j][0].astype(_BF) for j in range(G)],
               [v_refs[j][0].astype(_BF) for j in range(G)], bias)

    @pl.when((phase == 1) & (p == n_steps - 1))
    def _():
        attend([keys_scr[NP]], [knew_ref[0]], [vnew_ref[0]], biasn_ref[...])
        o = acc_scr[...] / l_scr[...]
        head_of_lane = lax.broadcasted_iota(jnp.int32, (t_new, ATTN_W), 1) // HEAD_DIM
        res = jnp.zeros((t_new, ATTN_W), _F32)
        for h in range(N_HEADS):
            res = res + jnp.where(head_of_lane == h, o[h * t_new:(h + 1) * t_new], 0.0)
        out_ref[0] = res.astype(out_ref.dtype)


def _sample_attention(q, k_new, v_new, qi, ki_new, wi, cache_k, cache_v, cache_kidx, page_table,
                      rel_bias):
    nb, t_new = q.shape[:2]
    n_pages = page_table.shape[1]
    n_phys = cache_k.shape[0]
    past = n_pages * PAGE_SIZE
    topk = min(TOPK_MAX, (past + t_new) // 4)
    rows = N_HEADS * t_new
    qis = jnp.swapaxes(qi.reshape(nb, t_new, IDX_HEADS, IDX_DIM), 1, 2)
    qis = qis.reshape(nb, IDX_HEADS * t_new, IDX_DIM)
    G = PAGES_PER_STEP
    assert n_pages % G == 0 and n_pages & (n_pages - 1) == 0
    n_steps = n_pages // G
    q4 = jnp.swapaxes(q.reshape(nb, t_new, N_HEADS, HEAD_DIM), 1, 2)
    eye = jnp.eye(N_HEADS, dtype=_BF)
    qbd = (q4[:, :, :, None, :] * eye[None, :, None, :, None]).reshape(nb, rows, ATTN_W)

    def keys_minor(a):
        return jnp.pad(jnp.swapaxes(a, 1, 2), ((0, 0), (0, 0), (0, PAGE_SIZE - t_new)))

    bd = _rel_bias_by_distance(rel_bias, 2 * PAGE_SIZE)
    qo = jnp.arange(t_new)[:, None]
    co = jnp.arange(PAGE_SIZE)[None, :]
    d_last = jnp.clip(PAGE_SIZE + qo - co, 0, 2 * PAGE_SIZE - 1)
    d_new = jnp.clip(qo - co, 0, 2 * PAGE_SIZE - 1)
    bias_last = jnp.transpose(bd[d_last], (2, 0, 1)).reshape(rows, PAGE_SIZE)
    bias_last = jnp.pad(bias_last, ((0, 0), ((G - 1) * PAGE_SIZE, 0)))
    bias_new = jnp.transpose(bd[d_new], (2, 0, 1)).reshape(rows, PAGE_SIZE)
    utri = (jnp.arange(PAGE_SIZE)[:, None] < jnp.arange(PAGE_SIZE)[None, :]).astype(_BF)

    kern = functools.partial(_sample_attn_kernel, n_pages=n_pages, t_new=t_new, topk=topk)
    seq = lambda i, ph, p, pt: (i, 0, 0)
    const2 = lambda i, ph, p, pt: (0, 0)

    def kidx_page(j):
        return lambda i, ph, p, pt: (
            jnp.where(ph == 0, pt[i * n_pages + p * G + j], pt[i * n_pages + n_pages - G + j]), 0, 0)

    def kv_page(j):
        return lambda i, ph, p, pt: (
            jnp.where(ph == 1, pt[i * n_pages + p * G + j], pt[i * n_pages + j]), 0, 0)

    grid_spec = pltpu.PrefetchScalarGridSpec(
        num_scalar_prefetch=1,
        grid=(nb, 2, n_steps),
        in_specs=[
            pl.BlockSpec((1, IDX_HEADS * t_new, IDX_DIM), seq),
            pl.BlockSpec((1, t_new, IDX_HEADS), seq),
            pl.BlockSpec((1, IDX_DIM, PAGE_SIZE), seq),
            pl.BlockSpec((1, rows, ATTN_W), seq),
            pl.BlockSpec((1, ATTN_W, PAGE_SIZE), seq),
            pl.BlockSpec((1, ATTN_W, PAGE_SIZE), seq),
        ] + [pl.BlockSpec((1, IDX_DIM, PAGE_SIZE), kidx_page(j)) for j in range(G)]
          + [pl.BlockSpec((1, ATTN_W, PAGE_SIZE), kv_page(j)) for j in range(G)]
          + [pl.BlockSpec((1, ATTN_W, PAGE_SIZE), kv_page(j)) for j in range(G)]
          + [
            pl.BlockSpec((rows, G * PAGE_SIZE), const2),
            pl.BlockSpec((rows, PAGE_SIZE), const2),
            pl.BlockSpec((PAGE_SIZE, PAGE_SIZE), const2),
        ],
        out_specs=pl.BlockSpec((1, t_new, ATTN_W), seq),
        scratch_shapes=[
            pltpu.VMEM((n_pages + 1, t_new, PAGE_SIZE), jnp.int32),
            pltpu.VMEM((t_new, 1), jnp.int32),
            pltpu.VMEM((t_new, 1), _F32),
            pltpu.VMEM((t_new, 1), _F32),
            pltpu.SMEM((1,), jnp.int32),
            pltpu.VMEM((rows, 1), _F32),
            pltpu.VMEM((rows, 1), _F32),
            pltpu.VMEM((rows, ATTN_W), _F32),
        ],
    )
    kidx_pages = jnp.swapaxes(cache_kidx, 1, 2)
    k_pages = jnp.transpose(cache_k, (0, 2, 3, 1)).reshape(n_phys, ATTN_W, PAGE_SIZE)
    v_pages = jnp.transpose(cache_v, (0, 2, 3, 1)).reshape(n_phys, ATTN_W, PAGE_SIZE)
    return pl.pallas_call(
        kern,
        grid_spec=grid_spec,
        out_shape=jax.ShapeDtypeStruct((nb, t_new, ATTN_W), _BF),
        compiler_params=pltpu.CompilerParams(
            dimension_semantics=("arbitrary", "arbitrary", "arbitrary"),
            vmem_limit_bytes=VMEM_LIMIT),
        name="sample_sparse_attention",
    )(page_table.reshape(-1).astype(jnp.int32), qis, wi.astype(_F32), keys_minor(ki_new), qbd,
      keys_minor(k_new), keys_minor(v_new),
      *([kidx_pages] * G), *([k_pages] * G), *([v_pages] * G), bias_last, bias_new, utri)


def _pool_diff_kernel(u_ref, halo_ref, d_ref, *, truncate_start):
    tm = u_ref.shape[0]
    i = pl.program_id(1)
    main = u_ref[...]
    halo = jnp.where(i == 0, 0.0, halo_ref[...])
    ext = jnp.concatenate([halo, main], axis=0)
    pos = i * tm + lax.broadcasted_iota(jnp.int32, (tm, POOL_GC), 0)
    for g, w in enumerate(POOL_WINDOWS):
        sl = slice(g * POOL_GC, (g + 1) * POOL_GC)
        acc = ext[:, sl]
        shift = 1
        while shift < w:
            acc = acc + pltpu.roll(acc, shift, 0)
            shift *= 2
        wsum = acc[POOL_HALO:]
        cnt = jnp.minimum(pos + 1, w).astype(_F32) if truncate_start else float(w)
        d_ref[:, sl] = (wsum / cnt - main[:, sl]).astype(d_ref.dtype)


def _pool_diff(u, n_seq, seq_rows, truncate_start):
    tm = min(ROW_TILE, seq_rows)
    assert seq_rows % tm == 0 and tm % POOL_HALO == 0
    nt = seq_rows // tm
    hb = tm // POOL_HALO
    kern = functools.partial(_pool_diff_kernel, truncate_start=truncate_start)
    return pl.pallas_call(
        kern,
        grid=(n_seq, nt),
        in_specs=[pl.BlockSpec((tm, POOL_W), lambda b, i: (b * nt + i, 0)),
                  pl.BlockSpec((POOL_HALO, POOL_W),
                               lambda b, i: (jnp.maximum((b * nt + i) * hb - 1, 0), 0))],
        out_specs=pl.BlockSpec((tm, POOL_W), lambda b, i: (b * nt + i, 0)),
        out_shape=jax.ShapeDtypeStruct(u.shape, _BF),
        compiler_params=pltpu.CompilerParams(dimension_semantics=("arbitrary", "arbitrary")),
        name="pool_window_diff",
    )(u, u)


def _merge_route_kernel(x_ref, attn_ref, d_ref, ga_ref, gp_ref, wmap_ref, pscale_ref, wba_ref,
                        wbp_ref, wout_ref, g2_ref, wr_ref, br_ref, ltri_ref,
                        x1_ref, xn_ref, eid_ref, gate_ref, rank_ref, cnt_ref):
    tm = x_ref.shape[0]
    d = d_ref[...]
    pooled = [jnp.dot(d[:, g * POOL_GC:(g + 1) * POOL_GC], wmap_ref[g], preferred_element_type=_F32)
              for g in range(POOL_GROUPS)]
    pool = (jnp.concatenate(pooled, axis=1) * pscale_ref[...]).astype(_BF)
    a = jnp.dot(attn_ref[...], wba_ref[...], preferred_element_type=_F32)
    pp = jnp.dot(pool, wbp_ref[...], preferred_element_type=_F32)
    m = jax.nn.sigmoid(ga_ref[...]) * a + jax.nn.sigmoid(gp_ref[...]) * pp
    x1 = x_ref[...] + jnp.dot(m.astype(_BF), wout_ref[...], preferred_element_type=_F32)
    x1_ref[...] = x1
    xn = _rms_scale(x1, g2_ref[...]).astype(_BF)
    xn_ref[...] = xn

    logit = jnp.dot(xn, wr_ref[...], preferred_element_type=_F32) + br_ref[...]
    lane = lax.broadcasted_iota(jnp.int32, (tm, LANES), 1).astype(_F32)
    far = float(LANES)

    def first_lane_of(hit):
        return jnp.min(jnp.where(hit, lane, far), axis=1, keepdims=True)

    lc = jnp.where(lane < N_GROUPS, logit, NEG_BIG)
    mc = jnp.max(lc, axis=1, keepdims=True)
    p_grp = 1.0 / jnp.sum(jnp.exp(lc - mc), axis=1, keepdims=True)
    grp = first_lane_of(lc == mc)
    lo = FINE_LANE0 + EXPERTS_PER_GROUP * grp
    in_grp = (lane >= lo) & (lane < lo + EXPERTS_PER_GROUP)
    lf = jnp.where(in_grp, logit, NEG_BIG)
    ef = jnp.exp(lf - jnp.max(lf, axis=1, keepdims=True))
    pf = jnp.where(in_grp, ef / jnp.sum(ef, axis=1, keepdims=True), -1.0)
    p1 = jnp.max(pf, axis=1, keepdims=True)
    l1 = first_lane_of(pf == p1)
    pf2 = jnp.where(lane == l1, -1.0, pf)
    p2 = jnp.max(pf2, axis=1, keepdims=True)
    l2 = first_lane_of(pf2 == p2)
    e1 = l1 - FINE_LANE0
    e2 = l2 - FINE_LANE0
    g1 = p_grp * p1 / (p1 + p2)
    g2 = p_grp * p2 / (p1 + p2)
    eid_ref[...] = jnp.where(lane == 0, e1, jnp.where(lane == 1, e2, 0.0)).astype(jnp.int32)
    gate_ref[...] = jnp.where(lane == 0, g1, jnp.where(lane == 1, g2, 0.0))

    onehot = jnp.where((lane == e1) | (lane == e2), 1.0, 0.0)
    before = jnp.dot(ltri_ref[...], onehot.astype(_BF), preferred_element_type=_F32)
    r1 = jnp.sum(jnp.where(lane == e1, before, 0.0), axis=1, keepdims=True)
    r2 = jnp.sum(jnp.where(lane == e2, before, 0.0), axis=1, keepdims=True)
    rank_ref[...] = jnp.where(lane == 0, r1, jnp.where(lane == 1, r2, 0.0)).astype(jnp.int32)
    cnt_ref[0] = jnp.broadcast_to(jnp.sum(onehot, axis=0, keepdims=True), (SUBLANES, LANES))


def _merge_route(x, attn, d, ga, gp, w):
    n = x.shape[0]
    tm = ROW_TILE
    assert n % tm == 0
    nt = n // tm
    row = lambda width: pl.BlockSpec((tm, width), lambda i: (i, 0))
    full = lambda a: _resident(a.shape, lambda i: (0,) * a.ndim)
    weights = (w["wmap"], w["pscale"], w["wba"], w["wbp"], w["wout"], w["g2"], w["wr"], w["br"],
               w["ltri"])
    return pl.pallas_call(
        _merge_route_kernel,
        grid=(nt,),
        in_specs=[row(D_MODEL), row(ATTN_W), row(POOL_W), row(D_MODEL), row(D_MODEL)]
                 + [full(a) for a in weights],
        out_specs=[row(D_MODEL), row(D_MODEL), row(LANES), row(LANES), row(LANES),
                   pl.BlockSpec((1, SUBLANES, LANES), lambda i: (i, 0, 0))],
        out_shape=[jax.ShapeDtypeStruct((n, D_MODEL), _F32), jax.ShapeDtypeStruct((n, D_MODEL), _BF),
                   jax.ShapeDtypeStruct((n, LANES), jnp.int32), jax.ShapeDtypeStruct((n, LANES), _F32),
                   jax.ShapeDtypeStruct((n, LANES), jnp.int32),
                   jax.ShapeDtypeStruct((nt, SUBLANES, LANES), _F32)],
        compiler_params=pltpu.CompilerParams(dimension_semantics=("arbitrary",),
                                             vmem_limit_bytes=VMEM_LIMIT),
        name="merge_norm_route",
    )(x, attn, d, ga, gp, *weights)


def _experts_kernel(be_ref, nu_ref, x_ref, wg_ref, wu_ref, wd_ref, y_ref):
    del be_ref
    i = pl.program_id(0)

    @pl.when(i < nu_ref[0])
    def _():
        x = x_ref[...]
        hg = jnp.dot(x, wg_ref[0], preferred_element_type=_F32)
        hu = jnp.dot(x, wu_ref[0], preferred_element_type=_F32)
        hdn = (hg * jax.nn.sigmoid(hg) * hu).astype(_BF)
        y_ref[...] = jnp.dot(hdn, wd_ref[0], preferred_element_type=_F32).astype(y_ref.dtype)

    @pl.when(i >= nu_ref[0])
    def _():
        y_ref[...] = jnp.zeros(y_ref.shape, y_ref.dtype)


def _experts(x_pad, block_expert, n_used, wg, wu, wd):
    nblk = block_expert.shape[0]
    bm = MOE_ROWS
    used = lambda i, be, nu: (jnp.minimum(i, jnp.maximum(nu[0] - 1, 0)), 0)
    grid_spec = pltpu.PrefetchScalarGridSpec(
        num_scalar_prefetch=2,
        grid=(nblk,),
        in_specs=[pl.BlockSpec((bm, D_MODEL), used),
                  pl.BlockSpec((1, D_MODEL, D_EXPERT), lambda i, be, nu: (be[i], 0, 0)),
                  pl.BlockSpec((1, D_MODEL, D_EXPERT), lambda i, be, nu: (be[i], 0, 0)),
                  pl.BlockSpec((1, D_EXPERT, D_MODEL), lambda i, be, nu: (be[i], 0, 0))],
        out_specs=pl.BlockSpec((bm, D_MODEL), lambda i, be, nu: (i, 0)),
    )
    return pl.pallas_call(
        _experts_kernel,
        grid_spec=grid_spec,
        out_shape=jax.ShapeDtypeStruct((nblk * bm, D_MODEL), _BF),
        compiler_params=pltpu.CompilerParams(dimension_semantics=("arbitrary",),
                                             vmem_limit_bytes=VMEM_LIMIT),
        name="grouped_swiglu_experts",
    )(block_expert, n_used, x_pad, wg, wu, wd)


def _route_layout(eid, rank, tile_cnt):
    n = eid.shape[0]
    bm = MOE_ROWS
    cnt = tile_cnt[:, 0, :N_EXPERTS].astype(jnp.int32)
    tile_start = jnp.cumsum(cnt, axis=0) - cnt
    counts = jnp.sum(cnt, axis=0)
    padded = ((counts + bm - 1) // bm) * bm
    pad_ends = jnp.cumsum(padded)
    pad_starts = pad_ends - padded
    base = jnp.repeat(tile_start + pad_starts[None, :], ROW_TILE, axis=0)
    onehot = eid[:, :, None] == jnp.arange(N_EXPERTS, dtype=jnp.int32)[None, None, :]
    dest = jnp.sum(jnp.where(onehot, base[:, None, :], 0), axis=-1) + rank
    nblk = (n * TOP_K_FINE) // bm + N_EXPERTS
    block_row0 = jnp.arange(nblk, dtype=jnp.int32) * bm
    block_expert = jnp.minimum(jnp.sum(pad_ends[None, :] <= block_row0[:, None], axis=1),
                               N_EXPERTS - 1).astype(jnp.int32)
    n_used = (pad_ends[-1] // bm).astype(jnp.int32).reshape(1)
    return dest, block_expert, n_used, nblk


def _combine_norm_kernel(x1_ref, y1_ref, y2_ref, gate_ref, g_ref, out_ref):
    gate = gate_ref[...]
    x2 = (x1_ref[...] + gate[:, 0:1] * y1_ref[...].astype(_F32)
          + gate[:, 1:2] * y2_ref[...].astype(_F32))
    out_ref[...] = _rms_scale(x2, g_ref[...])


def _combine_norm(x1, y1, y2, gate, g):
    n = x1.shape[0]
    tm = ROW_TILE
    row = lambda width: pl.BlockSpec((tm, width), lambda i: (i, 0))
    return pl.pallas_call(
        _combine_norm_kernel,
        grid=(n // tm,),
        in_specs=[row(D_MODEL), row(D_MODEL), row(D_MODEL), row(LANES),
                  _resident((1, D_MODEL), lambda i: (0, 0))],
        out_specs=row(D_MODEL),
        out_shape=jax.ShapeDtypeStruct((n, D_MODEL), _F32),
        compiler_params=pltpu.CompilerParams(dimension_semantics=("arbitrary",)),
        name="combine_final_norm",
    )(x1, y1, y2, gate, g.reshape(1, D_MODEL).astype(_F32))


def _experts_combine(xn, x1, eid, rank, gate, tile_cnt, expert_w, lnf_g):
    n = xn.shape[0]
    dest, block_expert, n_used, nblk = _route_layout(eid[:, :TOP_K_FINE], rank[:, :TOP_K_FINE],
                                                     tile_cnt)
    tok = jnp.broadcast_to(jnp.arange(n, dtype=jnp.int32)[:, None], dest.shape)
    slot_tok = jnp.full((nblk * MOE_ROWS,), n, jnp.int32).at[dest.reshape(-1)].set(tok.reshape(-1))
    x_pad = jnp.concatenate([xn, jnp.zeros((1, D_MODEL), _BF)], axis=0)[slot_tok]
    yb = _experts(x_pad, block_expert, n_used, *expert_w)
    return _combine_norm(x1, yb[dest[:, 0]], yb[dest[:, 1]], gate, lnf_g)


def kernel(x_prompt, x_sample, cache_k, cache_v, cache_kidx, state_pool, page_table, rel_bias,
           ln1_g, w_in, w_pool_map, pool_scale, w_br_attn, w_br_pool, w_out, ln2_g,
           w_coarse, b_coarse, w_fine, b_fine, w_gate, w_up, w_down, lnf_g):
    layer = 0
    nb, s = x_prompt.shape[:2]
    db, tn = x_sample.shape[:2]
    n_p, n_s = nb * s, db * tn
    rel_bias = rel_bias.astype(_F32)

    wr = jnp.zeros((D_MODEL, LANES), _F32)
    wr = wr.at[:, :N_GROUPS].set(w_coarse[layer])
    wr = wr.at[:, FINE_LANE0:FINE_LANE0 + N_EXPERTS].set(
        jnp.transpose(w_fine[layer], (1, 0, 2)).reshape(D_MODEL, N_EXPERTS))
    br = jnp.zeros((1, LANES), _F32)
    br = br.at[0, :N_GROUPS].set(b_coarse[layer])
    br = br.at[0, FINE_LANE0:FINE_LANE0 + N_EXPERTS].set(b_fine[layer].reshape(-1))
    mw = dict(
        wmap=w_pool_map[layer].astype(_BF), pscale=pool_scale[layer].reshape(1, POOL_W).astype(_F32),
        wba=w_br_attn[layer].astype(_BF), wbp=w_br_pool[layer].astype(_BF),
        wout=w_out[layer].astype(_BF), g2=ln2_g[layer].reshape(1, D_MODEL).astype(_F32),
        wr=wr.astype(_BF), br=br,
        ltri=(jnp.arange(ROW_TILE)[:, None] > jnp.arange(ROW_TILE)[None, :]).astype(_BF))

    pp = _project(x_prompt.reshape(n_p, D_MODEL), ln1_g[layer], w_in[layer], nb)
    attn_p = _prompt_attention(pp, rel_bias, nb, s)
    d_p = _pool_diff(pp["u"], nb, s, True)
    x1_p, xn_p, eid_p, gate_p, rank_p, cnt_p = _merge_route(
        x_prompt.reshape(n_p, D_MODEL), attn_p, d_p, pp["ga"], pp["gp"], mw)
    ew = (w_gate[layer].astype(_BF), w_up[layer].astype(_BF), w_down[layer].astype(_BF))
    y_prompt = _experts_combine(xn_p, x1_p, eid_p, rank_p, gate_p, cnt_p, ew, lnf_g)

    ps = _project(x_sample.reshape(n_s, D_MODEL), ln1_g[layer], w_in[layer], 1)
    q_s = ps["qT"].T.reshape(db, tn, ATTN_W)
    qi_s = ps["qiT"].T.reshape(db, tn, IDX_W)
    wi_s = ps["wiT"][:IDX_HEADS].T.reshape(db, tn, IDX_HEADS)
    k_s, v_s, ki_s = ps["kT32"][0].T, ps["vT32"][0].T, ps["kiT32"][0].T
    attn_s = _sample_attention(q_s, ps["k16"].reshape(db, tn, ATTN_W),
                               v_s.astype(_BF).reshape(db, tn, ATTN_W), qi_s,
                               ps["ki16"].reshape(db, tn, IDX_DIM), wi_s,
                               cache_k[layer], cache_v[layer], cache_kidx[layer], page_table, rel_bias)
    u_s = ps["u"].reshape(db, tn, POOL_W)
    buf = jnp.concatenate([jnp.zeros((db, 1, POOL_W), _F32), state_pool[layer].astype(_F32), u_s],
                          axis=1)
    grp_rows = 1 + POOL_CTX + tn
    d_s = _pool_diff(buf.reshape(db * grp_rows, POOL_W), 1, db * grp_rows, False)
    d_s = d_s.reshape(db, grp_rows, POOL_W)[:, 1 + POOL_CTX:].reshape(n_s, POOL_W)
    x1_s, xn_s, eid_s, gate_s, rank_s, cnt_s = _merge_route(
        x_sample.reshape(n_s, D_MODEL), attn_s.reshape(n_s, ATTN_W), d_s, ps["ga"], ps["gp"], mw)

    y_sample = _experts_combine(xn_s, x1_s, eid_s, rank_s, gate_s, cnt_s, ew, lnf_g)
    y_prompt = y_prompt.reshape(nb, s, D_MODEL)
    y_sample = y_sample.reshape(db, tn, D_MODEL)
    head = lambda a, n, t: a.reshape(1, n, t, N_HEADS, HEAD_DIM)
    heads_kminor = lambda a: jnp.transpose(a.reshape(nb, N_HEADS, HEAD_DIM, s), (0, 3, 1, 2))[None]
    return (y_prompt, y_sample,
            heads_kminor(pp["kT32"]), heads_kminor(pp["vT32"]), jnp.swapaxes(pp["kiT32"], 1, 2)[None],
            pp["u"].reshape(nb, s, POOL_W)[None, :, -POOL_CTX:],
            head(k_s, db, tn), head(v_s, db, tn), ki_s.reshape(1, db, tn, IDX_DIM),
            buf[None, :, -POOL_CTX:])
```
